```python
import jax
import jax.numpy as jnp
from jax import lax
import numpy as np

D_MODEL = 2048
BATCH = 8
SEQ = 4096
DEPTH = 2

D_MIX = D_MODEL
N_GROUPS = 4
GROUP_W = D_MIX // N_GROUPS
HG_HEADS = 4
HG_DIM = GROUP_W // HG_HEADS
HG_CHUNK = 64
RW_HEAD = 64
RW_HEADS = GROUP_W // RW_HEAD
RW_W_LORA = 32
RW_A_LORA = 32
RW_G_LORA = 96
RW_LN_EPS = 64e-5
ML_HEADS = 4
ML_DIM = GROUP_W // ML_HEADS
ML_CHUNK = 64
ML_CONV = 4
ML_QK_BLOCK = 4
NEG_BIG = -1e30
LRU_BLOCKS = 4
LRU_BLOCK = GROUP_W // LRU_BLOCKS
LRU_C = 8.0
LRU_CONV = 4
D_FF = 5632
N_SUB = 3
NORM_EPS = 1e-6
HG_IN = 4 * GROUP_W
RW_IN = 3 * GROUP_W + RW_W_LORA + RW_A_LORA + RW_G_LORA
ML_IN = 3 * GROUP_W + 2 * ML_HEADS
LRU_IN = 2 * GROUP_W
D_IN = HG_IN + RW_IN + ML_IN + LRU_IN

kernel_name = "hymba_style_hgrn2_rwkv7_mlstm_rglru_macaron_adaln"


def _split(a, sizes):
    return jnp.split(a, np.cumsum(sizes)[:-1].tolist(), axis=-1)


def _rmsnorm(x, w, eps=NORM_EPS):
    x32 = x.astype(jnp.float32)
    y = x32 * lax.rsqrt(jnp.mean(x32 * x32, axis=-1, keepdims=True) + eps)
    return (y * w.astype(jnp.float32)).astype(x.dtype)


def _head_rmsnorm(x, w, n_heads, eps=NORM_EPS):
    shp = x.shape
    xh = x.astype(jnp.float32).reshape(*shp[:-1], n_heads, shp[-1] // n_heads)
    xh = xh * lax.rsqrt(jnp.mean(xh * xh, axis=-1, keepdims=True) + eps)
    return xh.reshape(shp) * w.astype(jnp.float32)


def _modulate(x, gain, shift, scale):
    return _rmsnorm(x, gain) * (1.0 + scale[:, None, :]) + shift[:, None, :]


def _swiglu(h, w1, w3, w2):
    return (jax.nn.silu(h @ w1) * (h @ w3)) @ w2


def _token_shift(u):
    return jnp.pad(u[:, :-1], ((0, 0), (1, 0), (0, 0)))


def _causal_dwconv(x, w, b):
    width = w.shape[0]
    y = lax.conv_general_dilated(
        x, w[:, None, :].astype(x.dtype), window_strides=(1,), padding=[(width - 1, 0)],
        dimension_numbers=("NWC", "WIO", "NWC"), feature_group_count=x.shape[-1])
    return y + b


def _to_chunks(a, chunk):
    bsz, seqlen = a.shape[:2]
    a = a.reshape(bsz, seqlen // chunk, chunk, *a.shape[2:])
    return a.transpose((1, 0, 3, 2) + tuple(range(4, a.ndim)))


def _from_chunks(a):
    nc, bsz, nh, chunk = a.shape[:4]
    a = a.transpose((1, 0, 3, 2) + tuple(range(4, a.ndim)))
    return a.reshape(bsz, nc * chunk, nh, *a.shape[4:])


def _gla_chunked(q, k, v, log_f):
    bsz, _, nh, dk = q.shape
    dv = v.shape[-1]
    causal = jnp.tril(jnp.ones((HG_CHUNK, HG_CHUNK), bool))[:, :, None]

    def step(S, inp):
        q_c, k_c, v_c, g_c = inp
        b = jnp.cumsum(g_c, axis=2)
        diff = b[:, :, :, None, :] - b[:, :, None, :, :]
        decay = jnp.where(causal, jnp.exp(jnp.where(causal, diff, 0.0)), 0.0)
        scores = jnp.einsum("bhtk,bhtsk,bhsk->bhts", q_c, decay, k_c)
        o = (jnp.einsum("bhts,bhsv->bhtv", scores, v_c)
             + jnp.einsum("bhtk,bhkv->bhtv", q_c * jnp.exp(b), S))
        b_last = b[:, :, -1:, :]
        S = (jnp.exp(b_last[:, :, 0, :])[..., None] * S
             + jnp.einsum("bhsk,bhsv->bhkv", k_c * jnp.exp(b_last - b), v_c))
        return S, o

    S0 = jnp.zeros((bsz, nh, dk, dv), jnp.float32)
    _, o = lax.scan(step, S0, tuple(_to_chunks(a, HG_CHUNK) for a in (q, k, v, log_f)))
    return _from_chunks(o)


def _hgrn2(u, lb, g_norm):
    bsz, seqlen, _ = u.shape
    q_raw, f_raw, i_raw, g_raw = _split(u, (GROUP_W,) * 4)
    heads = lambda a: a.reshape(bsz, seqlen, HG_HEADS, HG_DIM)
    q = jax.nn.silu(q_raw)
    log_f = jnp.log(lb + (1.0 - lb) * jax.nn.sigmoid(f_raw))
    k = (1.0 - lb) * jax.nn.sigmoid(-f_raw)
    o = _gla_chunked(heads(q), heads(k), heads(i_raw), heads(log_f))
    o = _head_rmsnorm(o.reshape(bsz, seqlen, GROUP_W), g_norm, HG_HEADS)
    return o * jax.nn.silu(g_raw)


def _wkv7_scan(r, w, k, v, a, b):
    bsz, _, nh, n = r.shape

    def step(S, inp):
        r_t, w_t, k_t, v_t, a_t, b_t = inp
        sa = jnp.einsum("bhvk,bhk->bhv", S, a_t)
        S = (S * w_t[:, :, None, :] + sa[..., None] * b_t[:, :, None, :]
             + v_t[..., None] * k_t[:, :, None, :])
        return S, jnp.einsum("bhvk,bhk->bhv", S, r_t)

    S0 = jnp.zeros((bsz, nh, n, n), jnp.float32)
    _, o = lax.scan(step, S0, tuple(jnp.moveaxis(t, 1, 0) for t in (r, w, k, v, a, b)))
    return jnp.moveaxis(o, 0, 1)


def _rwkv7(u, mu, w0, w2, a0, a2, g2, k_k, k_a, r_k, ln_w, ln_b):
    bsz, seqlen, _ = u.shape
    u = u + (_token_shift(u) - u) * mu
    r, k, v, wd, ad, gd = _split(u, (GROUP_W, GROUP_W, GROUP_W, RW_W_LORA, RW_A_LORA, RW_G_LORA))
    w_log = -jax.nn.softplus(-(w0 + jnp.tanh(wd) @ w2)) - 0.5
    decay = jnp.exp(-jnp.exp(w_log))
    a = jax.nn.sigmoid(a0 + ad @ a2)
    g = jax.nn.sigmoid(gd) @ g2
    heads = lambda t: t.reshape(bsz, seqlen, RW_HEADS, RW_HEAD)
    kk = heads(k * k_k)
    kk = kk / jnp.maximum(jnp.sqrt(jnp.sum(kk * kk, axis=-1, keepdims=True) + 1e-12), 1e-6)
    k = k * (1.0 + (a - 1.0) * k_a)
    rh, kh, vh = heads(r), heads(k), heads(v)
    o = _wkv7_scan(rh, heads(decay), kh, vh, -kk, kk * heads(a))
    mean = jnp.mean(o, axis=-1, keepdims=True)
    var = jnp.mean(jnp.square(o - mean), axis=-1, keepdims=True)
    o = ((o - mean) * lax.rsqrt(var + RW_LN_EPS)).reshape(bsz, seqlen, GROUP_W) * ln_w + ln_b
    bonus = jnp.sum(rh * kh * r_k, axis=-1, keepdims=True) * vh
    return (o + bonus.reshape(bsz, seqlen, GROUP_W)) * g


def _mlstm_chunked(q, k, v, i_pre, log_f):
    bsz, _, nh, dk = q.shape
    dv = v.shape[-1]
    causal = jnp.tril(jnp.ones((ML_CHUNK, ML_CHUNK), bool))

    def step(carry, inp):
        M, n, m = carry
        q_c, k_c, v_c, i_c, f_c = inp
        b = jnp.cumsum(f_c, axis=-1)
        lw = b[..., :, None] - b[..., None, :] + i_c[..., None, :]
        lprev = b + m[..., None]
        m_t = jnp.maximum(lprev, jnp.max(jnp.where(causal, lw, NEG_BIG), axis=-1))
        wts = jnp.where(causal, jnp.exp(jnp.where(causal, lw - m_t[..., None], 0.0)), 0.0)
        s = jnp.einsum("bhtd,bhsd->bhts", q_c, k_c) * wts
        wp = jnp.exp(lprev - m_t)
        num = jnp.einsum("bhts,bhsv->bhtv", s, v_c) + wp[..., None] * jnp.einsum("bhtk,bhkv->bhtv", q_c, M)
        den = jnp.sum(s, axis=-1) + wp * jnp.einsum("bhtk,bhk->bht", q_c, n)
        h = num / jnp.maximum(jnp.abs(den), jnp.exp(-m_t))[..., None]
        m_new = m_t[..., -1]
        wl = jnp.exp(b[..., -1:] - b + i_c - m_new[..., None])
        dec = jnp.exp(b[..., -1] + m - m_new)
        M = dec[..., None, None] * M + jnp.einsum("bhs,bhsk,bhsv->bhkv", wl, k_c, v_c)
        n = dec[..., None] * n + jnp.einsum("bhs,bhsk->bhk", wl, k_c)
        return (M, n, m_new), h

    carry0 = (jnp.zeros((bsz, nh, dk, dv), jnp.float32), jnp.zeros((bsz, nh, dk), jnp.float32),
              jnp.zeros((bsz, nh), jnp.float32))
    _, h = lax.scan(step, carry0, tuple(_to_chunks(a, ML_CHUNK) for a in (q, k, v, i_pre, log_f)))
    return _from_chunks(h)


def _mlstm(u, conv_w, conv_b, wq, wk, i_b, f_b, norm_w, skip):
    bsz, seqlen, _ = u.shape
    xm, v, o_raw, i_raw, f_raw = _split(u, (GROUP_W, GROUP_W, GROUP_W, ML_HEADS, ML_HEADS))
    xc = jax.nn.silu(_causal_dwconv(xm, conv_w, conv_b))
    blocks = xc.reshape(bsz, seqlen, GROUP_W // ML_QK_BLOCK, ML_QK_BLOCK)
    q = jnp.einsum("btnj,nji->btni", blocks, wq).reshape(bsz, seqlen, ML_HEADS, ML_DIM) * ML_DIM ** -0.5
    k = jnp.einsum("btnj,nji->btni", blocks, wk).reshape(bsz, seqlen, ML_HEADS, ML_DIM)
    h = _mlstm_chunked(q, k, v.reshape(bsz, seqlen, ML_HEADS, ML_DIM),
                       i_raw + i_b, jax.nn.log_sigmoid(f_raw + f_b))
    h = jax.nn.sigmoid(o_raw) * h.reshape(bsz, seqlen, GROUP_W)
    return _head_rmsnorm(h, norm_w, ML_HEADS) + skip * xc


def _rglru(u, conv_w, conv_b, wa, ba, wx, bx, lam, norm_w):
    bsz, seqlen, _ = u.shape
    xb, gb = _split(u, (GROUP_W, GROUP_W))
    xc = _causal_dwconv(xb, conv_w, conv_b)
    blocks = xc.reshape(bsz, seqlen, LRU_BLOCKS, LRU_BLOCK)
    r = jax.nn.sigmoid(jnp.einsum("btni,nij->btnj", blocks, wa).reshape(bsz, seqlen, GROUP_W) + ba)
    i = jax.nn.sigmoid(jnp.einsum("btni,nij->btnj", blocks, wx).reshape(bsz, seqlen, GROUP_W) + bx)
    log_a = -LRU_C * r * jax.nn.softplus(-lam)
    a = jnp.exp(log_a)
    bterm = jnp.sqrt(-jnp.expm1(2.0 * log_a)) * (i * xc)

    def combine(p, s):
        return p[0] * s[0], s[0] * p[1] + s[1]

    _, h = lax.associative_scan(combine, (a, bterm), axis=1)
    return _rmsnorm(h * jax.nn.gelu(gb), norm_w)


def setup_inputs(seed: int = 0) -> dict:
    key = jax.random.key(seed)
    ks = iter(jax.random.split(key, 64))

    def nrm(shape, scale):
        return scale * jax.random.normal(next(ks), shape, jnp.float32)

    chan = jnp.arange(GROUP_W, dtype=jnp.float32) / (GROUP_W - 1)
    a8 = jax.random.uniform(next(ks), (DEPTH, GROUP_W), jnp.float32, 0.9, 0.999)
    a_lru = a8 ** (1.0 / LRU_C)
    return {
        "x": nrm((BATCH, SEQ, D_MODEL), 1.0),
        "c": nrm((BATCH, D_MODEL), 1.0),
        "norm_gain": 1.0 + nrm((DEPTH, N_SUB, D_MODEL), 0.02),
        "mod_w": nrm((DEPTH, D_MODEL, N_SUB * 3 * D_MODEL), 0.2 * D_MODEL ** -0.5),
        "mod_b": nrm((DEPTH, N_SUB * 3 * D_MODEL), 0.02),
        "ffn_w1": nrm((DEPTH, 2, D_MODEL, D_FF), D_MODEL ** -0.5),
        "ffn_w3": nrm((DEPTH, 2, D_MODEL, D_FF), D_MODEL ** -0.5),
        "ffn_w2": nrm((DEPTH, 2, D_FF, D_MODEL), D_FF ** -0.5),
        "w_in": nrm((DEPTH, D_MODEL, D_IN), D_MODEL ** -0.5),
        "w_out": nrm((DEPTH, D_MIX, D_MODEL), D_MIX ** -0.5),
        "hg_lb_logits": nrm((DEPTH, GROUP_W), 0.5),
        "hg_norm": 1.0 + nrm((DEPTH, GROUP_W), 0.02),
        "rw_mu": jax.random.uniform(next(ks), (DEPTH, RW_IN), jnp.float32, 0.0, 1.0),
        "rw_w0": (-6.5 + 5.0 * chan ** 0.85) + nrm((DEPTH, GROUP_W), 0.1),
        "rw_w2": nrm((DEPTH, RW_W_LORA, GROUP_W), 0.1),
        "rw_a0": nrm((DEPTH, GROUP_W), 0.1),
        "rw_a2": nrm((DEPTH, RW_A_LORA, GROUP_W), 0.1),
        "rw_g2": nrm((DEPTH, RW_G_LORA, GROUP_W), RW_G_LORA ** -0.5),
        "rw_kk": 0.85 + nrm((DEPTH, GROUP_W), 0.02),
        "rw_ka": 1.0 + nrm((DEPTH, GROUP_W), 0.02),
        "rw_rk": -0.04 + nrm((DEPTH, RW_HEADS, RW_HEAD), 0.02),
        "rw_ln_w": 1.0 + nrm((DEPTH, GROUP_W), 0.02),
        "rw_ln_b": nrm((DEPTH, GROUP_W), 0.02),
        "ml_conv_w": nrm((DEPTH, ML_CONV, GROUP_W), ML_CONV ** -0.5),
        "ml_conv_b": nrm((DEPTH, GROUP_W), 0.02),
        "ml_wq": nrm((DEPTH, GROUP_W // ML_QK_BLOCK, ML_QK_BLOCK, ML_QK_BLOCK), ML_QK_BLOCK ** -0.5),
        "ml_wk": nrm((DEPTH, GROUP_W // ML_QK_BLOCK, ML_QK_BLOCK, ML_QK_BLOCK), ML_QK_BLOCK ** -0.5),
        "ml_i_b": nrm((DEPTH, ML_HEADS), 0.1),
        "ml_f_b": jnp.linspace(3.0, 6.0, ML_HEADS, dtype=jnp.float32) + nrm((DEPTH, ML_HEADS), 0.1),
        "ml_norm": 1.0 + nrm((DEPTH, GROUP_W), 0.02),
        "ml_skip": 1.0 + nrm((DEPTH, GROUP_W), 0.02),
        "lru_conv_w": nrm((DEPTH, LRU_CONV, GROUP_W), LRU_CONV ** -0.5),
        "lru_conv_b": nrm((DEPTH, GROUP_W), 0.02),
        "lru_wa": nrm((DEPTH, LRU_BLOCKS, LRU_BLOCK, LRU_BLOCK), LRU_BLOCK ** -0.5),
        "lru_ba": nrm((DEPTH, GROUP_W), 0.02),
        "lru_wx": nrm((DEPTH, LRU_BLOCKS, LRU_BLOCK, LRU_BLOCK), LRU_BLOCK ** -0.5),
        "lru_bx": nrm((DEPTH, GROUP_W), 0.02),
        "lru_lambda": jnp.log(a_lru) - jnp.log1p(-a_lru),
        "lru_norm": 1.0 + nrm((DEPTH, GROUP_W), 0.02),
        "final_norm": 1.0 + nrm((D_MODEL,), 0.02),
    }


def reference(x, c, norm_gain, mod_w, mod_b, ffn_w1, ffn_w3, ffn_w2, w_in, w_out,
              hg_lb_logits, hg_norm, rw_mu, rw_w0, rw_w2, rw_a0, rw_a2, rw_g2, rw_kk, rw_ka,
              rw_rk, rw_ln_w, rw_ln_b, ml_conv_w, ml_conv_b, ml_wq, ml_wk, ml_i_b, ml_f_b,
              ml_norm, ml_skip, lru_conv_w, lru_conv_b, lru_wa, lru_ba, lru_wx, lru_bx,
              lru_lambda, lru_norm, final_norm):
    lb_w = jax.nn.softmax(hg_lb_logits.astype(jnp.float32), axis=0)
    lower_bounds = jnp.cumsum(lb_w, axis=0) - lb_w[0]
    c_act = jax.nn.silu(c)
    for l in range(DEPTH):
        mod = (c_act @ mod_w[l] + mod_b[l]).reshape(c.shape[0], N_SUB, 3, D_MODEL)
        shift, scale, gate = mod[:, :, 0], mod[:, :, 1], mod[:, :, 2]

        h = _modulate(x, norm_gain[l, 0], shift[:, 0], scale[:, 0])
        x = x + 0.5 * (1.0 + gate[:, 0, None, :]) * _swiglu(h, ffn_w1[l, 0], ffn_w3[l, 0], ffn_w2[l, 0])

        h = _modulate(x, norm_gain[l, 1], shift[:, 1], scale[:, 1])
        u = (h @ w_in[l]).astype(jnp.float32)
        u_hg, u_rw, u_ml, u_lru = _split(u, (HG_IN, RW_IN, ML_IN, LRU_IN))
        y_hg = _hgrn2(u_hg, lower_bounds[l], hg_norm[l])
        y_rw = _rwkv7(u_rw, rw_mu[l], rw_w0[l], rw_w2[l], rw_a0[l], rw_a2[l], rw_g2[l],
                      rw_kk[l], rw_ka[l], rw_rk[l], rw_ln_w[l], rw_ln_b[l])
        y_ml = _mlstm(u_ml, ml_conv_w[l], ml_conv_b[l], ml_wq[l], ml_wk[l], ml_i_b[l],
                      ml_f_b[l], ml_norm[l], ml_skip[l])
        y_lru = _rglru(u_lru, lru_conv_w[l], lru_conv_b[l], lru_wa[l], lru_ba[l], lru_wx[l],
                       lru_bx[l], lru_lambda[l], lru_norm[l])
        y = jnp.concatenate([y_hg, y_rw, y_ml, y_lru], axis=-1).astype(x.dtype)
        x = x + (1.0 + gate[:, 1, None, :]) * (y @ w_out[l])

        h = _modulate(x, norm_gain[l, 2], shift[:, 2], scale[:, 2])
        x = x + 0.5 * (1.0 + gate[:, 2, None, :]) * _swiglu(h, ffn_w1[l, 1], ffn_w3[l, 1], ffn_w2[l, 1])
    return _rmsnorm(x, final_norm)
```

```python
import functools

import jax
import jax.numpy as jnp
from jax import lax
from jax.experimental import pallas as pl
from jax.experimental.pallas import tpu as pltpu

F32 = jnp.float32
BF16 = jnp.bfloat16

D_MODEL = 2048
DEPTH = 2
GROUP_W = 512
N_SUB = 3
D_FF = 5632
NORM_EPS = 1e-6
HG_HEADS = 4
RW_HEAD = 64
RW_HEADS = 8
RW_W_LORA = 32
RW_A_LORA = 32
RW_G_LORA = 96
RW_LN_EPS = 64e-5
RW_IN = 3 * GROUP_W + RW_W_LORA + RW_A_LORA + RW_G_LORA
ML_HEADS = 4
ML_DIM = 128
ML_IN = 3 * GROUP_W + 2 * ML_HEADS
NEG_BIG = -1e30
LRU_C = 8.0
HG_IN = 4 * GROUP_W
LRU_IN = 2 * GROUP_W

LANES = 128
CHUNK = 64
SUB = 16
RW_PAD = 1792
ML_PAD = 1664
U_HG = 0
U_RW = U_HG + HG_IN
U_ML = U_RW + RW_PAD
U_LRU = U_ML + ML_PAD
U_TOT = 6656
LORA_PAD = RW_PAD - 3 * GROUP_W
VMEM_LIMIT = 56 * 1024 * 1024


def _silu(x):
    return x * jax.nn.sigmoid(x)


def _softplus(x):
    return jnp.maximum(x, 0.0) + jnp.log(1.0 + jnp.exp(-jnp.abs(x)))


def _dot(a, b):
    return jnp.dot(a.astype(BF16), b.astype(BF16), preferred_element_type=F32)


def _dot_nt(a, b):
    return lax.dot_general(a.astype(BF16), b.astype(BF16), (((1,), (1,)), ((), ())),
                           preferred_element_type=F32)


def _dot_tn(a, b):
    return lax.dot_general(a.astype(BF16), b.astype(BF16), (((0,), (0,)), ((), ())),
                           preferred_element_type=F32)


def _split3(x):
    hi = x.astype(BF16)
    r1 = x - hi.astype(F32)
    mid = r1.astype(BF16)
    lo = (r1 - mid.astype(F32)).astype(BF16)
    return hi, mid, lo


def _cumsum_rows(x, tri):
    hi, mid, lo = _split3(x)
    d = functools.partial(jnp.dot, preferred_element_type=F32)
    return d(tri, hi) + d(tri, mid) + d(tri, lo)


def _seg_dot(x, seg):
    hi, mid, lo = _split3(x)
    d = functools.partial(jnp.dot, preferred_element_type=F32)
    return d(hi, seg) + d(mid, seg) + d(lo, seg)


def _shift_rows(x, prev, d):
    rows = lax.broadcasted_iota(jnp.int32, x.shape, 0)
    return jnp.where(rows < d, pltpu.roll(prev, d, 0), pltpu.roll(x, d, 0))


def _modulated_norm(x, gain, shift, scale):
    y = x * lax.rsqrt(jnp.mean(x * x, axis=-1, keepdims=True) + NORM_EPS) * gain
    return y * (1.0 + scale) + shift


def _mod_kernel(c_ref, w_ref, b_ref, o_ref):
    o_ref[...] = _dot(_silu(c_ref[...]), w_ref[...]) + b_ref[...]


def _mod_call(c, mod_w, mod_b):
    depth, d, n = mod_w.shape
    bsz = c.shape[0]
    tn = 1024
    return pl.pallas_call(
        _mod_kernel,
        grid=(depth, n // tn),
        in_specs=[pl.BlockSpec((bsz, d), lambda l, j: (0, 0)),
                  pl.BlockSpec((None, d, tn), lambda l, j: (l, 0, j)),
                  pl.BlockSpec((None, 1, tn), lambda l, j: (l, 0, j))],
        out_specs=pl.BlockSpec((None, bsz, tn), lambda l, j: (l, 0, j)),
        out_shape=jax.ShapeDtypeStruct((depth, bsz, n), F32),
        compiler_params=pltpu.CompilerParams(
            dimension_semantics=("arbitrary", "arbitrary"), vmem_limit_bytes=VMEM_LIMIT),
        name="adaln_mod",
    )(c, mod_w, mod_b.reshape(depth, 1, n))


def _ffn_kernel(x_ref, gain_ref, shift_ref, scale_ref, gate_ref, w1_ref, w3_ref, w2_ref, *rest,
                n_ff, final):
    if final:
        fgain_ref, o_ref, h_ref, acc_ref = rest
    else:
        o_ref, h_ref, acc_ref = rest
    j = pl.program_id(2)

    @pl.when(j == 0)
    def _():
        h = _modulated_norm(x_ref[...], gain_ref[...], shift_ref[...], scale_ref[...])
        h_ref[...] = h.astype(BF16)
        acc_ref[...] = jnp.zeros_like(acc_ref)

    h = h_ref[...]
    a = jnp.dot(h, w1_ref[...], preferred_element_type=F32)
    b = jnp.dot(h, w3_ref[...], preferred_element_type=F32)
    g = (_silu(a) * b).astype(BF16)
    acc_ref[...] += jnp.dot(g, w2_ref[...], preferred_element_type=F32)

    @pl.when(j == n_ff - 1)
    def _():
        xn = x_ref[...] + (0.5 * (1.0 + gate_ref[...])) * acc_ref[...]
        if final:
            xn = xn * lax.rsqrt(jnp.mean(xn * xn, axis=-1, keepdims=True) + NORM_EPS) * fgain_ref[...]
        o_ref[...] = xn


def _ffn_call(x, gain, mod4, sub, w1, w3, w2, final_gain=None, tm=512, tf=512):
    bsz, seqlen, d = x.shape
    tm = min(tm, seqlen)
    n_ff = w1.shape[1] // tf
    final = final_gain is not None
    vec = lambda k: pl.BlockSpec((None, None, 1, d), lambda b, i, j: (b, 3 * sub + k, 0, 0))
    in_specs = [pl.BlockSpec((None, tm, d), lambda b, i, j: (b, i, 0)),
                pl.BlockSpec((1, d), lambda b, i, j: (0, 0)),
                vec(0), vec(1), vec(2),
                pl.BlockSpec((d, tf), lambda b, i, j: (0, j)),
                pl.BlockSpec((d, tf), lambda b, i, j: (0, j)),
                pl.BlockSpec((tf, d), lambda b, i, j: (j, 0))]
    args = [x, gain.reshape(1, d), mod4, mod4, mod4, w1, w3, w2]
    if final:
        in_specs.append(pl.BlockSpec((1, d), lambda b, i, j: (0, 0)))
        args.append(final_gain.reshape(1, d))
    return pl.pallas_call(
        functools.partial(_ffn_kernel, n_ff=n_ff, final=final),
        grid=(bsz, seqlen // tm, n_ff),
        in_specs=in_specs,
        out_specs=pl.BlockSpec((None, tm, d), lambda b, i, j: (b, i, 0)),
        out_shape=jax.ShapeDtypeStruct(x.shape, F32),
        scratch_shapes=[pltpu.VMEM((tm, d), BF16), pltpu.VMEM((tm, d), F32)],
        compiler_params=pltpu.CompilerParams(
            dimension_semantics=("parallel", "parallel", "arbitrary"), vmem_limit_bytes=VMEM_LIMIT),
        name="ffn_final" if final else "ffn",
    )(*args)


def _proj_kernel(x_ref, gain_ref, shift_ref, scale_ref, w_ref, o_ref, h_ref):
    @pl.when(pl.program_id(2) == 0)
    def _():
        h = _modulated_norm(x_ref[...], gain_ref[...], shift_ref[...], scale_ref[...])
        h_ref[...] = h.astype(BF16)

    o_ref[...] = jnp.dot(h_ref[...], w_ref[...], preferred_element_type=F32)


def _proj_call(x, gain, mod4, sub, w, tm=1024, tn=512):
    bsz, seqlen, d = x.shape
    tm = min(tm, seqlen)
    n = w.shape[1]
    vec = lambda k: pl.BlockSpec((None, None, 1, d), lambda b, i, j: (b, 3 * sub + k, 0, 0))
    return pl.pallas_call(
        _proj_kernel,
        grid=(bsz, seqlen // tm, n // tn),
        in_specs=[pl.BlockSpec((None, tm, d), lambda b, i, j: (b, i, 0)),
                  pl.BlockSpec((1, d), lambda b, i, j: (0, 0)),
                  vec(0), vec(1),
                  pl.BlockSpec((d, tn), lambda b, i, j: (0, j))],
        out_specs=pl.BlockSpec((None, tm, tn), lambda b, i, j: (b, i, j)),
        out_shape=jax.ShapeDtypeStruct((bsz, seqlen, n), F32),
        scratch_shapes=[pltpu.VMEM((tm, d), BF16)],
        compiler_params=pltpu.CompilerParams(
            dimension_semantics=("parallel", "parallel", "arbitrary"), vmem_limit_bytes=VMEM_LIMIT),
        name="in_proj",
    )(x, gain.reshape(1, d), mod4, mod4, w)


def _outproj_kernel(x_ref, y_ref, gate_ref, w_ref, o_ref):
    o_ref[...] = x_ref[...] + (1.0 + gate_ref[...]) * jnp.dot(
        y_ref[...], w_ref[...], preferred_element_type=F32)


def _outproj_call(x, y, mod4, sub, w, tm=512):
    bsz, seqlen, d = x.shape
    tm = min(tm, seqlen)
    return pl.pallas_call(
        _outproj_kernel,
        grid=(bsz, seqlen // tm),
        in_specs=[pl.BlockSpec((None, tm, d), lambda b, i: (b, i, 0)),
                  pl.BlockSpec((None, tm, y.shape[-1]), lambda b, i: (b, i, 0)),
                  pl.BlockSpec((None, None, 1, d), lambda b, i: (b, 3 * sub + 2, 0, 0)),
                  pl.BlockSpec(w.shape, lambda b, i: (0, 0))],
        out_specs=pl.BlockSpec((None, tm, d), lambda b, i: (b, i, 0)),
        out_shape=jax.ShapeDtypeStruct(x.shape, F32),
        compiler_params=pltpu.CompilerParams(
            dimension_semantics=("parallel", "parallel"), vmem_limit_bytes=VMEM_LIMIT),
        name="out_proj",
    )(x, y, mod4, w)


def _hgrn2_chunk(u_ref, lb, gnorm, tri, st_ref, y_ref):
    gw = GROUP_W
    q = _silu(u_ref[:, U_HG:U_HG + gw])
    f_raw = u_ref[:, U_HG + gw:U_HG + 2 * gw]
    v = u_ref[:, U_HG + 2 * gw:U_HG + 3 * gw]
    g_raw = u_ref[:, U_HG + 3 * gw:U_HG + 4 * gw]
    log_f = jnp.log(lb + (1.0 - lb) * jax.nn.sigmoid(f_raw))
    k = (1.0 - lb) * jax.nn.sigmoid(-f_raw)
    bcum = _cumsum_rows(log_f, tri)
    sub_row = lax.broadcasted_iota(jnp.int32, (SUB, 1), 0)
    for h in range(HG_HEADS):
        sl = slice(h * LANES, (h + 1) * LANES)
        qh, kh, vh, bh = q[:, sl], k[:, sl], v[:, sl], bcum[:, sl]
        st = st_ref[h]
        o_inter = _dot_nt(qh * jnp.exp(bh), st)
        pieces = []
        for i in range(CHUNK // SUB):
            r0 = i * SUB
            qd, kd, vd, bd = qh[r0:r0 + SUB], kh[r0:r0 + SUB], vh[r0:r0 + SUB], bh[r0:r0 + SUB]
            oi = jnp.zeros((SUB, LANES), F32)
            if i > 0:
                bref = bh[r0 - 1:r0]
                qi = qd * jnp.exp(bd - bref)
                kp = kh[0:r0] * jnp.exp(bref - bh[0:r0])
                oi = _dot(_dot_nt(qi, kp), vh[0:r0])
            for j in range(SUB):
                w = jnp.exp(jnp.minimum(bd - bd[j:j + 1], 0.0))
                col = jnp.sum(qd * w * kd[j:j + 1], axis=-1, keepdims=True)
                col = jnp.where(sub_row >= j, col, 0.0)
                oi = oi + col * vd[j:j + 1]
            pieces.append(oi)
        o = o_inter + jnp.concatenate(pieces, axis=0)
        b_last = bh[CHUNK - 1:CHUNK]
        st_ref[h] = st * jnp.exp(b_last) + _dot_tn(vh, kh * jnp.exp(b_last - bh))
        on = o * lax.rsqrt(jnp.mean(o * o, axis=-1, keepdims=True) + NORM_EPS) * gnorm[:, sl]
        y_ref[:, sl] = (on * _silu(g_raw[:, sl])).astype(y_ref.dtype)


def _rwkv7_chunk(u_ref, p, tri, prev_ref, ht_ref, o_scr, y_ref):
    gw = GROUP_W
    u_raw = u_ref[:, U_RW:U_RW + RW_PAD]
    prev = prev_ref[...]
    prev_ref[...] = u_raw
    u = u_raw + (_shift_rows(u_raw, prev, 1) - u_raw) * p["mu"][...]
    r, k, v, lora = u[:, 0:gw], u[:, gw:2 * gw], u[:, 2 * gw:3 * gw], u[:, 3 * gw:RW_PAD]
    zw = p["w0"][...] + _dot(jnp.tanh(lora), p["w2"][...])
    lw = -jnp.exp(-_softplus(-zw) - 0.5)
    iclr = jax.nn.sigmoid(p["a0"][...] + _dot(lora, p["a2"][...]))
    gate = _dot(jax.nn.sigmoid(lora), p["g2"][...])
    kk = k * p["kk"][...]
    ss = _seg_dot(kk * kk, p["seg1"][...])
    kk = kk / jnp.maximum(jnp.sqrt(ss + 1e-12), 1e-6)
    k = k * (1.0 + (iclr - 1.0) * p["ka"][...])
    a_vec = -kk
    b_vec = kk * iclr

    lc = _cumsum_rows(lw, tri)
    l_last = lc[CHUNK - 1:CHUNK]
    g_in = jnp.exp(lc)
    g_out = jnp.exp(-lc)
    g_end = jnp.exp(l_last - lc)
    a_s = a_vec * jnp.exp(lc - lw)
    r_s = r * g_in
    b_s = b_vec * g_out
    k_s = k * g_out
    b_e = b_vec * g_end
    k_e = k * g_end
    decay_end = jnp.exp(l_last)

    lane = lax.broadcasted_iota(jnp.int32, (1, LANES), 1)
    m_a = (lane < RW_HEAD).astype(F32)
    m_b = 1.0 - m_a
    stack = lambda t: jnp.concatenate([t * m_a, t * m_b], axis=0)
    n2 = 2 * CHUNK
    row = lax.broadcasted_iota(jnp.int32, (n2, n2), 0)
    col = lax.broadcasted_iota(jnp.int32, (n2, n2), 1)
    same = (row // CHUNK) == (col // CHUNK)
    strict = same & (row > col)
    incl = same & (row >= col)
    eye = (row == col).astype(F32)
    for pr in range(RW_HEADS // 2):
        sl = slice(pr * LANES, (pr + 1) * LANES)
        la, lr = stack(a_s[:, sl]), stack(r_s[:, sl])
        rb, rk = stack(b_s[:, sl]), stack(k_s[:, sl])
        sc = _dot_nt(jnp.concatenate([la, lr], axis=0), jnp.concatenate([rb, rk], axis=0))
        a_ab = jnp.where(strict, sc[0:n2, 0:n2], 0.0)
        a_ak = jnp.where(strict, sc[0:n2, n2:2 * n2], 0.0)
        a_rb = jnp.where(incl, sc[n2:2 * n2, 0:n2], 0.0)
        a_rk = jnp.where(incl, sc[n2:2 * n2, n2:2 * n2], 0.0)
        inv = eye + a_ab
        pw = a_ab
        for _ in range(5):
            pw = _dot(pw, pw)
            inv = inv + _dot(inv, pw)
        ht = ht_ref[pr]
        v_st = stack(v[:, sl])
        u_st = _dot(inv, _dot(a_ak, v_st) + _dot_nt(la, ht))
        o_st = _dot_nt(lr, ht) + _dot(a_rb, u_st) + _dot(a_rk, v_st)
        o_scr[:, sl] = o_st[0:CHUNK] + o_st[CHUNK:n2]
        ht_ref[pr] = (ht * decay_end[:, sl] + _dot_tn(u_st, stack(b_e[:, sl]))
                      + _dot_tn(v_st, stack(k_e[:, sl])))

    o = o_scr[...]
    mean = _seg_dot(o, p["segm"][...])
    cen = o - mean
    var = _seg_dot(cen * cen, p["segm"][...])
    o = cen * lax.rsqrt(var + RW_LN_EPS) * p["ln_w"][...] + p["ln_b"][...]
    bonus = _seg_dot(r * k * p["rk"][...], p["seg1"][...]) * v
    y_ref[...] = ((o + bonus) * gate).astype(y_ref.dtype)


def _causal_conv4(x, prev, w_ref, b_ref):
    y = x * w_ref[3:4, :] + b_ref[...]
    for d in (1, 2, 3):
        y = y + _shift_rows(x, prev, d) * w_ref[3 - d:4 - d, :]
    return y


def _mlstm_chunk(u_ref, p, tri, prev_ref, m_ref, n_ref, run_ref, y_ref):
    gw = GROUP_W
    xm = u_ref[:, U_ML:U_ML + gw]
    v = u_ref[:, U_ML + gw:U_ML + 2 * gw]
    o_raw = u_ref[:, U_ML + 2 * gw:U_ML + 3 * gw]
    gates = u_ref[:, U_ML + 3 * gw:U_ML + ML_PAD] + p["gate_b"][...]
    prev = prev_ref[...]
    prev_ref[...] = xm
    xc = _silu(_causal_conv4(xm, prev, p["conv_w"], p["conv_b"]))
    q = _dot(xc, p["wq"][...]) * (ML_DIM ** -0.5)
    k = _dot(xc, p["wk"][...])
    log_f = jnp.minimum(gates, 0.0) - jnp.log(1.0 + jnp.exp(-jnp.abs(gates)))
    bcum = _cumsum_rows(log_f, tri)
    row = lax.broadcasted_iota(jnp.int32, (CHUNK, CHUNK), 0)
    col = lax.broadcasted_iota(jnp.int32, (CHUNK, CHUNK), 1)
    causal = row >= col
    eye = row == col
    to_row = lambda c: jnp.sum(jnp.where(eye, c, 0.0), axis=0, keepdims=True)
    for h in range(ML_HEADS):
        sl = slice(h * LANES, (h + 1) * LANES)
        qh, kh, vh = q[:, sl], k[:, sl], v[:, sl]
        b_col = bcum[:, ML_HEADS + h:ML_HEADS + h + 1]
        i_col = gates[:, h:h + 1]
        m_prev = run_ref[h][0:1, 0:1]
        n_prev = n_ref[h][0:1, :]
        mat = m_ref[h]
        lw = b_col - to_row(b_col) + to_row(i_col)
        lprev = b_col + m_prev
        m_t = jnp.maximum(lprev, jnp.max(jnp.where(causal, lw, NEG_BIG), axis=-1, keepdims=True))
        wts = jnp.where(causal, jnp.exp(jnp.where(causal, lw - m_t, 0.0)), 0.0)
        s = _dot_nt(qh, kh) * wts
        wp = jnp.exp(lprev - m_t)
        num = _dot(s, vh) + wp * _dot(qh, mat)
        den = jnp.sum(s, axis=-1, keepdims=True) + wp * jnp.sum(qh * n_prev, axis=-1, keepdims=True)
        hh = num / jnp.maximum(jnp.abs(den), jnp.exp(-m_t))
        m_new = m_t[CHUNK - 1:CHUNK]
        b_end = b_col[CHUNK - 1:CHUNK]
        wl = jnp.exp(b_end - b_col + i_col - m_new)
        dec = jnp.exp(b_end + m_prev - m_new)
        kw = kh * wl
        m_ref[h] = dec * mat + _dot_tn(kw, vh)
        n_ref[h] = jnp.broadcast_to(dec * n_prev + jnp.sum(kw, axis=0, keepdims=True), (8, LANES))
        run_ref[h] = jnp.broadcast_to(m_new, (8, LANES))
        hg = jax.nn.sigmoid(o_raw[:, sl]) * hh
        hn = hg * lax.rsqrt(jnp.mean(hg * hg, axis=-1, keepdims=True) + NORM_EPS) * p["norm"][:, sl]
        y_ref[:, sl] = (hn + p["skip"][:, sl] * xc[:, sl]).astype(y_ref.dtype)


def _rglru_chunk(u_ref, p, prev_ref, h_ref, y_ref):
    gw = GROUP_W
    xb = u_ref[:, U_LRU:U_LRU + gw]
    gb = u_ref[:, U_LRU + gw:U_LRU + 2 * gw]
    prev = prev_ref[...]
    prev_ref[...] = xb
    xc = _causal_conv4(xb, prev, p["conv_w"], p["conv_b"])
    r = jax.nn.sigmoid(_dot(xc, p["wa"][...]) + p["ba"][...])
    i = jax.nn.sigmoid(_dot(xc, p["wx"][...]) + p["bx"][...])
    log_a = -LRU_C * r * _softplus(-p["lam"][...])
    a = jnp.exp(log_a)
    bt = jnp.sqrt(1.0 - jnp.exp(2.0 * log_a)) * (i * xc)
    rows = lax.broadcasted_iota(jnp.int32, a.shape, 0)
    d = 1
    while d < CHUNK:
        keep = rows >= d
        bt = a * jnp.where(keep, pltpu.roll(bt, d, 0), 0.0) + bt
        a = a * jnp.where(keep, pltpu.roll(a, d, 0), 1.0)
        d *= 2
    hs = bt + a * h_ref[0:1, :]
    h_ref[...] = jnp.broadcast_to(hs[CHUNK - 1:CHUNK], h_ref.shape)
    gelu = 0.5 * gb * (1.0 + jnp.tanh(0.7978845608028654 * (gb + 0.044715 * (gb * gb * gb))))
    yl = hs * gelu
    y_ref[...] = (yl * lax.rsqrt(jnp.mean(yl * yl, axis=-1, keepdims=True) + NORM_EPS)
                  * p["norm"][...]).astype(y_ref.dtype)


_HG_KEYS = ("lb", "gnorm")
_RW_KEYS = ("mu", "w0", "w2", "a0", "a2", "g2", "kk", "ka", "rk", "ln_w", "ln_b", "seg1", "segm")
_ML_KEYS = ("conv_w", "conv_b", "wq", "wk", "gate_b", "norm", "skip")
_LRU_KEYS = ("conv_w", "conv_b", "wa", "ba", "wx", "bx", "lam", "norm")


def _mixer_kernel(*refs):
    it = iter(refs)
    u_ref = next(it)
    hg = {k: next(it) for k in _HG_KEYS}
    rw = {k: next(it) for k in _RW_KEYS}
    ml = {k: next(it) for k in _ML_KEYS}
    lru = {k: next(it) for k in _LRU_KEYS}
    y_ref = next(it)
    (hg_st, rw_prev, rw_ht, rw_o, ml_prev, ml_m, ml_n, ml_run, lru_prev, lru_h) = it

    @pl.when(pl.program_id(1) == 0)
    def _():
        for ref in (hg_st, rw_prev, rw_ht, ml_prev, ml_m, ml_n, ml_run, lru_prev, lru_h):
            ref[...] = jnp.zeros_like(ref)

    r_i = lax.broadcasted_iota(jnp.int32, (CHUNK, CHUNK), 0)
    c_i = lax.broadcasted_iota(jnp.int32, (CHUNK, CHUNK), 1)
    tri = (r_i >= c_i).astype(BF16)
    gw = GROUP_W
    _hgrn2_chunk(u_ref, hg["lb"][...], hg["gnorm"][...], tri, hg_st, y_ref.at[:, 0:gw])
    _rwkv7_chunk(u_ref, rw, tri, rw_prev, rw_ht, rw_o, y_ref.at[:, gw:2 * gw])
    _mlstm_chunk(u_ref, ml, tri, ml_prev, ml_m, ml_n, ml_run, y_ref.at[:, 2 * gw:3 * gw])
    _rglru_chunk(u_ref, lru, lru_prev, lru_h, y_ref.at[:, 3 * gw:4 * gw])


def _block_diag(blocks):
    n, a, b = blocks.shape
    eye = jnp.eye(n, dtype=blocks.dtype)
    return (eye[:, None, :, None] * blocks[:, :, None, :]).reshape(n * a, n * b)


def _mixer_params(l, lower_bounds, hg_norm, rw_mu, rw_w0, rw_w2, rw_a0, rw_a2, rw_g2, rw_kk, rw_ka,
                  rw_rk, rw_ln_w, rw_ln_b, ml_conv_w, ml_conv_b, ml_wq, ml_wk, ml_i_b, ml_f_b,
                  ml_norm, ml_skip, lru_conv_w, lru_conv_b, lru_wa, lru_ba, lru_wx, lru_bx,
                  lru_lambda, lru_norm):
    gw = GROUP_W
    row = lambda a: a.reshape(1, -1).astype(F32)

    def lora_rows(w, start):
        z = jnp.zeros((LORA_PAD, gw), F32)
        return lax.dynamic_update_slice(z, w.astype(F32), (start, 0)).astype(BF16)

    head_id = jnp.arange(gw) // RW_HEAD
    seg1 = (head_id[:, None] == head_id[None, :]).astype(F32)
    hg = dict(lb=row(lower_bounds[l]), gnorm=row(hg_norm[l]))
    rw = dict(mu=row(jnp.pad(rw_mu[l], (0, RW_PAD - RW_IN))), w0=row(rw_w0[l]),
              w2=lora_rows(rw_w2[l], 0), a0=row(rw_a0[l]),
              a2=lora_rows(rw_a2[l], RW_W_LORA), g2=lora_rows(rw_g2[l], RW_W_LORA + RW_A_LORA),
              kk=row(rw_kk[l]), ka=row(rw_ka[l]), rk=row(rw_rk[l]), ln_w=row(rw_ln_w[l]),
              ln_b=row(rw_ln_b[l]), seg1=seg1.astype(BF16), segm=(seg1 / RW_HEAD).astype(BF16))
    gate_b = jnp.concatenate([ml_i_b[l], ml_f_b[l], jnp.zeros((LANES - 2 * ML_HEADS,), F32)])
    ml = dict(conv_w=ml_conv_w[l].astype(F32), conv_b=row(ml_conv_b[l]),
              wq=_block_diag(ml_wq[l]).astype(BF16), wk=_block_diag(ml_wk[l]).astype(BF16),
              gate_b=row(gate_b), norm=row(ml_norm[l]), skip=row(ml_skip[l]))
    lru = dict(conv_w=lru_conv_w[l].astype(F32), conv_b=row(lru_conv_b[l]),
               wa=_block_diag(lru_wa[l]).astype(BF16), ba=row(lru_ba[l]),
               wx=_block_diag(lru_wx[l]).astype(BF16), bx=row(lru_bx[l]),
               lam=row(lru_lambda[l]), norm=row(lru_norm[l]))
    return ([hg[k] for k in _HG_KEYS] + [rw[k] for k in _RW_KEYS]
            + [ml[k] for k in _ML_KEYS] + [lru[k] for k in _LRU_KEYS])


def _mixer_call(u, params):
    bsz, seqlen, _ = u.shape
    d_mix = 4 * GROUP_W
    full = lambda a: pl.BlockSpec(a.shape, lambda b, c: (0,) * a.ndim)
    scratch = [
        pltpu.VMEM((HG_HEADS, LANES, LANES), F32),
        pltpu.VMEM((CHUNK, RW_PAD), F32),
        pltpu.VMEM((RW_HEADS // 2, LANES, LANES), F32),
        pltpu.VMEM((CHUNK, GROUP_W), F32),
        pltpu.VMEM((CHUNK, GROUP_W), F32),
        pltpu.VMEM((ML_HEADS, LANES, LANES), F32),
        pltpu.VMEM((ML_HEADS, 8, LANES), F32),
        pltpu.VMEM((ML_HEADS, 8, LANES), F32),
        pltpu.VMEM((CHUNK, GROUP_W), F32),
        pltpu.VMEM((8, GROUP_W), F32),
    ]
    return pl.pallas_call(
        _mixer_kernel,
        grid=(bsz, seqlen // CHUNK),
        in_specs=[pl.BlockSpec((None, CHUNK, U_TOT), lambda b, c: (b, c, 0))]
        + [full(a) for a in params],
        out_specs=pl.BlockSpec((None, CHUNK, d_mix), lambda b, c: (b, c, 0)),
        out_shape=jax.ShapeDtypeStruct((bsz, seqlen, d_mix), BF16),
        scratch_shapes=scratch,
        compiler_params=pltpu.CompilerParams(
            dimension_semantics=("parallel", "arbitrary"), vmem_limit_bytes=VMEM_LIMIT),
        name="mixers",
    )(u, *params)


def _pad_w_in(w):
    d = w.shape[0]
    z = lambda n: jnp.zeros((d, n), w.dtype)
    c0, c1, c2 = HG_IN, HG_IN + RW_IN, HG_IN + RW_IN + ML_IN
    return jnp.concatenate(
        [w[:, :c1], z(RW_PAD - RW_IN), w[:, c1:c2], z(ML_PAD - ML_IN), w[:, c2:],
         z(U_TOT - U_LRU - LRU_IN)], axis=1)


def kernel(x, c, norm_gain, mod_w, mod_b, ffn_w1, ffn_w3, ffn_w2, w_in, w_out, hg_lb_logits, hg_norm, rw_mu, rw_w0, rw_w2, rw_a0, rw_a2, rw_g2, rw_kk, rw_ka, rw_rk, rw_ln_w, rw_ln_b, ml_conv_w, ml_conv_b, ml_wq, ml_wk, ml_i_b, ml_f_b, ml_norm, ml_skip, lru_conv_w, lru_conv_b, lru_wa, lru_ba, lru_wx, lru_bx, lru_lambda, lru_norm, final_norm):
    bsz = x.shape[0]
    depth = mod_w.shape[0]
    lb_w = jax.nn.softmax(hg_lb_logits.astype(F32), axis=0)
    lower_bounds = jnp.cumsum(lb_w, axis=0) - lb_w[0]
    mod = _mod_call(c, mod_w, mod_b)
    mod5 = mod.reshape(depth, bsz, 3 * N_SUB, 1, D_MODEL)
    for l in range(depth):
        mod4 = mod5[l]
        bf = lambda a: a.astype(BF16)
        x = _ffn_call(x, norm_gain[l, 0], mod4, 0, bf(ffn_w1[l, 0]), bf(ffn_w3[l, 0]), bf(ffn_w2[l, 0]))
        u = _proj_call(x, norm_gain[l, 1], mod4, 1, bf(_pad_w_in(w_in[l])))
        params = _mixer_params(
            l, lower_bounds, hg_norm, rw_mu, rw_w0, rw_w2, rw_a0, rw_a2, rw_g2, rw_kk, rw_ka, rw_rk,
            rw_ln_w, rw_ln_b, ml_conv_w, ml_conv_b, ml_wq, ml_wk, ml_i_b, ml_f_b, ml_norm, ml_skip,
            lru_conv_w, lru_conv_b, lru_wa, lru_ba, lru_wx, lru_bx, lru_lambda, lru_norm)
        y = _mixer_call(u, params)
        x = _outproj_call(x, y, mod4, 1, bf(w_out[l]))
        last = l == depth - 1
        x = _ffn_call(x, norm_gain[l, 2], mod4, 2, bf(ffn_w1[l, 1]), bf(ffn_w3[l, 1]), bf(ffn_w2[l, 1]),
                      final_gain=final_norm if last else None)
    return x
```

```python
import functools

import jax
import jax.numpy as jnp
from jax import lax
from jax.experimental import pallas as pl
from jax.experimental.pallas import tpu as pltpu

F32 = jnp.float32
BF16 = jnp.bfloat16

D_MODEL = 2048
DEPTH = 2
GROUP_W = 512
N_SUB = 3
D_FF = 5632
NORM_EPS = 1e-6
HG_HEADS = 4
RW_HEAD = 64
RW_HEADS = 8
RW_W_LORA = 32
RW_A_LORA = 32
RW_G_LORA = 96
RW_LN_EPS = 64e-5
RW_IN = 3 * GROUP_W + RW_W_LORA + RW_A_LORA + RW_G_LORA
ML_HEADS = 4
ML_DIM = 128
ML_IN = 3 * GROUP_W + 2 * ML_HEADS
NEG_BIG = -1e30
LRU_C = 8.0
HG_IN = 4 * GROUP_W
LRU_IN = 2 * GROUP_W

LANES = 128
CHUNK = 64
SUB = 16
RW_PAD = 1792
ML_PAD = 1664
U_HG = 0
U_RW = U_HG + HG_IN
U_ML = U_RW + RW_PAD
U_LRU = U_ML + ML_PAD
U_TOT = 6656
LORA_PAD = RW_PAD - 3 * GROUP_W
VMEM_LIMIT = 56 * 1024 * 1024


def _silu(x):
    return x * jax.nn.sigmoid(x)


def _softplus(x):
    return jnp.maximum(x, 0.0) + jnp.log(1.0 + jnp.exp(-jnp.abs(x)))


def _dot(a, b):
    return jnp.dot(a.astype(BF16), b.astype(BF16), preferred_element_type=F32)


def _dot_nt(a, b):
    return lax.dot_general(a.astype(BF16), b.astype(BF16), (((1,), (1,)), ((), ())),
                           preferred_element_type=F32)


def _dot_tn(a, b):
    return lax.dot_general(a.astype(BF16), b.astype(BF16), (((0,), (0,)), ((), ())),
                           preferred_element_type=F32)


def _split3(x):
    hi = x.astype(BF16)
    r1 = x - hi.astype(F32)
    mid = r1.astype(BF16)
    lo = (r1 - mid.astype(F32)).astype(BF16)
    return hi, mid, lo


def _cumsum_rows(x, tri):
    hi, mid, lo = _split3(x)
    d = functools.partial(jnp.dot, preferred_element_type=F32)
    return d(tri, hi) + d(tri, mid) + d(tri, lo)


def _seg_dot(x, seg):
    hi, mid, lo = _split3(x)
    d = functools.partial(jnp.dot, preferred_element_type=F32)
    return d(hi, seg) + d(mid, seg) + d(lo, seg)


def _shift_rows(x, prev, d):
    rows = lax.broadcasted_iota(jnp.int32, x.shape, 0)
    return jnp.where(rows < d, pltpu.roll(prev, d, 0), pltpu.roll(x, d, 0))


def _modulated_norm(x, gain, shift, scale):
    y = x * lax.rsqrt(jnp.mean(x * x, axis=-1, keepdims=True) + NORM_EPS) * gain
    return y * (1.0 + scale) + shift


def _mod_kernel(c_ref, w_ref, b_ref, o_ref):
    o_ref[...] = _dot(_silu(c_ref[...]), w_ref[...]) + b_ref[...]


def _mod_call(c, mod_w, mod_b):
    depth, d, n = mod_w.shape
    bsz = c.shape[0]
    tn = 1024
    return pl.pallas_call(
        _mod_kernel,
        grid=(depth, n // tn),
        in_specs=[pl.BlockSpec((bsz, d), lambda l, j: (0, 0)),
                  pl.BlockSpec((None, d, tn), lambda l, j: (l, 0, j)),
                  pl.BlockSpec((None, 1, tn), lambda l, j: (l, 0, j))],
        out_specs=pl.BlockSpec((None, bsz, tn), lambda l, j: (l, 0, j)),
        out_shape=jax.ShapeDtypeStruct((depth, bsz, n), F32),
        compiler_params=pltpu.CompilerParams(
            dimension_semantics=("arbitrary", "arbitrary"), vmem_limit_bytes=VMEM_LIMIT),
        name="adaln_mod",
    )(c, mod_w, mod_b.reshape(depth, 1, n))


def _ffn_kernel(x_ref, gain_ref, shift_ref, scale_ref, gate_ref, w1_ref, w3_ref, w2_ref, *rest,
                n_ff, final):
    if final:
        fgain_ref, o_ref, h_ref, acc_ref = rest
    else:
        o_ref, h_ref, acc_ref = rest
    j = pl.program_id(2)

    @pl.when(j == 0)
    def _():
        h = _modulated_norm(x_ref[...], gain_ref[...], shift_ref[...], scale_ref[...])
        h_ref[...] = h.astype(BF16)
        acc_ref[...] = jnp.zeros_like(acc_ref)

    h = h_ref[...]
    a = jnp.dot(h, w1_ref[...], preferred_element_type=F32)
    b = jnp.dot(h, w3_ref[...], preferred_element_type=F32)
    g = (_silu(a) * b).astype(BF16)
    acc_ref[...] += jnp.dot(g, w2_ref[...], preferred_element_type=F32)

    @pl.when(j == n_ff - 1)
    def _():
        xn = x_ref[...] + (0.5 * (1.0 + gate_ref[...])) * acc_ref[...]
        if final:
            xn = xn * lax.rsqrt(jnp.mean(xn * xn, axis=-1, keepdims=True) + NORM_EPS) * fgain_ref[...]
        o_ref[...] = xn


def _ffn_call(x, gain, mod4, sub, w1, w3, w2, final_gain=None, tm=512, tf=512):
    bsz, seqlen, d = x.shape
    tm = min(tm, seqlen)
    n_ff = w1.shape[1] // tf
    final = final_gain is not None
    vec = lambda k: pl.BlockSpec((None, None, 1, d), lambda b, i, j: (b, 3 * sub + k, 0, 0))
    in_specs = [pl.BlockSpec((None, tm, d), lambda b, i, j: (b, i, 0)),
                pl.BlockSpec((1, d), lambda b, i, j: (0, 0)),
                vec(0), vec(1), vec(2),
                pl.BlockSpec((d, tf), lambda b, i, j: (0, j)),
                pl.BlockSpec((d, tf), lambda b, i, j: (0, j)),
                pl.BlockSpec((tf, d), lambda b, i, j: (j, 0))]
    args = [x, gain.reshape(1, d), mod4, mod4, mod4, w1, w3, w2]
    if final:
        in_specs.append(pl.BlockSpec((1, d), lambda b, i, j: (0, 0)))
        args.append(final_gain.reshape(1, d))
    return pl.pallas_call(
        functools.partial(_ffn_kernel, n_ff=n_ff, final=final),
        grid=(bsz, seqlen // tm, n_ff),
        in_specs=in_specs,
        out_specs=pl.BlockSpec((None, tm, d), lambda b, i, j: (b, i, 0)),
        out_shape=jax.ShapeDtypeStruct(x.shape, F32),
        scratch_shapes=[pltpu.VMEM((tm, d), BF16), pltpu.VMEM((tm, d), F32)],
        compiler_params=pltpu.CompilerParams(
            dimension_semantics=("parallel", "parallel", "arbitrary"), vmem_limit_bytes=VMEM_LIMIT),
        name="ffn_final" if final else "ffn",
    )(*args)


def _proj_kernel(x_ref, gain_ref, shift_ref, scale_ref, w_ref, o_ref, h_ref):
    @pl.when(pl.program_id(2) == 0)
    def _():
        h = _modulated_norm(x_ref[...], gain_ref[...], shift_ref[...], scale_ref[...])
        h_ref[...] = h.astype(BF16)

    o_ref[...] = jnp.dot(h_ref[...], w_ref[...], preferred_element_type=F32)


def _proj_call(x, gain, mod4, sub, w, tm=1024, tn=512):
    bsz, seqlen, d = x.shape
    tm = min(tm, seqlen)
    n = w.shape[1]
    vec = lambda k: pl.BlockSpec((None, None, 1, d), lambda b, i, j: (b, 3 * sub + k, 0, 0))
    return pl.pallas_call(
        _proj_kernel,
        grid=(bsz, seqlen // tm, n // tn),
        in_specs=[pl.BlockSpec((None, tm, d), lambda b, i, j: (b, i, 0)),
                  pl.BlockSpec((1, d), lambda b, i, j: (0, 0)),
                  vec(0), vec(1),
                  pl.BlockSpec((d, tn), lambda b, i, j: (0, j))],
        out_specs=pl.BlockSpec((None, tm, tn), lambda b, i, j: (b, i, j)),
        out_shape=jax.ShapeDtypeStruct((bsz, seqlen, n), F32),
        scratch_shapes=[pltpu.VMEM((tm, d), BF16)],
        compiler_params=pltpu.CompilerParams(
            dimension_semantics=("parallel", "parallel", "arbitrary"), vmem_limit_bytes=VMEM_LIMIT),
        name="in_proj",
    )(x, gain.reshape(1, d), mod4, mod4, w)


def _outproj_kernel(x_ref, y_ref, gate_ref, w_ref, o_ref):
    o_ref[...] = x_ref[...] + (1.0 + gate_ref[...]) * jnp.dot(
        y_ref[...], w_ref[...], preferred_element_type=F32)


def _outproj_call(x, y, mod4, sub, w, tm=512):
    bsz, seqlen, d = x.shape
    tm = min(tm, seqlen)
    return pl.pallas_call(
        _outproj_kernel,
        grid=(bsz, seqlen // tm),
        in_specs=[pl.BlockSpec((None, tm, d), lambda b, i: (b, i, 0)),
                  pl.BlockSpec((None, tm, y.shape[-1]), lambda b, i: (b, i, 0)),
                  pl.BlockSpec((None, None, 1, d), lambda b, i: (b, 3 * sub + 2, 0, 0)),
                  pl.BlockSpec(w.shape, lambda b, i: (0, 0))],
        out_specs=pl.BlockSpec((None, tm, d), lambda b, i: (b, i, 0)),
        out_shape=jax.ShapeDtypeStruct(x.shape, F32),
        compiler_params=pltpu.CompilerParams(
            dimension_semantics=("parallel", "parallel"), vmem_limit_bytes=VMEM_LIMIT),
        name="out_proj",
    )(x, y, mod4, w)


def _hgrn2_chunk(u_ref, lb, gnorm, tri, st_ref, y_ref):
    gw = GROUP_W
    q = _silu(u_ref[:, U_HG:U_HG + gw])
    f_raw = u_ref[:, U_HG + gw:U_HG + 2 * gw]
    v = u_ref[:, U_HG + 2 * gw:U_HG + 3 * gw]
    g_raw = u_ref[:, U_HG + 3 * gw:U_HG + 4 * gw]
    log_f = jnp.log(lb + (1.0 - lb) * jax.nn.sigmoid(f_raw))
    k = (1.0 - lb) * jax.nn.sigmoid(-f_raw)
    yield
    bcum = _cumsum_rows(log_f, tri)
    yield
    sub_row = lax.broadcasted_iota(jnp.int32, (SUB, 1), 0)
    for h in range(HG_HEADS):
        sl = slice(h * LANES, (h + 1) * LANES)
        qh, kh, vh, bh = q[:, sl], k[:, sl], v[:, sl], bcum[:, sl]
        st = st_ref[h]
        o_inter = _dot_nt(qh * jnp.exp(bh), st)
        pieces = []
        for i in range(CHUNK // SUB):
            r0 = i * SUB
            qd, kd, vd, bd = qh[r0:r0 + SUB], kh[r0:r0 + SUB], vh[r0:r0 + SUB], bh[r0:r0 + SUB]
            oi = jnp.zeros((SUB, LANES), F32)
            if i > 0:
                bref = bh[r0 - 1:r0]
                qi = qd * jnp.exp(bd - bref)
                kp = kh[0:r0] * jnp.exp(bref - bh[0:r0])
                oi = _dot(_dot_nt(qi, kp), vh[0:r0])
            for j in range(SUB):
                w = jnp.exp(jnp.minimum(bd - bd[j:j + 1], 0.0))
                col = jnp.sum(qd * w * kd[j:j + 1], axis=-1, keepdims=True)
                col = jnp.where(sub_row >= j, col, 0.0)
                oi = oi + col * vd[j:j + 1]
            pieces.append(oi)
            yield
        o = o_inter + jnp.concatenate(pieces, axis=0)
        b_last = bh[CHUNK - 1:CHUNK]
        st_ref[h] = st * jnp.exp(b_last) + _dot_tn(vh, kh * jnp.exp(b_last - bh))
        on = o * lax.rsqrt(jnp.mean(o * o, axis=-1, keepdims=True) + NORM_EPS) * gnorm[:, sl]
        y_ref[:, sl] = (on * _silu(g_raw[:, sl])).astype(y_ref.dtype)
        yield


def _rwkv7_chunk(u_ref, p, tri, prev_ref, ht_ref, o_scr, y_ref):
    gw = GROUP_W
    u_raw = u_ref[:, U_RW:U_RW + RW_PAD]
    prev = prev_ref[...]
    prev_ref[...] = u_raw
    u = u_raw + (_shift_rows(u_raw, prev, 1) - u_raw) * p["mu"][...]
    yield
    r, k, v, lora = u[:, 0:gw], u[:, gw:2 * gw], u[:, 2 * gw:3 * gw], u[:, 3 * gw:RW_PAD]
    zw = p["w0"][...] + _dot(jnp.tanh(lora), p["w2"][...])
    lw = -jnp.exp(-_softplus(-zw) - 0.5)
    iclr = jax.nn.sigmoid(p["a0"][...] + _dot(lora, p["a2"][...]))
    gate = _dot(jax.nn.sigmoid(lora), p["g2"][...])
    yield
    kk = k * p["kk"][...]
    ss = _seg_dot(kk * kk, p["seg1"][...])
    kk = kk / jnp.maximum(jnp.sqrt(ss + 1e-12), 1e-6)
    k = k * (1.0 + (iclr - 1.0) * p["ka"][...])
    a_vec = -kk
    b_vec = kk * iclr
    yield

    lc = _cumsum_rows(lw, tri)
    l_last = lc[CHUNK - 1:CHUNK]
    g_in = jnp.exp(lc)
    g_out = jnp.exp(-lc)
    g_end = jnp.exp(l_last - lc)
    a_s = a_vec * jnp.exp(lc - lw)
    r_s = r * g_in
    b_s = b_vec * g_out
    k_s = k * g_out
    b_e = b_vec * g_end
    k_e = k * g_end
    decay_end = jnp.exp(l_last)
    yield

    lane = lax.broadcasted_iota(jnp.int32, (1, LANES), 1)
    m_a = (lane < RW_HEAD).astype(F32)
    m_b = 1.0 - m_a
    stack = lambda t: jnp.concatenate([t * m_a, t * m_b], axis=0)
    n2 = 2 * CHUNK
    row = lax.broadcasted_iota(jnp.int32, (n2, n2), 0)
    col = lax.broadcasted_iota(jnp.int32, (n2, n2), 1)
    same = (row // CHUNK) == (col // CHUNK)
    strict = same & (row > col)
    incl = same & (row >= col)
    pairs = range(RW_HEADS // 2)
    sls = [slice(pr * LANES, (pr + 1) * LANES) for pr in pairs]
    la = [stack(a_s[:, sl]) for sl in sls]
    lr = [stack(r_s[:, sl]) for sl in sls]
    v_st = [stack(v[:, sl]) for sl in sls]
    ht = [ht_ref[pr] for pr in pairs]
    sc = [_dot_nt(jnp.concatenate([la[pr], lr[pr]], axis=0),
                  jnp.concatenate([stack(b_s[:, sls[pr]]), stack(k_s[:, sls[pr]])], axis=0))
          for pr in pairs]
    yield
    pw = [jnp.where(strict, s[0:n2, 0:n2], 0.0) for s in sc]
    x = [_dot(jnp.where(strict, sc[pr][0:n2, n2:2 * n2], 0.0), v_st[pr]) + _dot_nt(la[pr], ht[pr])
         for pr in pairs]
    for step in range(6):
        x = [x[pr] + _dot(pw[pr], x[pr]) for pr in pairs]
        if step < 5:
            pw = [_dot(m, m) for m in pw]
        yield
    for pr in pairs:
        sl = sls[pr]
        a_rb = jnp.where(incl, sc[pr][n2:2 * n2, 0:n2], 0.0)
        a_rk = jnp.where(incl, sc[pr][n2:2 * n2, n2:2 * n2], 0.0)
        o_st = _dot_nt(lr[pr], ht[pr]) + _dot(a_rb, x[pr]) + _dot(a_rk, v_st[pr])
        o_scr[:, sl] = o_st[0:CHUNK] + o_st[CHUNK:n2]
        ht_ref[pr] = (ht[pr] * decay_end[:, sl] + _dot_tn(x[pr], stack(b_e[:, sl]))
                      + _dot_tn(v_st[pr], stack(k_e[:, sl])))
        yield

    o = o_scr[...]
    mean = _seg_dot(o, p["segm"][...])
    cen = o - mean
    var = _seg_dot(cen * cen, p["segm"][...])
    yield
    o = cen * lax.rsqrt(var + RW_LN_EPS) * p["ln_w"][...] + p["ln_b"][...]
    bonus = _seg_dot(r * k * p["rk"][...], p["seg1"][...]) * v
    y_ref[...] = ((o + bonus) * gate).astype(y_ref.dtype)


def _causal_conv4(x, prev, w_ref, b_ref):
    y = x * w_ref[3:4, :] + b_ref[...]
    for d in (1, 2, 3):
        y = y + _shift_rows(x, prev, d) * w_ref[3 - d:4 - d, :]
    return y


def _mlstm_chunk(u_ref, p, tri, prev_ref, m_ref, n_ref, run_ref, y_ref):
    gw = GROUP_W
    xm = u_ref[:, U_ML:U_ML + gw]
    v = u_ref[:, U_ML + gw:U_ML + 2 * gw]
    o_raw = u_ref[:, U_ML + 2 * gw:U_ML + 3 * gw]
    gates = u_ref[:, U_ML + 3 * gw:U_ML + ML_PAD] + p["gate_b"][...]
    prev = prev_ref[...]
    prev_ref[...] = xm
    xc = _silu(_causal_conv4(xm, prev, p["conv_w"], p["conv_b"]))
    q = _dot(xc, p["wq"][...]) * (ML_DIM ** -0.5)
    k = _dot(xc, p["wk"][...])
    yield
    log_f = jnp.minimum(gates, 0.0) - jnp.log(1.0 + jnp.exp(-jnp.abs(gates)))
    bcum = _cumsum_rows(log_f, tri)
    row = lax.broadcasted_iota(jnp.int32, (CHUNK, CHUNK), 0)
    col = lax.broadcasted_iota(jnp.int32, (CHUNK, CHUNK), 1)
    causal = row >= col
    eye = row == col
    to_row = lambda c: jnp.sum(jnp.where(eye, c, 0.0), axis=0, keepdims=True)
    for h in range(ML_HEADS):
        sl = slice(h * LANES, (h + 1) * LANES)
        qh, kh, vh = q[:, sl], k[:, sl], v[:, sl]
        b_col = bcum[:, ML_HEADS + h:ML_HEADS + h + 1]
        i_col = gates[:, h:h + 1]
        m_prev = run_ref[h][0:1, 0:1]
        n_prev = n_ref[h][0:1, :]
        mat = m_ref[h]
        lw = b_col - to_row(b_col) + to_row(i_col)
        lprev = b_col + m_prev
        m_t = jnp.maximum(lprev, jnp.max(jnp.where(causal, lw, NEG_BIG), axis=-1, keepdims=True))
        wts = jnp.where(causal, jnp.exp(jnp.where(causal, lw - m_t, 0.0)), 0.0)
        s = _dot_nt(qh, kh) * wts
        wp = jnp.exp(lprev - m_t)
        yield
        num = _dot(s, vh) + wp * _dot(qh, mat)
        den = jnp.sum(s, axis=-1, keepdims=True) + wp * jnp.sum(qh * n_prev, axis=-1, keepdims=True)
        hh = num / jnp.maximum(jnp.abs(den), jnp.exp(-m_t))
        yield
        m_new = m_t[CHUNK - 1:CHUNK]
        b_end = b_col[CHUNK - 1:CHUNK]
        wl = jnp.exp(b_end - b_col + i_col - m_new)
        dec = jnp.exp(b_end + m_prev - m_new)
        kw = kh * wl
        m_ref[h] = dec * mat + _dot_tn(kw, vh)
        n_ref[h] = jnp.broadcast_to(dec * n_prev + jnp.sum(kw, axis=0, keepdims=True), (8, LANES))
        run_ref[h] = jnp.broadcast_to(m_new, (8, LANES))
        hg = jax.nn.sigmoid(o_raw[:, sl]) * hh
        hn = hg * lax.rsqrt(jnp.mean(hg * hg, axis=-1, keepdims=True) + NORM_EPS) * p["norm"][:, sl]
        y_ref[:, sl] = (hn + p["skip"][:, sl] * xc[:, sl]).astype(y_ref.dtype)
        yield


def _rglru_chunk(u_ref, p, prev_ref, h_ref, y_ref):
    gw = GROUP_W
    xb = u_ref[:, U_LRU:U_LRU + gw]
    gb = u_ref[:, U_LRU + gw:U_LRU + 2 * gw]
    prev = prev_ref[...]
    prev_ref[...] = xb
    xc = _causal_conv4(xb, prev, p["conv_w"], p["conv_b"])
    r = jax.nn.sigmoid(_dot(xc, p["wa"][...]) + p["ba"][...])
    i = jax.nn.sigmoid(_dot(xc, p["wx"][...]) + p["bx"][...])
    yield
    log_a = -LRU_C * r * _softplus(-p["lam"][...])
    a = jnp.exp(log_a)
    bt = jnp.sqrt(1.0 - jnp.exp(2.0 * log_a)) * (i * xc)
    rows = lax.broadcasted_iota(jnp.int32, a.shape, 0)
    d = 1
    while d < CHUNK:
        keep = rows >= d
        bt = a * jnp.where(keep, pltpu.roll(bt, d, 0), 0.0) + bt
        a = a * jnp.where(keep, pltpu.roll(a, d, 0), 1.0)
        d *= 2
        yield
    hs = bt + a * h_ref[0:1, :]
    h_ref[...] = jnp.broadcast_to(hs[CHUNK - 1:CHUNK], h_ref.shape)
    gelu = 0.5 * gb * (1.0 + jnp.tanh(0.7978845608028654 * (gb + 0.044715 * (gb * gb * gb))))
    yl = hs * gelu
    y_ref[...] = (yl * lax.rsqrt(jnp.mean(yl * yl, axis=-1, keepdims=True) + NORM_EPS)
                  * p["norm"][...]).astype(y_ref.dtype)


def _round_robin(stages):
    live = list(stages)
    while live:
        for g in list(live):
            try:
                next(g)
            except StopIteration:
                live.remove(g)


_HG_KEYS = ("lb", "gnorm")
_RW_KEYS = ("mu", "w0", "w2", "a0", "a2", "g2", "kk", "ka", "rk", "ln_w", "ln_b", "seg1", "segm")
_ML_KEYS = ("conv_w", "conv_b", "wq", "wk", "gate_b", "norm", "skip")
_LRU_KEYS = ("conv_w", "conv_b", "wa", "ba", "wx", "bx", "lam", "norm")


def _mixer_kernel(*refs):
    it = iter(refs)
    u_ref = next(it)
    hg = {k: next(it) for k in _HG_KEYS}
    rw = {k: next(it) for k in _RW_KEYS}
    ml = {k: next(it) for k in _ML_KEYS}
    lru = {k: next(it) for k in _LRU_KEYS}
    y_ref = next(it)
    (hg_st, rw_prev, rw_ht, rw_o, ml_prev, ml_m, ml_n, ml_run, lru_prev, lru_h) = it

    @pl.when(pl.program_id(1) == 0)
    def _():
        for ref in (hg_st, rw_prev, rw_ht, ml_prev, ml_m, ml_n, ml_run, lru_prev, lru_h):
            ref[...] = jnp.zeros_like(ref)

    r_i = lax.broadcasted_iota(jnp.int32, (CHUNK, CHUNK), 0)
    c_i = lax.broadcasted_iota(jnp.int32, (CHUNK, CHUNK), 1)
    tri = (r_i >= c_i).astype(BF16)
    gw = GROUP_W
    _round_robin([
        _hgrn2_chunk(u_ref, hg["lb"][...], hg["gnorm"][...], tri, hg_st, y_ref.at[:, 0:gw]),
        _rwkv7_chunk(u_ref, rw, tri, rw_prev, rw_ht, rw_o, y_ref.at[:, gw:2 * gw]),
        _mlstm_chunk(u_ref, ml, tri, ml_prev, ml_m, ml_n, ml_run, y_ref.at[:, 2 * gw:3 * gw]),
        _rglru_chunk(u_ref, lru, lru_prev, lru_h, y_ref.at[:, 3 * gw:4 * gw])])


def _block_diag(blocks):
    n, a, b = blocks.shape
    eye = jnp.eye(n, dtype=blocks.dtype)
    return (eye[:, None, :, None] * blocks[:, :, None, :]).reshape(n * a, n * b)


def _mixer_params(l, lower_bounds, hg_norm, rw_mu, rw_w0, rw_w2, rw_a0, rw_a2, rw_g2, rw_kk, rw_ka,
                  rw_rk, rw_ln_w, rw_ln_b, ml_conv_w, ml_conv_b, ml_wq, ml_wk, ml_i_b, ml_f_b,
                  ml_norm, ml_skip, lru_conv_w, lru_conv_b, lru_wa, lru_ba, lru_wx, lru_bx,
                  lru_lambda, lru_norm):
    gw = GROUP_W
    row = lambda a: a.reshape(1, -1).astype(F32)

    def lora_rows(w, start):
        z = jnp.zeros((LORA_PAD, gw), F32)
        return lax.dynamic_update_slice(z, w.astype(F32), (start, 0)).astype(BF16)

    head_id = jnp.arange(gw) // RW_HEAD
    seg1 = (head_id[:, None] == head_id[None, :]).astype(F32)
    hg = dict(lb=row(lower_bounds[l]), gnorm=row(hg_norm[l]))
    rw = dict(mu=row(jnp.pad(rw_mu[l], (0, RW_PAD - RW_IN))), w0=row(rw_w0[l]),
              w2=lora_rows(rw_w2[l], 0), a0=row(rw_a0[l]),
              a2=lora_rows(rw_a2[l], RW_W_LORA), g2=lora_rows(rw_g2[l], RW_W_LORA + RW_A_LORA),
              kk=row(rw_kk[l]), ka=row(rw_ka[l]), rk=row(rw_rk[l]), ln_w=row(rw_ln_w[l]),
              ln_b=row(rw_ln_b[l]), seg1=seg1.astype(BF16), segm=(seg1 / RW_HEAD).astype(BF16))
    gate_b = jnp.concatenate([ml_i_b[l], ml_f_b[l], jnp.zeros((LANES - 2 * ML_HEADS,), F32)])
    ml = dict(conv_w=ml_conv_w[l].astype(F32), conv_b=row(ml_conv_b[l]),
              wq=_block_diag(ml_wq[l]).astype(BF16), wk=_block_diag(ml_wk[l]).astype(BF16),
              gate_b=row(gate_b), norm=row(ml_norm[l]), skip=row(ml_skip[l]))
    lru = dict(conv_w=lru_conv_w[l].astype(F32), conv_b=row(lru_conv_b[l]),
               wa=_block_diag(lru_wa[l]).astype(BF16), ba=row(lru_ba[l]),
               wx=_block_diag(lru_wx[l]).astype(BF16), bx=row(lru_bx[l]),
               lam=row(lru_lambda[l]), norm=row(lru_norm[l]))
    return ([hg[k] for k in _HG_KEYS] + [rw[k] for k in _RW_KEYS]
            + [ml[k] for k in _ML_KEYS] + [lru[k] for k in _LRU_KEYS])


def _mixer_call(u, params):
    bsz, seqlen, _ = u.shape
    d_mix = 4 * GROUP_W
    full = lambda a: pl.BlockSpec(a.shape, lambda b, c: (0,) * a.ndim)
    scratch = [
        pltpu.VMEM((HG_HEADS, LANES, LANES), F32),
        pltpu.VMEM((CHUNK, RW_PAD), F32),
        pltpu.VMEM((RW_HEADS // 2, LANES, LANES), F32),
        pltpu.VMEM((CHUNK, GROUP_W), F32),
        pltpu.VMEM((CHUNK, GROUP_W), F32),
        pltpu.VMEM((ML_HEADS, LANES, LANES), F32),
        pltpu.VMEM((ML_HEADS, 8, LANES), F32),
        pltpu.VMEM((ML_HEADS, 8, LANES), F32),
        pltpu.VMEM((CHUNK, GROUP_W), F32),
        pltpu.VMEM((8, GROUP_W), F32),
    ]
    return pl.pallas_call(
        _mixer_kernel,
        grid=(bsz, seqlen // CHUNK),
        in_specs=[pl.BlockSpec((None, CHUNK, U_TOT), lambda b, c: (b, c, 0))]
        + [full(a) for a in params],
        out_specs=pl.BlockSpec((None, CHUNK, d_mix), lambda b, c: (b, c, 0)),
        out_shape=jax.ShapeDtypeStruct((bsz, seqlen, d_mix), BF16),
        scratch_shapes=scratch,
        compiler_params=pltpu.CompilerParams(
            dimension_semantics=("parallel", "arbitrary"), vmem_limit_bytes=VMEM_LIMIT),
        name="mixers",
    )(u, *params)


def _pad_w_in(w):
    d = w.shape[0]
    z = lambda n: jnp.zeros((d, n), w.dtype)
    c0, c1, c2 = HG_IN, HG_IN + RW_IN, HG_IN + RW_IN + ML_IN
    return jnp.concatenate(
        [w[:, :c1], z(RW_PAD - RW_IN), w[:, c1:c2], z(ML_PAD - ML_IN), w[:, c2:],
         z(U_TOT - U_LRU - LRU_IN)], axis=1)


def kernel(x, c, norm_gain, mod_w, mod_b, ffn_w1, ffn_w3, ffn_w2, w_in, w_out, hg_lb_logits, hg_norm, rw_mu, rw_w0, rw_w2, rw_a0, rw_a2, rw_g2, rw_kk, rw_ka, rw_rk, rw_ln_w, rw_ln_b, ml_conv_w, ml_conv_b, ml_wq, ml_wk, ml_i_b, ml_f_b, ml_norm, ml_skip, lru_conv_w, lru_conv_b, lru_wa, lru_ba, lru_wx, lru_bx, lru_lambda, lru_norm, final_norm):
    bsz = x.shape[0]
    depth = mod_w.shape[0]
    lb_w = jax.nn.softmax(hg_lb_logits.astype(F32), axis=0)
    lower_bounds = jnp.cumsum(lb_w, axis=0) - lb_w[0]
    mod = _mod_call(c, mod_w, mod_b)
    mod5 = mod.reshape(depth, bsz, 3 * N_SUB, 1, D_MODEL)
    for l in range(depth):
        mod4 = mod5[l]
        bf = lambda a: a.astype(BF16)
        x = _ffn_call(x, norm_gain[l, 0], mod4, 0, bf(ffn_w1[l, 0]), bf(ffn_w3[l, 0]), bf(ffn_w2[l, 0]))
        u = _proj_call(x, norm_gain[l, 1], mod4, 1, bf(_pad_w_in(w_in[l])))
        params = _mixer_params(
            l, lower_bounds, hg_norm, rw_mu, rw_w0, rw_w2, rw_a0, rw_a2, rw_g2, rw_kk, rw_ka, rw_rk,
            rw_ln_w, rw_ln_b, ml_conv_w, ml_conv_b, ml_wq, ml_wk, ml_i_b, ml_f_b, ml_norm, ml_skip,
            lru_conv_w, lru_conv_b, lru_wa, lru_ba, lru_wx, lru_bx, lru_lambda, lru_norm)
        y = _mixer_call(u, params)
        x = _outproj_call(x, y, mod4, 1, bf(w_out[l]))
        last = l == depth - 1
        x = _ffn_call(x, norm_gain[l, 2], mod4, 2, bf(ffn_w1[l, 1]), bf(ffn_w3[l, 1]), bf(ffn_w2[l, 1]),
                      final_gain=final_norm if last else None)
    return x
```

```python
import functools

import jax
import jax.numpy as jnp
from jax import lax
from jax.experimental import pallas as pl
from jax.experimental.pallas import tpu as pltpu

F32 = jnp.float32
BF16 = jnp.bfloat16

D_MODEL = 2048
DEPTH = 2
GROUP_W = 512
N_SUB = 3
D_FF = 5632
NORM_EPS = 1e-6
HG_HEADS = 4
RW_HEAD = 64
RW_HEADS = 8
RW_W_LORA = 32
RW_A_LORA = 32
RW_G_LORA = 96
RW_LN_EPS = 64e-5
RW_IN = 3 * GROUP_W + RW_W_LORA + RW_A_LORA + RW_G_LORA
ML_HEADS = 4
ML_DIM = 128
ML_IN = 3 * GROUP_W + 2 * ML_HEADS
NEG_BIG = -1e30
LRU_C = 8.0
HG_IN = 4 * GROUP_W
LRU_IN = 2 * GROUP_W

LANES = 128
CHUNK = 64
MIX_ROWS = 2
SUB = 16
RW_PAD = 1792
ML_PAD = 1664
U_HG = 0
U_RW = U_HG + HG_IN
U_ML = U_RW + RW_PAD
U_LRU = U_ML + ML_PAD
U_TOT = 6656
LORA_PAD = RW_PAD - 3 * GROUP_W
VMEM_LIMIT = 56 * 1024 * 1024


def _sigmoid(x):
    return 0.5 * jnp.tanh(0.5 * x) + 0.5


def _silu(x):
    return x * _sigmoid(x)


def _softplus(x):
    return jnp.maximum(x, 0.0) + jnp.log(1.0 + jnp.exp(-jnp.abs(x)))


def _dot(a, b):
    return jnp.dot(a.astype(BF16), b.astype(BF16), preferred_element_type=F32)


def _dot_nt(a, b):
    return lax.dot_general(a.astype(BF16), b.astype(BF16), (((1,), (1,)), ((), ())),
                           preferred_element_type=F32)


def _dot_tn(a, b):
    return lax.dot_general(a.astype(BF16), b.astype(BF16), (((0,), (0,)), ((), ())),
                           preferred_element_type=F32)


def _split3(x):
    hi = x.astype(BF16)
    r1 = x - hi.astype(F32)
    mid = r1.astype(BF16)
    lo = (r1 - mid.astype(F32)).astype(BF16)
    return hi, mid, lo


def _cumsum_rows(x, tri):
    hi, mid, lo = _split3(x)
    d = functools.partial(jnp.dot, preferred_element_type=F32)
    return d(tri, hi) + d(tri, mid) + d(tri, lo)


def _seg_dot(x, seg):
    hi = x.astype(BF16)
    lo = (x - hi.astype(F32)).astype(BF16)
    d = functools.partial(jnp.dot, preferred_element_type=F32)
    return d(hi, seg) + d(lo, seg)


HIST = 8


def _push_history(hist_ref, x):
    hist_ref[0:HIST, :] = hist_ref[CHUNK:CHUNK + HIST, :]
    hist_ref[HIST:HIST + CHUNK, :] = x


def _delayed(hist_ref, d):
    return hist_ref[HIST - d:HIST - d + CHUNK, :]


def _modulated_norm(x, gain, shift, scale):
    y = x * lax.rsqrt(jnp.mean(x * x, axis=-1, keepdims=True) + NORM_EPS) * gain
    return y * (1.0 + scale) + shift


def _mod_kernel(c_ref, w_ref, b_ref, o_ref):
    o_ref[...] = _dot(_silu(c_ref[...]), w_ref[...]) + b_ref[...]


def _mod_call(c, mod_w, mod_b):
    depth, d, n = mod_w.shape
    bsz = c.shape[0]
    tn = 1024
    return pl.pallas_call(
        _mod_kernel,
        grid=(depth, n // tn),
        in_specs=[pl.BlockSpec((bsz, d), lambda l, j: (0, 0)),
                  pl.BlockSpec((None, d, tn), lambda l, j: (l, 0, j)),
                  pl.BlockSpec((None, 1, tn), lambda l, j: (l, 0, j))],
        out_specs=pl.BlockSpec((None, bsz, tn), lambda l, j: (l, 0, j)),
        out_shape=jax.ShapeDtypeStruct((depth, bsz, n), F32),
        compiler_params=pltpu.CompilerParams(
            dimension_semantics=("arbitrary", "arbitrary"), vmem_limit_bytes=VMEM_LIMIT),
        name="adaln_mod",
    )(c, mod_w, mod_b.reshape(depth, 1, n))


def _ffn_kernel(x_ref, gain_ref, shift_ref, scale_ref, gate_ref, w1_ref, w3_ref, w2_ref, *rest,
                n_ff, final):
    if final:
        fgain_ref, o_ref, h_ref = rest
    else:
        o_ref, h_ref = rest
    j = pl.program_id(2)

    @pl.when(j == 0)
    def _():
        h = _modulated_norm(x_ref[...], gain_ref[...], shift_ref[...], scale_ref[...])
        h_ref[...] = h.astype(BF16)
        o_ref[...] = jnp.zeros_like(o_ref)

    h = h_ref[...]
    a = jnp.dot(h, w1_ref[...], preferred_element_type=F32)
    b = jnp.dot(h, w3_ref[...], preferred_element_type=F32)
    g = (_silu(a) * b).astype(BF16)
    o_ref[...] += jnp.dot(g, w2_ref[...], preferred_element_type=F32)

    @pl.when(j == n_ff - 1)
    def _():
        xn = x_ref[...] + (0.5 * (1.0 + gate_ref[...])) * o_ref[...]
        if final:
            xn = xn * lax.rsqrt(jnp.mean(xn * xn, axis=-1, keepdims=True) + NORM_EPS) * fgain_ref[...]
        o_ref[...] = xn


def _ffn_call(x, gain, mod5, l, sub, s, w1, w3, w2, final_gain=None, tm=512, tf=512):
    bsz, seqlen, d = x.shape
    tm = min(tm, seqlen)
    n_ff = w1.shape[-1] // tf
    final = final_gain is not None
    vec = lambda k: pl.BlockSpec((None, None, None, 1, d), lambda b, i, j: (l, b, 3 * sub + k, 0, 0))
    in_specs = [pl.BlockSpec((None, tm, d), lambda b, i, j: (b, i, 0)),
                pl.BlockSpec((1, d), lambda b, i, j: (0, 0)),
                vec(0), vec(1), vec(2),
                pl.BlockSpec((None, None, d, tf), lambda b, i, j: (l, s, 0, j)),
                pl.BlockSpec((None, None, d, tf), lambda b, i, j: (l, s, 0, j)),
                pl.BlockSpec((None, None, tf, d), lambda b, i, j: (l, s, j, 0))]
    args = [x, gain.reshape(1, d), mod5, mod5, mod5, w1, w3, w2]
    if final:
        in_specs.append(pl.BlockSpec((1, d), lambda b, i, j: (0, 0)))
        args.append(final_gain.reshape(1, d))
    return pl.pallas_call(
        functools.partial(_ffn_kernel, n_ff=n_ff, final=final),
        grid=(bsz, seqlen // tm, n_ff),
        in_specs=in_specs,
        out_specs=pl.BlockSpec((None, tm, d), lambda b, i, j: (b, i, 0)),
        out_shape=jax.ShapeDtypeStruct(x.shape, F32),
        scratch_shapes=[pltpu.VMEM((tm, d), BF16)],
        compiler_params=pltpu.CompilerParams(
            dimension_semantics=("parallel", "parallel", "arbitrary"), vmem_limit_bytes=VMEM_LIMIT),
        name="ffn_final" if final else "ffn",
    )(*args)


def _proj_kernel(x_ref, gain_ref, shift_ref, scale_ref, w_ref, o_ref, h_ref):
    @pl.when(pl.program_id(2) == 0)
    def _():
        h = _modulated_norm(x_ref[...], gain_ref[...], shift_ref[...], scale_ref[...])
        h_ref[...] = h.astype(BF16)

    o_ref[...] = jnp.dot(h_ref[...], w_ref[...], preferred_element_type=F32)


def _proj_call(x, gain, mod5, l, sub, w, tm=512, tn=1664):
    bsz, seqlen, d = x.shape
    tm = min(tm, seqlen)
    n = w.shape[-1]
    vec = lambda k: pl.BlockSpec((None, None, None, 1, d), lambda b, i, j: (l, b, 3 * sub + k, 0, 0))
    return pl.pallas_call(
        _proj_kernel,
        grid=(bsz, seqlen // tm, n // tn),
        in_specs=[pl.BlockSpec((None, tm, d), lambda b, i, j: (b, i, 0)),
                  pl.BlockSpec((1, d), lambda b, i, j: (0, 0)),
                  vec(0), vec(1),
                  pl.BlockSpec((None, d, tn), lambda b, i, j: (l, 0, j))],
        out_specs=pl.BlockSpec((None, tm, tn), lambda b, i, j: (b, i, j)),
        out_shape=jax.ShapeDtypeStruct((bsz, seqlen, n), F32),
        scratch_shapes=[pltpu.VMEM((tm, d), BF16)],
        compiler_params=pltpu.CompilerParams(
            dimension_semantics=("parallel", "parallel", "arbitrary"), vmem_limit_bytes=VMEM_LIMIT),
        name="in_proj",
    )(x, gain.reshape(1, d), mod5, mod5, w)


def _outproj_kernel(x_ref, y_ref, gate_ref, w_ref, o_ref):
    o_ref[...] = x_ref[...] + (1.0 + gate_ref[...]) * jnp.dot(
        y_ref[...], w_ref[...], preferred_element_type=F32)


def _outproj_call(x, y, mod5, l, sub, w, tm=512):
    bsz, seqlen, d = x.shape
    tm = min(tm, seqlen)
    return pl.pallas_call(
        _outproj_kernel,
        grid=(bsz, seqlen // tm),
        in_specs=[pl.BlockSpec((None, tm, d), lambda b, i: (b, i, 0)),
                  pl.BlockSpec((None, tm, y.shape[-1]), lambda b, i: (b, i, 0)),
                  pl.BlockSpec((None, None, None, 1, d), lambda b, i: (l, b, 3 * sub + 2, 0, 0)),
                  pl.BlockSpec((None,) + w.shape[1:], lambda b, i: (l, 0, 0))],
        out_specs=pl.BlockSpec((None, tm, d), lambda b, i: (b, i, 0)),
        out_shape=jax.ShapeDtypeStruct(x.shape, F32),
        compiler_params=pltpu.CompilerParams(
            dimension_semantics=("parallel", "parallel"), vmem_limit_bytes=VMEM_LIMIT),
        name="out_proj",
    )(x, y, mod5, w)


def _hgrn2_chunk(u_ref, lb, gnorm, tri, st_ref, y_ref):
    gw = GROUP_W
    q = _silu(u_ref[:, U_HG:U_HG + gw])
    f_raw = u_ref[:, U_HG + gw:U_HG + 2 * gw]
    v = u_ref[:, U_HG + 2 * gw:U_HG + 3 * gw]
    g_raw = u_ref[:, U_HG + 3 * gw:U_HG + 4 * gw]
    e = jnp.exp(-f_raw)
    log_f = jnp.log(1.0 + lb * e) - jnp.log(1.0 + e)
    k = (1.0 - lb) * _sigmoid(-f_raw)
    yield
    bcum = _cumsum_rows(log_f, tri)
    yield
    sub_row = lax.broadcasted_iota(jnp.int32, (SUB, 1), 0)
    for h in range(HG_HEADS):
        sl = slice(h * LANES, (h + 1) * LANES)
        qh, kh, vh, bh = q[:, sl], k[:, sl], v[:, sl], bcum[:, sl]
        st = st_ref[h]
        o_inter = _dot_nt(qh * jnp.exp(bh), st)
        pieces = []
        for i in range(CHUNK // SUB):
            r0 = i * SUB
            qd, kd, vd, bd = qh[r0:r0 + SUB], kh[r0:r0 + SUB], vh[r0:r0 + SUB], bh[r0:r0 + SUB]
            oi = jnp.zeros((SUB, LANES), F32)
            if i > 0:
                bref = bh[r0 - 1:r0]
                qi = qd * jnp.exp(bd - bref)
                kp = kh[0:r0] * jnp.exp(bref - bh[0:r0])
                oi = _dot(_dot_nt(qi, kp), vh[0:r0])
            for j in range(SUB):
                w = jnp.exp(jnp.minimum(bd - bd[j:j + 1], 0.0))
                col = jnp.sum(qd * w * kd[j:j + 1], axis=-1, keepdims=True)
                col = jnp.where(sub_row >= j, col, 0.0)
                oi = oi + col * vd[j:j + 1]
            pieces.append(oi)
            yield
        o = o_inter + jnp.concatenate(pieces, axis=0)
        b_last = bh[CHUNK - 1:CHUNK]
        st_ref[h] = st * jnp.exp(b_last) + _dot_tn(vh, kh * jnp.exp(b_last - bh))
        on = o * lax.rsqrt(jnp.mean(o * o, axis=-1, keepdims=True) + NORM_EPS) * gnorm[:, sl]
        y_ref[:, sl] = (on * _silu(g_raw[:, sl])).astype(y_ref.dtype)
        yield


def _rwkv7_chunk(u_ref, p, tri, prev_ref, ht_ref, o_scr, y_ref):
    gw = GROUP_W
    u_raw = u_ref[:, U_RW:U_RW + RW_PAD]
    _push_history(prev_ref, u_raw)
    u = u_raw + (_delayed(prev_ref, 1) - u_raw) * p["mu"][...]
    yield
    r, k, v, lora = u[:, 0:gw], u[:, gw:2 * gw], u[:, 2 * gw:3 * gw], u[:, 3 * gw:RW_PAD]
    zw = p["w0"][...] + _dot(jnp.tanh(lora), p["w2"][...])
    lw = -jnp.exp(-_softplus(-zw) - 0.5)
    iclr = _sigmoid(p["a0"][...] + _dot(lora, p["a2"][...]))
    gate = _dot(_sigmoid(lora), p["g2"][...])
    yield
    kk = k * p["kk"][...]
    ss = _seg_dot(kk * kk, p["seg1"][...])
    kk = kk / jnp.maximum(jnp.sqrt(ss + 1e-12), 1e-6)
    k = k * (1.0 + (iclr - 1.0) * p["ka"][...])
    a_vec = -kk
    b_vec = kk * iclr
    yield

    lc = _cumsum_rows(lw, tri)
    l_last = lc[CHUNK - 1:CHUNK]
    g_in = jnp.exp(lc)
    g_out = jnp.exp(-lc)
    g_end = jnp.exp(l_last - lc)
    a_s = a_vec * jnp.exp(lc - lw)
    r_s = r * g_in
    b_s = b_vec * g_out
    k_s = k * g_out
    b_e = b_vec * g_end
    k_e = k * g_end
    decay_end = jnp.exp(l_last)
    yield

    lane = lax.broadcasted_iota(jnp.int32, (1, LANES), 1)
    m_a = (lane < RW_HEAD).astype(F32)
    m_b = 1.0 - m_a
    stack = lambda t: jnp.concatenate([t * m_a, t * m_b], axis=0)
    n2 = 2 * CHUNK
    row = lax.broadcasted_iota(jnp.int32, (n2, n2), 0)
    col = lax.broadcasted_iota(jnp.int32, (n2, n2), 1)
    same = (row // CHUNK) == (col // CHUNK)
    strict = same & (row > col)
    incl = same & (row >= col)
    pairs = range(RW_HEADS // 2)
    sls = [slice(pr * LANES, (pr + 1) * LANES) for pr in pairs]
    la = [stack(a_s[:, sl]) for sl in sls]
    lr = [stack(r_s[:, sl]) for sl in sls]
    v_st = [stack(v[:, sl]) for sl in sls]
    ht = [ht_ref[pr] for pr in pairs]
    sc = [_dot_nt(jnp.concatenate([la[pr], lr[pr]], axis=0),
                  jnp.concatenate([stack(b_s[:, sls[pr]]), stack(k_s[:, sls[pr]])], axis=0))
          for pr in pairs]
    yield
    pw = [jnp.where(strict, s[0:n2, 0:n2], 0.0) for s in sc]
    x = [_dot(jnp.where(strict, sc[pr][0:n2, n2:2 * n2], 0.0), v_st[pr]) + _dot_nt(la[pr], ht[pr])
         for pr in pairs]
    for step in range(6):
        x = [x[pr] + _dot(pw[pr], x[pr]) for pr in pairs]
        if step < 5:
            pw = [_dot(m, m) for m in pw]
        yield
    for pr in pairs:
        sl = sls[pr]
        a_rb = jnp.where(incl, sc[pr][n2:2 * n2, 0:n2], 0.0)
        a_rk = jnp.where(incl, sc[pr][n2:2 * n2, n2:2 * n2], 0.0)
        o_st = _dot_nt(lr[pr], ht[pr]) + _dot(a_rb, x[pr]) + _dot(a_rk, v_st[pr])
        o_scr[:, sl] = o_st[0:CHUNK] + o_st[CHUNK:n2]
        ht_ref[pr] = (ht[pr] * decay_end[:, sl] + _dot_tn(x[pr], stack(b_e[:, sl]))
                      + _dot_tn(v_st[pr], stack(k_e[:, sl])))
        yield

    o = o_scr[...]
    mean = _seg_dot(o, p["segm"][...])
    cen = o - mean
    var = _seg_dot(cen * cen, p["segm"][...])
    yield
    o = cen * lax.rsqrt(var + RW_LN_EPS) * p["ln_w"][...] + p["ln_b"][...]
    bonus = _seg_dot(r * k * p["rk"][...], p["seg1"][...]) * v
    y_ref[...] = ((o + bonus) * gate).astype(y_ref.dtype)


def _causal_conv4(x, hist_ref, w_ref, b_ref):
    _push_history(hist_ref, x)
    y = x * w_ref[3:4, :] + b_ref[...]
    for d in (1, 2, 3):
        y = y + _delayed(hist_ref, d) * w_ref[3 - d:4 - d, :]
    return y


def _mlstm_chunk(u_ref, p, tri, prev_ref, m_ref, n_ref, run_ref, y_ref):
    gw = GROUP_W
    xm = u_ref[:, U_ML:U_ML + gw]
    v = u_ref[:, U_ML + gw:U_ML + 2 * gw]
    o_raw = u_ref[:, U_ML + 2 * gw:U_ML + 3 * gw]
    gates = u_ref[:, U_ML + 3 * gw:U_ML + ML_PAD] + p["gate_b"][...]
    xc = _silu(_causal_conv4(xm, prev_ref, p["conv_w"], p["conv_b"]))
    q = _dot(xc, p["wq"][...]) * (ML_DIM ** -0.5)
    k = _dot(xc, p["wk"][...])
    yield
    log_f = jnp.minimum(gates, 0.0) - jnp.log(1.0 + jnp.exp(-jnp.abs(gates)))
    bcum = _cumsum_rows(log_f, tri)
    row = lax.broadcasted_iota(jnp.int32, (CHUNK, CHUNK), 0)
    col = lax.broadcasted_iota(jnp.int32, (CHUNK, CHUNK), 1)
    causal = row >= col
    eye = row == col
    to_row = lambda c: jnp.sum(jnp.where(eye, c, 0.0), axis=0, keepdims=True)
    ones = jnp.ones((CHUNK, LANES), BF16)
    for h in range(ML_HEADS):
        sl = slice(h * LANES, (h + 1) * LANES)
        qh, kh, vh = q[:, sl], k[:, sl], v[:, sl]
        b_col = bcum[:, ML_HEADS + h:ML_HEADS + h + 1]
        i_col = gates[:, h:h + 1]
        m_prev = run_ref[h][0:1, 0:1]
        n_prev = n_ref[h][0:1, :]
        mat = m_ref[h]
        lw = b_col - to_row(b_col) + to_row(i_col)
        lprev = b_col + m_prev
        m_t = jnp.maximum(lprev, jnp.max(jnp.where(causal, lw, NEG_BIG), axis=-1, keepdims=True))
        wts = jnp.where(causal, jnp.exp(jnp.where(causal, lw - m_t, 0.0)), 0.0)
        s = _dot_nt(qh, kh) * wts
        wp = jnp.exp(lprev - m_t)
        yield
        num = _dot(s, vh) + wp * _dot(qh, mat)
        den = _dot(s, ones)[:, 0:1] + wp * _dot_nt(qh, n_ref[h])[:, 0:1]
        hh = num / jnp.maximum(jnp.abs(den), jnp.exp(-m_t))
        yield
        m_new = m_t[CHUNK - 1:CHUNK]
        b_end = b_col[CHUNK - 1:CHUNK]
        wl = jnp.exp(b_end - b_col + i_col - m_new)
        dec = jnp.exp(b_end + m_prev - m_new)
        kw = kh * wl
        m_ref[h] = dec * mat + _dot_tn(kw, vh)
        n_ref[h] = jnp.broadcast_to(dec * n_prev + jnp.sum(kw, axis=0, keepdims=True), (8, LANES))
        run_ref[h] = jnp.broadcast_to(m_new, (8, LANES))
        hg = _sigmoid(o_raw[:, sl]) * hh
        hn = hg * lax.rsqrt(jnp.mean(hg * hg, axis=-1, keepdims=True) + NORM_EPS) * p["norm"][:, sl]
        y_ref[:, sl] = (hn + p["skip"][:, sl] * xc[:, sl]).astype(y_ref.dtype)
        yield


def _rglru_chunk(u_ref, p, prev_ref, h_ref, y_ref):
    gw = GROUP_W
    xb = u_ref[:, U_LRU:U_LRU + gw]
    gb = u_ref[:, U_LRU + gw:U_LRU + 2 * gw]
    xc = _causal_conv4(xb, prev_ref, p["conv_w"], p["conv_b"])
    r = _sigmoid(_dot(xc, p["wa"][...]) + p["ba"][...])
    i = _sigmoid(_dot(xc, p["wx"][...]) + p["bx"][...])
    yield
    log_a = -LRU_C * r * _softplus(-p["lam"][...])
    a = jnp.exp(log_a)
    bt = jnp.sqrt(1.0 - jnp.exp(2.0 * log_a)) * (i * xc)
    rows = lax.broadcasted_iota(jnp.int32, a.shape, 0)
    d = 1
    while d < CHUNK:
        keep = rows >= d
        bt = a * jnp.where(keep, pltpu.roll(bt, d, 0), 0.0) + bt
        a = a * jnp.where(keep, pltpu.roll(a, d, 0), 1.0)
        d *= 2
        yield
    hs = bt + a * h_ref[0:1, :]
    h_ref[...] = jnp.broadcast_to(hs[CHUNK - 1:CHUNK], h_ref.shape)
    gelu = 0.5 * gb * (1.0 + jnp.tanh(0.7978845608028654 * (gb + 0.044715 * (gb * gb * gb))))
    yl = hs * gelu
    y_ref[...] = (yl * lax.rsqrt(jnp.mean(yl * yl, axis=-1, keepdims=True) + NORM_EPS)
                  * p["norm"][...]).astype(y_ref.dtype)


def _round_robin(stages):
    live = [[g, 0, n] for g, n in stages]
    while live:
        entry = min(live, key=lambda e: (e[1] + 1) / e[2])
        try:
            next(entry[0])
            entry[1] += 1
        except StopIteration:
            live.remove(entry)


_HG_KEYS = ("lb", "gnorm")
_RW_KEYS = ("mu", "w0", "w2", "a0", "a2", "g2", "kk", "ka", "rk", "ln_w", "ln_b", "seg1", "segm")
_ML_KEYS = ("conv_w", "conv_b", "wq", "wk", "gate_b", "norm", "skip")
_LRU_KEYS = ("conv_w", "conv_b", "wa", "ba", "wx", "bx", "lam", "norm")


def _mixer_kernel(*refs):
    it = iter(refs)
    u_ref = next(it)
    hg = {k: next(it) for k in _HG_KEYS}
    rw = {k: next(it) for k in _RW_KEYS}
    ml = {k: next(it) for k in _ML_KEYS}
    lru = {k: next(it) for k in _LRU_KEYS}
    y_ref = next(it)
    (hg_st, rw_prev, rw_ht, rw_o, ml_prev, ml_m, ml_n, ml_run, lru_prev, lru_h) = it

    @pl.when(pl.program_id(1) == 0)
    def _():
        for ref in (hg_st, rw_prev, rw_ht, ml_prev, ml_m, ml_n, ml_run, lru_prev, lru_h):
            ref[...] = jnp.zeros_like(ref)

    r_i = lax.broadcasted_iota(jnp.int32, (CHUNK, CHUNK), 0)
    c_i = lax.broadcasted_iota(jnp.int32, (CHUNK, CHUNK), 1)
    tri = (r_i >= c_i).astype(BF16)
    gw = GROUP_W
    stages = []
    for r in range(u_ref.shape[0]):
        u_r, y_r = u_ref.at[r], y_ref.at[r]
        stages += [
            (_hgrn2_chunk(u_r, hg["lb"][...], hg["gnorm"][...], tri, hg_st.at[r], y_r.at[:, 0:gw]),
             2 + HG_HEADS * (CHUNK // SUB + 1)),
            (_rwkv7_chunk(u_r, rw, tri, rw_prev.at[r], rw_ht.at[r], rw_o.at[r], y_r.at[:, gw:2 * gw]),
             12 + RW_HEADS // 2),
            (_mlstm_chunk(u_r, ml, tri, ml_prev.at[r], ml_m.at[r], ml_n.at[r], ml_run.at[r],
                          y_r.at[:, 2 * gw:3 * gw]), 1 + 3 * ML_HEADS),
            (_rglru_chunk(u_r, lru, lru_prev.at[r], lru_h.at[r], y_r.at[:, 3 * gw:4 * gw]), 7)]
    _round_robin(stages)


def _block_diag(blocks):
    n, a, b = blocks.shape
    eye = jnp.eye(n, dtype=blocks.dtype)
    return (eye[:, None, :, None] * blocks[:, :, None, :]).reshape(n * a, n * b)


def _mixer_params(l, lower_bounds, hg_norm, rw_mu, rw_w0, rw_w2, rw_a0, rw_a2, rw_g2, rw_kk, rw_ka,
                  rw_rk, rw_ln_w, rw_ln_b, ml_conv_w, ml_conv_b, ml_wq, ml_wk, ml_i_b, ml_f_b,
                  ml_norm, ml_skip, lru_conv_w, lru_conv_b, lru_wa, lru_ba, lru_wx, lru_bx,
                  lru_lambda, lru_norm):
    gw = GROUP_W
    row = lambda a: a.reshape(1, -1).astype(F32)

    def lora_rows(w, start):
        z = jnp.zeros((LORA_PAD, gw), F32)
        return lax.dynamic_update_slice(z, w.astype(F32), (start, 0)).astype(BF16)

    head_id = jnp.arange(gw) // RW_HEAD
    seg1 = (head_id[:, None] == head_id[None, :]).astype(F32)
    hg = dict(lb=row(lower_bounds[l]), gnorm=row(hg_norm[l]))
    rw = dict(mu=row(jnp.pad(rw_mu[l], (0, RW_PAD - RW_IN))), w0=row(rw_w0[l]),
              w2=lora_rows(rw_w2[l], 0), a0=row(rw_a0[l]),
              a2=lora_rows(rw_a2[l], RW_W_LORA), g2=lora_rows(rw_g2[l], RW_W_LORA + RW_A_LORA),
              kk=row(rw_kk[l]), ka=row(rw_ka[l]), rk=row(rw_rk[l]), ln_w=row(rw_ln_w[l]),
              ln_b=row(rw_ln_b[l]), seg1=seg1.astype(BF16), segm=(seg1 / RW_HEAD).astype(BF16))
    gate_b = jnp.concatenate([ml_i_b[l], ml_f_b[l], jnp.zeros((LANES - 2 * ML_HEADS,), F32)])
    ml = dict(conv_w=ml_conv_w[l].astype(F32), conv_b=row(ml_conv_b[l]),
              wq=_block_diag(ml_wq[l]).astype(BF16), wk=_block_diag(ml_wk[l]).astype(BF16),
              gate_b=row(gate_b), norm=row(ml_norm[l]), skip=row(ml_skip[l]))
    lru = dict(conv_w=lru_conv_w[l].astype(F32), conv_b=row(lru_conv_b[l]),
               wa=_block_diag(lru_wa[l]).astype(BF16), ba=row(lru_ba[l]),
               wx=_block_diag(lru_wx[l]).astype(BF16), bx=row(lru_bx[l]),
               lam=row(lru_lambda[l]), norm=row(lru_norm[l]))
    return ([hg[k] for k in _HG_KEYS] + [rw[k] for k in _RW_KEYS]
            + [ml[k] for k in _ML_KEYS] + [lru[k] for k in _LRU_KEYS])


def _mixer_call(u, params):
    bsz, seqlen, _ = u.shape
    d_mix = 4 * GROUP_W
    full = lambda a: pl.BlockSpec(a.shape, lambda b, c: (0,) * a.ndim)
    rows = MIX_ROWS if bsz % MIX_ROWS == 0 else 1
    scratch = [
        pltpu.VMEM((rows, HG_HEADS, LANES, LANES), F32),
        pltpu.VMEM((rows, HIST + CHUNK, RW_PAD), F32),
        pltpu.VMEM((rows, RW_HEADS // 2, LANES, LANES), F32),
        pltpu.VMEM((rows, CHUNK, GROUP_W), F32),
        pltpu.VMEM((rows, HIST + CHUNK, GROUP_W), F32),
        pltpu.VMEM((rows, ML_HEADS, LANES, LANES), F32),
        pltpu.VMEM((rows, ML_HEADS, 8, LANES), F32),
        pltpu.VMEM((rows, ML_HEADS, 8, LANES), F32),
        pltpu.VMEM((rows, HIST + CHUNK, GROUP_W), F32),
        pltpu.VMEM((rows, 8, GROUP_W), F32),
    ]
    return pl.pallas_call(
        _mixer_kernel,
        grid=(bsz // rows, seqlen // CHUNK),
        in_specs=[pl.BlockSpec((rows, CHUNK, U_TOT), lambda b, c: (b, c, 0))]
        + [full(a) for a in params],
        out_specs=pl.BlockSpec((rows, CHUNK, d_mix), lambda b, c: (b, c, 0)),
        out_shape=jax.ShapeDtypeStruct((bsz, seqlen, d_mix), BF16),
        scratch_shapes=scratch,
        compiler_params=pltpu.CompilerParams(
            dimension_semantics=("parallel", "arbitrary"), vmem_limit_bytes=VMEM_LIMIT),
        name="mixers",
    )(u, *params)


def _pad_w_in(w):
    z = lambda n: jnp.zeros(w.shape[:-1] + (n,), BF16)
    c1, c2 = HG_IN + RW_IN, HG_IN + RW_IN + ML_IN
    wb = w.astype(BF16)
    return jnp.concatenate(
        [wb[..., :c1], z(RW_PAD - RW_IN), wb[..., c1:c2], z(ML_PAD - ML_IN), wb[..., c2:],
         z(U_TOT - U_LRU - LRU_IN)], axis=-1)


def kernel(x, c, norm_gain, mod_w, mod_b, ffn_w1, ffn_w3, ffn_w2, w_in, w_out, hg_lb_logits, hg_norm, rw_mu, rw_w0, rw_w2, rw_a0, rw_a2, rw_g2, rw_kk, rw_ka, rw_rk, rw_ln_w, rw_ln_b, ml_conv_w, ml_conv_b, ml_wq, ml_wk, ml_i_b, ml_f_b, ml_norm, ml_skip, lru_conv_w, lru_conv_b, lru_wa, lru_ba, lru_wx, lru_bx, lru_lambda, lru_norm, final_norm):
    bsz = x.shape[0]
    depth = mod_w.shape[0]
    lb_w = jax.nn.softmax(hg_lb_logits.astype(F32), axis=0)
    lower_bounds = jnp.cumsum(lb_w, axis=0) - lb_w[0]
    mod = _mod_call(c, mod_w, mod_b)
    mod5 = mod.reshape(depth, bsz, 3 * N_SUB, 1, D_MODEL)
    w1, w3, w2 = ffn_w1.astype(BF16), ffn_w3.astype(BF16), ffn_w2.astype(BF16)
    w_in_p, w_out_b = _pad_w_in(w_in), w_out.astype(BF16)
    for l in range(depth):
        x = _ffn_call(x, norm_gain[l, 0], mod5, l, 0, 0, w1, w3, w2)
        u = _proj_call(x, norm_gain[l, 1], mod5, l, 1, w_in_p)
        params = _mixer_params(
            l, lower_bounds, hg_norm, rw_mu, rw_w0, rw_w2, rw_a0, rw_a2, rw_g2, rw_kk, rw_ka, rw_rk,
            rw_ln_w, rw_ln_b, ml_conv_w, ml_conv_b, ml_wq, ml_wk, ml_i_b, ml_f_b, ml_norm, ml_skip,
            lru_conv_w, lru_conv_b, lru_wa, lru_ba, lru_wx, lru_bx, lru_lambda, lru_norm)
        y = _mixer_call(u, params)
        x = _outproj_call(x, y, mod5, l, 1, w_out_b)
        last = l == depth - 1
        x = _ffn_call(x, norm_gain[l, 2], mod5, l, 2, 1, w1, w3, w2,
                      final_gain=final_norm if last else None)
    return x
```

```python
import functools

import jax
import jax.numpy as jnp
from jax import lax
from jax.experimental import pallas as pl
from jax.experimental.pallas import tpu as pltpu

F32 = jnp.float32
BF16 = jnp.bfloat16

D_MODEL = 2048
DEPTH = 2
GROUP_W = 512
N_SUB = 3
D_FF = 5632
NORM_EPS = 1e-6
HG_HEADS = 4
RW_HEAD = 64
RW_HEADS = 8
RW_W_LORA = 32
RW_A_LORA = 32
RW_G_LORA = 96
RW_LN_EPS = 64e-5
RW_IN = 3 * GROUP_W + RW_W_LORA + RW_A_LORA + RW_G_LORA
ML_HEADS = 4
ML_DIM = 128
ML_IN = 3 * GROUP_W + 2 * ML_HEADS
NEG_BIG = -1e30
LRU_C = 8.0
HG_IN = 4 * GROUP_W
LRU_IN = 2 * GROUP_W

LANES = 128
CHUNK = 64
MIX_ROWS = 2
SUB = 16
RW_PAD = 1792
ML_PAD = 1664
U_HG = 0
U_RW = U_HG + HG_IN
U_ML = U_RW + RW_PAD
U_LRU = U_ML + ML_PAD
U_TOT = 6656
LORA_PAD = RW_PAD - 3 * GROUP_W
VMEM_LIMIT = 56 * 1024 * 1024
MIXER_VMEM_LIMIT = 60 * 1024 * 1024
PROJ_TILE = 512


def _sigmoid(x):
    return 0.5 * jnp.tanh(0.5 * x) + 0.5


def _silu(x):
    return x * _sigmoid(x)


def _softplus(x):
    return jnp.maximum(x, 0.0) + jnp.log(1.0 + jnp.exp(-jnp.abs(x)))


def _dot(a, b):
    return jnp.dot(a.astype(BF16), b.astype(BF16), preferred_element_type=F32)


def _dot_nt(a, b):
    return lax.dot_general(a.astype(BF16), b.astype(BF16), (((1,), (1,)), ((), ())),
                           preferred_element_type=F32)


def _dot_tn(a, b):
    return lax.dot_general(a.astype(BF16), b.astype(BF16), (((0,), (0,)), ((), ())),
                           preferred_element_type=F32)


def _split3(x):
    hi = x.astype(BF16)
    r1 = x - hi.astype(F32)
    mid = r1.astype(BF16)
    lo = (r1 - mid.astype(F32)).astype(BF16)
    return hi, mid, lo


def _cumsum_rows(x, tri):
    hi, mid, lo = _split3(x)
    d = functools.partial(jnp.dot, preferred_element_type=F32)
    return d(tri, hi) + d(tri, mid) + d(tri, lo)


def _seg_dot(x, seg):
    hi = x.astype(BF16)
    lo = (x - hi.astype(F32)).astype(BF16)
    d = functools.partial(jnp.dot, preferred_element_type=F32)
    return d(hi, seg) + d(lo, seg)


HIST = 8


def _push_history(hist_ref, x):
    hist_ref[0:HIST, :] = hist_ref[CHUNK:CHUNK + HIST, :]
    hist_ref[HIST:HIST + CHUNK, :] = x


def _delayed(hist_ref, d):
    return hist_ref[HIST - d:HIST - d + CHUNK, :]


def _modulated_norm(x, gain, shift, scale):
    y = x * lax.rsqrt(jnp.mean(x * x, axis=-1, keepdims=True) + NORM_EPS) * gain
    return y * (1.0 + scale) + shift


def _mod_kernel(c_ref, w_ref, b_ref, o_ref):
    o_ref[...] = _dot(_silu(c_ref[...]), w_ref[...]) + b_ref[...]


def _mod_call(c, mod_w, mod_b):
    depth, d, n = mod_w.shape
    bsz = c.shape[0]
    tn = 1024
    return pl.pallas_call(
        _mod_kernel,
        grid=(depth, n // tn),
        in_specs=[pl.BlockSpec((bsz, d), lambda l, j: (0, 0)),
                  pl.BlockSpec((None, d, tn), lambda l, j: (l, 0, j)),
                  pl.BlockSpec((None, 1, tn), lambda l, j: (l, 0, j))],
        out_specs=pl.BlockSpec((None, bsz, tn), lambda l, j: (l, 0, j)),
        out_shape=jax.ShapeDtypeStruct((depth, bsz, n), F32),
        compiler_params=pltpu.CompilerParams(
            dimension_semantics=("arbitrary", "arbitrary"), vmem_limit_bytes=VMEM_LIMIT),
        name="adaln_mod",
    )(c, mod_w, mod_b.reshape(depth, 1, n))


def _ffn_kernel(x_ref, gain_ref, shift_ref, scale_ref, gate_ref, w1_ref, w3_ref, w2_ref, *rest,
                n_ff, final):
    if final:
        fgain_ref, o_ref, h_ref = rest
    else:
        o_ref, h_ref = rest
    j = pl.program_id(2)

    @pl.when(j == 0)
    def _():
        h = _modulated_norm(x_ref[...], gain_ref[...], shift_ref[...], scale_ref[...])
        h_ref[...] = h.astype(BF16)
        o_ref[...] = jnp.zeros_like(o_ref)

    h = h_ref[...]
    a = jnp.dot(h, w1_ref[...], preferred_element_type=F32)
    b = jnp.dot(h, w3_ref[...], preferred_element_type=F32)
    g = (_silu(a) * b).astype(BF16)
    o_ref[...] += jnp.dot(g, w2_ref[...], preferred_element_type=F32)

    @pl.when(j == n_ff - 1)
    def _():
        xn = x_ref[...] + (0.5 * (1.0 + gate_ref[...])) * o_ref[...]
        if final:
            xn = xn * lax.rsqrt(jnp.mean(xn * xn, axis=-1, keepdims=True) + NORM_EPS) * fgain_ref[...]
        o_ref[...] = xn


def _ffn_call(x, gain, mod5, l, sub, s, w1, w3, w2, final_gain=None, tm=512, tf=512):
    bsz, seqlen, d = x.shape
    tm = min(tm, seqlen)
    n_ff = w1.shape[-1] // tf
    final = final_gain is not None
    vec = lambda k: pl.BlockSpec((None, None, None, 1, d), lambda b, i, j: (l, b, 3 * sub + k, 0, 0))
    in_specs = [pl.BlockSpec((None, tm, d), lambda b, i, j: (b, i, 0)),
                pl.BlockSpec((1, d), lambda b, i, j: (0, 0)),
                vec(0), vec(1), vec(2),
                pl.BlockSpec((None, None, d, tf), lambda b, i, j: (l, s, 0, j)),
                pl.BlockSpec((None, None, d, tf), lambda b, i, j: (l, s, 0, j)),
                pl.BlockSpec((None, None, tf, d), lambda b, i, j: (l, s, j, 0))]
    args = [x, gain.reshape(1, d), mod5, mod5, mod5, w1, w3, w2]
    if final:
        in_specs.append(pl.BlockSpec((1, d), lambda b, i, j: (0, 0)))
        args.append(final_gain.reshape(1, d))
    return pl.pallas_call(
        functools.partial(_ffn_kernel, n_ff=n_ff, final=final),
        grid=(bsz, seqlen // tm, n_ff),
        in_specs=in_specs,
        out_specs=pl.BlockSpec((None, tm, d), lambda b, i, j: (b, i, 0)),
        out_shape=jax.ShapeDtypeStruct(x.shape, F32),
        scratch_shapes=[pltpu.VMEM((tm, d), BF16)],
        compiler_params=pltpu.CompilerParams(
            dimension_semantics=("parallel", "parallel", "arbitrary"), vmem_limit_bytes=VMEM_LIMIT),
        name="ffn_final" if final else "ffn",
    )(*args)


def _outproj_kernel(x_ref, y_ref, gate_ref, w_ref, o_ref):
    o_ref[...] = x_ref[...] + (1.0 + gate_ref[...]) * jnp.dot(
        y_ref[...], w_ref[...], preferred_element_type=F32)


def _outproj_call(x, y, mod5, l, sub, w, tm=512):
    bsz, seqlen, d = x.shape
    tm = min(tm, seqlen)
    return pl.pallas_call(
        _outproj_kernel,
        grid=(bsz, seqlen // tm),
        in_specs=[pl.BlockSpec((None, tm, d), lambda b, i: (b, i, 0)),
                  pl.BlockSpec((None, tm, y.shape[-1]), lambda b, i: (b, i, 0)),
                  pl.BlockSpec((None, None, None, 1, d), lambda b, i: (l, b, 3 * sub + 2, 0, 0)),
                  pl.BlockSpec((None,) + w.shape[1:], lambda b, i: (l, 0, 0))],
        out_specs=pl.BlockSpec((None, tm, d), lambda b, i: (b, i, 0)),
        out_shape=jax.ShapeDtypeStruct(x.shape, F32),
        compiler_params=pltpu.CompilerParams(
            dimension_semantics=("parallel", "parallel"), vmem_limit_bytes=VMEM_LIMIT),
        name="out_proj",
    )(x, y, mod5, w)


def _hgrn2_chunk(u_ref, lb, gnorm, tri, st_ref, y_ref):
    gw = GROUP_W
    q = _silu(u_ref[:, U_HG:U_HG + gw])
    f_raw = u_ref[:, U_HG + gw:U_HG + 2 * gw]
    v = u_ref[:, U_HG + 2 * gw:U_HG + 3 * gw]
    g_raw = u_ref[:, U_HG + 3 * gw:U_HG + 4 * gw]
    e = jnp.exp(-f_raw)
    log_f = jnp.log(1.0 + lb * e) - jnp.log(1.0 + e)
    k = (1.0 - lb) * _sigmoid(-f_raw)
    yield
    bcum = _cumsum_rows(log_f, tri)
    yield
    sub_row = lax.broadcasted_iota(jnp.int32, (SUB, 1), 0)
    for h in range(HG_HEADS):
        sl = slice(h * LANES, (h + 1) * LANES)
        qh, kh, vh, bh = q[:, sl], k[:, sl], v[:, sl], bcum[:, sl]
        st = st_ref[h]
        o_inter = _dot_nt(qh * jnp.exp(bh), st)
        pieces = []
        for i in range(CHUNK // SUB):
            r0 = i * SUB
            qd, kd, vd, bd = qh[r0:r0 + SUB], kh[r0:r0 + SUB], vh[r0:r0 + SUB], bh[r0:r0 + SUB]
            oi = jnp.zeros((SUB, LANES), F32)
            if i > 0:
                bref = bh[r0 - 1:r0]
                qi = qd * jnp.exp(bd - bref)
                kp = kh[0:r0] * jnp.exp(bref - bh[0:r0])
                oi = _dot(_dot_nt(qi, kp), vh[0:r0])
            for j in range(SUB):
                w = jnp.exp(jnp.minimum(bd - bd[j:j + 1], 0.0))
                col = jnp.sum(qd * w * kd[j:j + 1], axis=-1, keepdims=True)
                col = jnp.where(sub_row >= j, col, 0.0)
                oi = oi + col * vd[j:j + 1]
            pieces.append(oi)
            yield
        o = o_inter + jnp.concatenate(pieces, axis=0)
        b_last = bh[CHUNK - 1:CHUNK]
        st_ref[h] = st * jnp.exp(b_last) + _dot_tn(vh, kh * jnp.exp(b_last - bh))
        on = o * lax.rsqrt(jnp.mean(o * o, axis=-1, keepdims=True) + NORM_EPS) * gnorm[:, sl]
        y_ref[:, sl] = (on * _silu(g_raw[:, sl])).astype(y_ref.dtype)
        yield


def _rwkv7_chunk(u_ref, p, tri, prev_ref, ht_ref, o_scr, y_ref):
    gw = GROUP_W
    u_raw = u_ref[:, U_RW:U_RW + RW_PAD]
    _push_history(prev_ref, u_raw)
    u = u_raw + (_delayed(prev_ref, 1) - u_raw) * p["mu"][...]
    yield
    r, k, v, lora = u[:, 0:gw], u[:, gw:2 * gw], u[:, 2 * gw:3 * gw], u[:, 3 * gw:RW_PAD]
    zw = p["w0"][...] + _dot(jnp.tanh(lora), p["w2"][...])
    lw = -jnp.exp(-_softplus(-zw) - 0.5)
    iclr = _sigmoid(p["a0"][...] + _dot(lora, p["a2"][...]))
    gate = _dot(_sigmoid(lora), p["g2"][...])
    yield
    kk = k * p["kk"][...]
    ss = _seg_dot(kk * kk, p["seg1"][...])
    kk = kk / jnp.maximum(jnp.sqrt(ss + 1e-12), 1e-6)
    k = k * (1.0 + (iclr - 1.0) * p["ka"][...])
    a_vec = -kk
    b_vec = kk * iclr
    yield

    lc = _cumsum_rows(lw, tri)
    l_last = lc[CHUNK - 1:CHUNK]
    g_in = jnp.exp(lc)
    g_out = jnp.exp(-lc)
    g_end = jnp.exp(l_last - lc)
    a_s = a_vec * jnp.exp(lc - lw)
    r_s = r * g_in
    b_s = b_vec * g_out
    k_s = k * g_out
    b_e = b_vec * g_end
    k_e = k * g_end
    decay_end = jnp.exp(l_last)
    yield

    lane = lax.broadcasted_iota(jnp.int32, (1, LANES), 1)
    m_a = (lane < RW_HEAD).astype(F32)
    m_b = 1.0 - m_a
    stack = lambda t: jnp.concatenate([t * m_a, t * m_b], axis=0)
    n2 = 2 * CHUNK
    row = lax.broadcasted_iota(jnp.int32, (n2, n2), 0)
    col = lax.broadcasted_iota(jnp.int32, (n2, n2), 1)
    same = (row // CHUNK) == (col // CHUNK)
    strict = same & (row > col)
    incl = same & (row >= col)
    pairs = range(RW_HEADS // 2)
    sls = [slice(pr * LANES, (pr + 1) * LANES) for pr in pairs]
    la = [stack(a_s[:, sl]) for sl in sls]
    lr = [stack(r_s[:, sl]) for sl in sls]
    v_st = [stack(v[:, sl]) for sl in sls]
    ht = [ht_ref[pr] for pr in pairs]
    sc = [_dot_nt(jnp.concatenate([la[pr], lr[pr]], axis=0),
                  jnp.concatenate([stack(b_s[:, sls[pr]]), stack(k_s[:, sls[pr]])], axis=0))
          for pr in pairs]
    yield
    pw = [jnp.where(strict, s[0:n2, 0:n2], 0.0) for s in sc]
    x = [_dot(jnp.where(strict, sc[pr][0:n2, n2:2 * n2], 0.0), v_st[pr]) + _dot_nt(la[pr], ht[pr])
         for pr in pairs]
    for step in range(6):
        x = [x[pr] + _dot(pw[pr], x[pr]) for pr in pairs]
        if step < 5:
            pw = [_dot(m, m) for m in pw]
        yield
    for pr in pairs:
        sl = sls[pr]
        a_rb = jnp.where(incl, sc[pr][n2:2 * n2, 0:n2], 0.0)
        a_rk = jnp.where(incl, sc[pr][n2:2 * n2, n2:2 * n2], 0.0)
        o_st = _dot_nt(lr[pr], ht[pr]) + _dot(a_rb, x[pr]) + _dot(a_rk, v_st[pr])
        o_scr[:, sl] = o_st[0:CHUNK] + o_st[CHUNK:n2]
        ht_ref[pr] = (ht[pr] * decay_end[:, sl] + _dot_tn(x[pr], stack(b_e[:, sl]))
                      + _dot_tn(v_st[pr], stack(k_e[:, sl])))
        yield

    o = o_scr[...]
    mean = _seg_dot(o, p["segm"][...])
    cen = o - mean
    var = _seg_dot(cen * cen, p["segm"][...])
    yield
    o = cen * lax.rsqrt(var + RW_LN_EPS) * p["ln_w"][...] + p["ln_b"][...]
    bonus = _seg_dot(r * k * p["rk"][...], p["seg1"][...]) * v
    y_ref[...] = ((o + bonus) * gate).astype(y_ref.dtype)


def _causal_conv4(x, hist_ref, w_ref, b_ref):
    _push_history(hist_ref, x)
    y = x * w_ref[3:4, :] + b_ref[...]
    for d in (1, 2, 3):
        y = y + _delayed(hist_ref, d) * w_ref[3 - d:4 - d, :]
    return y


def _mlstm_chunk(u_ref, p, tri, prev_ref, m_ref, n_ref, run_ref, y_ref):
    gw = GROUP_W
    xm = u_ref[:, U_ML:U_ML + gw]
    v = u_ref[:, U_ML + gw:U_ML + 2 * gw]
    o_raw = u_ref[:, U_ML + 2 * gw:U_ML + 3 * gw]
    gates = u_ref[:, U_ML + 3 * gw:U_ML + ML_PAD] + p["gate_b"][...]
    xc = _silu(_causal_conv4(xm, prev_ref, p["conv_w"], p["conv_b"]))
    q = _dot(xc, p["wq"][...]) * (ML_DIM ** -0.5)
    k = _dot(xc, p["wk"][...])
    yield
    log_f = jnp.minimum(gates, 0.0) - jnp.log(1.0 + jnp.exp(-jnp.abs(gates)))
    bcum = _cumsum_rows(log_f, tri)
    row = lax.broadcasted_iota(jnp.int32, (CHUNK, CHUNK), 0)
    col = lax.broadcasted_iota(jnp.int32, (CHUNK, CHUNK), 1)
    causal = row >= col
    eye = row == col
    to_row = lambda c: jnp.sum(jnp.where(eye, c, 0.0), axis=0, keepdims=True)
    ones = jnp.ones((CHUNK, LANES), BF16)
    for h in range(ML_HEADS):
        sl = slice(h * LANES, (h + 1) * LANES)
        qh, kh, vh = q[:, sl], k[:, sl], v[:, sl]
        b_col = bcum[:, ML_HEADS + h:ML_HEADS + h + 1]
        i_col = gates[:, h:h + 1]
        m_prev = run_ref[h][0:1, 0:1]
        n_prev = n_ref[h][0:1, :]
        mat = m_ref[h]
        lw = b_col - to_row(b_col) + to_row(i_col)
        lprev = b_col + m_prev
        m_t = jnp.maximum(lprev, jnp.max(jnp.where(causal, lw, NEG_BIG), axis=-1, keepdims=True))
        wts = jnp.where(causal, jnp.exp(jnp.where(causal, lw - m_t, 0.0)), 0.0)
        s = _dot_nt(qh, kh) * wts
        wp = jnp.exp(lprev - m_t)
        yield
        num = _dot(s, vh) + wp * _dot(qh, mat)
        den = _dot(s, ones)[:, 0:1] + wp * _dot_nt(qh, n_ref[h])[:, 0:1]
        hh = num / jnp.maximum(jnp.abs(den), jnp.exp(-m_t))
        yield
        m_new = m_t[CHUNK - 1:CHUNK]
        b_end = b_col[CHUNK - 1:CHUNK]
        wl = jnp.exp(b_end - b_col + i_col - m_new)
        dec = jnp.exp(b_end + m_prev - m_new)
        kw = kh * wl
        m_ref[h] = dec * mat + _dot_tn(kw, vh)
        n_ref[h] = jnp.broadcast_to(dec * n_prev + jnp.sum(kw, axis=0, keepdims=True), (8, LANES))
        run_ref[h] = jnp.broadcast_to(m_new, (8, LANES))
        hg = _sigmoid(o_raw[:, sl]) * hh
        hn = hg * lax.rsqrt(jnp.mean(hg * hg, axis=-1, keepdims=True) + NORM_EPS) * p["norm"][:, sl]
        y_ref[:, sl] = (hn + p["skip"][:, sl] * xc[:, sl]).astype(y_ref.dtype)
        yield


def _rglru_chunk(u_ref, p, prev_ref, h_ref, y_ref):
    gw = GROUP_W
    xb = u_ref[:, U_LRU:U_LRU + gw]
    gb = u_ref[:, U_LRU + gw:U_LRU + 2 * gw]
    xc = _causal_conv4(xb, prev_ref, p["conv_w"], p["conv_b"])
    r = _sigmoid(_dot(xc, p["wa"][...]) + p["ba"][...])
    i = _sigmoid(_dot(xc, p["wx"][...]) + p["bx"][...])
    yield
    log_a = -LRU_C * r * _softplus(-p["lam"][...])
    a = jnp.exp(log_a)
    bt = jnp.sqrt(1.0 - jnp.exp(2.0 * log_a)) * (i * xc)
    rows = lax.broadcasted_iota(jnp.int32, a.shape, 0)
    d = 1
    while d < CHUNK:
        keep = rows >= d
        bt = a * jnp.where(keep, pltpu.roll(bt, d, 0), 0.0) + bt
        a = a * jnp.where(keep, pltpu.roll(a, d, 0), 1.0)
        d *= 2
        yield
    hs = bt + a * h_ref[0:1, :]
    h_ref[...] = jnp.broadcast_to(hs[CHUNK - 1:CHUNK], h_ref.shape)
    gelu = 0.5 * gb * (1.0 + jnp.tanh(0.7978845608028654 * (gb + 0.044715 * (gb * gb * gb))))
    yl = hs * gelu
    y_ref[...] = (yl * lax.rsqrt(jnp.mean(yl * yl, axis=-1, keepdims=True) + NORM_EPS)
                  * p["norm"][...]).astype(y_ref.dtype)


def _round_robin(stages):
    live = [[g, 0, n] for g, n in stages]
    while live:
        entry = min(live, key=lambda e: (e[1] + 1) / e[2])
        try:
            next(entry[0])
            entry[1] += 1
        except StopIteration:
            live.remove(entry)


_HG_KEYS = ("lb", "gnorm")
_RW_KEYS = ("mu", "w0", "w2", "a0", "a2", "g2", "kk", "ka", "rk", "ln_w", "ln_b", "seg1", "segm")
_ML_KEYS = ("conv_w", "conv_b", "wq", "wk", "gate_b", "norm", "skip")
_LRU_KEYS = ("conv_w", "conv_b", "wa", "ba", "wx", "bx", "lam", "norm")


def _proj_stages(x_ref, t0, u_dst, gain, shift_ref, scale_ref, w_ref):
    rows = u_dst.shape[0]
    h = jnp.concatenate(
        [_modulated_norm(x_ref[r, t0:t0 + CHUNK, :], gain, shift_ref[r], scale_ref[r]).astype(BF16)
         for r in range(rows)], axis=0)
    yield
    for j in range(U_TOT // PROJ_TILE):
        cols = slice(j * PROJ_TILE, (j + 1) * PROJ_TILE)
        t = jnp.dot(h, w_ref[:, cols], preferred_element_type=F32)
        for r in range(rows):
            u_dst[r, :, cols] = t[r * CHUNK:(r + 1) * CHUNK]
        yield


def _mixer_kernel(*refs):
    it = iter(refs)
    x2_ref, xn_ref, gain_ref, shift_ref, scale_ref, w_ref = (next(it) for _ in range(6))
    hg = {k: next(it) for k in _HG_KEYS}
    rw = {k: next(it) for k in _RW_KEYS}
    ml = {k: next(it) for k in _ML_KEYS}
    lru = {k: next(it) for k in _LRU_KEYS}
    y_ref = next(it)
    (u_a, u_b, hg_st, rw_prev, rw_ht, rw_o, ml_prev, ml_m, ml_n, ml_run, lru_prev, lru_h) = it
    proj = functools.partial(_proj_stages, gain=gain_ref[...], shift_ref=shift_ref,
                             scale_ref=scale_ref, w_ref=w_ref)

    @pl.when(pl.program_id(1) == 0)
    def _():
        for ref in (hg_st, rw_prev, rw_ht, ml_prev, ml_m, ml_n, ml_run, lru_prev, lru_h):
            ref[...] = jnp.zeros_like(ref)
        for _ in proj(x2_ref, 0, u_a):
            pass

    r_i = lax.broadcasted_iota(jnp.int32, (CHUNK, CHUNK), 0)
    c_i = lax.broadcasted_iota(jnp.int32, (CHUNK, CHUNK), 1)
    tri = (r_i >= c_i).astype(BF16)
    gw = GROUP_W

    def mixers(u_buf, t0):
        stages = []
        for r in range(u_buf.shape[0]):
            u_r = u_buf.at[r]
            y_g = lambda g: y_ref.at[r, t0:t0 + CHUNK, g * gw:(g + 1) * gw]
            stages += [
                (_hgrn2_chunk(u_r, hg["lb"][...], hg["gnorm"][...], tri, hg_st.at[r], y_g(0)),
                 2 + HG_HEADS * (CHUNK // SUB + 1)),
                (_rwkv7_chunk(u_r, rw, tri, rw_prev.at[r], rw_ht.at[r], rw_o.at[r], y_g(1)),
                 12 + RW_HEADS // 2),
                (_mlstm_chunk(u_r, ml, tri, ml_prev.at[r], ml_m.at[r], ml_n.at[r], ml_run.at[r], y_g(2)),
                 1 + 3 * ML_HEADS),
                (_rglru_chunk(u_r, lru, lru_prev.at[r], lru_h.at[r], y_g(3)), 7)]
        return stages

    n_proj = 1 + U_TOT // PROJ_TILE
    _round_robin([(proj(x2_ref, CHUNK, u_b), n_proj)] + mixers(u_a, 0))
    _round_robin([(proj(xn_ref, 0, u_a), n_proj)] + mixers(u_b, CHUNK))


def _block_diag(blocks):
    n, a, b = blocks.shape
    eye = jnp.eye(n, dtype=blocks.dtype)
    return (eye[:, None, :, None] * blocks[:, :, None, :]).reshape(n * a, n * b)


def _mixer_params(l, lower_bounds, hg_norm, rw_mu, rw_w0, rw_w2, rw_a0, rw_a2, rw_g2, rw_kk, rw_ka,
                  rw_rk, rw_ln_w, rw_ln_b, ml_conv_w, ml_conv_b, ml_wq, ml_wk, ml_i_b, ml_f_b,
                  ml_norm, ml_skip, lru_conv_w, lru_conv_b, lru_wa, lru_ba, lru_wx, lru_bx,
                  lru_lambda, lru_norm):
    gw = GROUP_W
    row = lambda a: a.reshape(1, -1).astype(F32)

    def lora_rows(w, start):
        z = jnp.zeros((LORA_PAD, gw), F32)
        return lax.dynamic_update_slice(z, w.astype(F32), (start, 0)).astype(BF16)

    head_id = jnp.arange(gw) // RW_HEAD
    seg1 = (head_id[:, None] == head_id[None, :]).astype(F32)
    hg = dict(lb=row(lower_bounds[l]), gnorm=row(hg_norm[l]))
    rw = dict(mu=row(jnp.pad(rw_mu[l], (0, RW_PAD - RW_IN))), w0=row(rw_w0[l]),
              w2=lora_rows(rw_w2[l], 0), a0=row(rw_a0[l]),
              a2=lora_rows(rw_a2[l], RW_W_LORA), g2=lora_rows(rw_g2[l], RW_W_LORA + RW_A_LORA),
              kk=row(rw_kk[l]), ka=row(rw_ka[l]), rk=row(rw_rk[l]), ln_w=row(rw_ln_w[l]),
              ln_b=row(rw_ln_b[l]), seg1=seg1.astype(BF16), segm=(seg1 / RW_HEAD).astype(BF16))
    gate_b = jnp.concatenate([ml_i_b[l], ml_f_b[l], jnp.zeros((LANES - 2 * ML_HEADS,), F32)])
    ml = dict(conv_w=ml_conv_w[l].astype(F32), conv_b=row(ml_conv_b[l]),
              wq=_block_diag(ml_wq[l]).astype(BF16), wk=_block_diag(ml_wk[l]).astype(BF16),
              gate_b=row(gate_b), norm=row(ml_norm[l]), skip=row(ml_skip[l]))
    lru = dict(conv_w=lru_conv_w[l].astype(F32), conv_b=row(lru_conv_b[l]),
               wa=_block_diag(lru_wa[l]).astype(BF16), ba=row(lru_ba[l]),
               wx=_block_diag(lru_wx[l]).astype(BF16), bx=row(lru_bx[l]),
               lam=row(lru_lambda[l]), norm=row(lru_norm[l]))
    return ([hg[k] for k in _HG_KEYS] + [rw[k] for k in _RW_KEYS]
            + [ml[k] for k in _ML_KEYS] + [lru[k] for k in _LRU_KEYS])


def _mixer_call(x, gain, mod5, l, sub, w_in_p, params):
    bsz, seqlen, d = x.shape
    d_mix = 4 * GROUP_W
    rows = MIX_ROWS if bsz % MIX_ROWS == 0 else 1
    last_chunk = seqlen // CHUNK - 1
    once = pl.Buffered(1)
    full = lambda a: pl.BlockSpec(a.shape, lambda b, c: (0,) * a.ndim, pipeline_mode=once)
    vec = lambda k: pl.BlockSpec((None, rows, None, 1, d), lambda b, c: (l, b, 3 * sub + k, 0, 0))
    scratch = [
        pltpu.VMEM((rows, CHUNK, U_TOT), F32),
        pltpu.VMEM((rows, CHUNK, U_TOT), F32),
        pltpu.VMEM((rows, HG_HEADS, LANES, LANES), F32),
        pltpu.VMEM((rows, HIST + CHUNK, RW_PAD), F32),
        pltpu.VMEM((rows, RW_HEADS // 2, LANES, LANES), F32),
        pltpu.VMEM((rows, CHUNK, GROUP_W), F32),
        pltpu.VMEM((rows, HIST + CHUNK, GROUP_W), F32),
        pltpu.VMEM((rows, ML_HEADS, LANES, LANES), F32),
        pltpu.VMEM((rows, ML_HEADS, 8, LANES), F32),
        pltpu.VMEM((rows, ML_HEADS, 8, LANES), F32),
        pltpu.VMEM((rows, HIST + CHUNK, GROUP_W), F32),
        pltpu.VMEM((rows, 8, GROUP_W), F32),
    ]
    return pl.pallas_call(
        _mixer_kernel,
        grid=(bsz // rows, seqlen // (2 * CHUNK)),
        in_specs=[pl.BlockSpec((rows, 2 * CHUNK, d), lambda b, c: (b, c, 0)),
                  pl.BlockSpec((rows, CHUNK, d), lambda b, c: (b, jnp.minimum(2 * c + 2, last_chunk), 0)),
                  pl.BlockSpec((1, d), lambda b, c: (0, 0)),
                  vec(0), vec(1),
                  pl.BlockSpec((None, d, U_TOT), lambda b, c: (l, 0, 0), pipeline_mode=once)]
        + [full(a) for a in params],
        out_specs=pl.BlockSpec((rows, 2 * CHUNK, d_mix), lambda b, c: (b, c, 0)),
        out_shape=jax.ShapeDtypeStruct((bsz, seqlen, d_mix), BF16),
        scratch_shapes=scratch,
        compiler_params=pltpu.CompilerParams(
            dimension_semantics=("parallel", "arbitrary"), vmem_limit_bytes=MIXER_VMEM_LIMIT),
        name="mixers",
    )(x, x, gain.reshape(1, d), mod5, mod5, w_in_p, *params)


def _pad_w_in(w):
    z = lambda n: jnp.zeros(w.shape[:-1] + (n,), BF16)
    c1, c2 = HG_IN + RW_IN, HG_IN + RW_IN + ML_IN
    wb = w.astype(BF16)
    return jnp.concatenate(
        [wb[..., :c1], z(RW_PAD - RW_IN), wb[..., c1:c2], z(ML_PAD - ML_IN), wb[..., c2:],
         z(U_TOT - U_LRU - LRU_IN)], axis=-1)


def kernel(x, c, norm_gain, mod_w, mod_b, ffn_w1, ffn_w3, ffn_w2, w_in, w_out, hg_lb_logits, hg_norm, rw_mu, rw_w0, rw_w2, rw_a0, rw_a2, rw_g2, rw_kk, rw_ka, rw_rk, rw_ln_w, rw_ln_b, ml_conv_w, ml_conv_b, ml_wq, ml_wk, ml_i_b, ml_f_b, ml_norm, ml_skip, lru_conv_w, lru_conv_b, lru_wa, lru_ba, lru_wx, lru_bx, lru_lambda, lru_norm, final_norm):
    bsz = x.shape[0]
    depth = mod_w.shape[0]
    lb_w = jax.nn.softmax(hg_lb_logits.astype(F32), axis=0)
    lower_bounds = jnp.cumsum(lb_w, axis=0) - lb_w[0]
    mod = _mod_call(c, mod_w, mod_b)
    mod5 = mod.reshape(depth, bsz, 3 * N_SUB, 1, D_MODEL)
    w1, w3, w2 = ffn_w1.astype(BF16), ffn_w3.astype(BF16), ffn_w2.astype(BF16)
    w_in_p, w_out_b = _pad_w_in(w_in), w_out.astype(BF16)
    for l in range(depth):
        x = _ffn_call(x, norm_gain[l, 0], mod5, l, 0, 0, w1, w3, w2)
        params = _mixer_params(
            l, lower_bounds, hg_norm, rw_mu, rw_w0, rw_w2, rw_a0, rw_a2, rw_g2, rw_kk, rw_ka, rw_rk,
            rw_ln_w, rw_ln_b, ml_conv_w, ml_conv_b, ml_wq, ml_wk, ml_i_b, ml_f_b, ml_norm, ml_skip,
            lru_conv_w, lru_conv_b, lru_wa, lru_ba, lru_wx, lru_bx, lru_lambda, lru_norm)
        y = _mixer_call(x, norm_gain[l, 1], mod5, l, 1, w_in_p, params)
        x = _outproj_call(x, y, mod5, l, 1, w_out_b)
        last = l == depth - 1
        x = _ffn_call(x, norm_gain[l, 2], mod5, l, 2, 1, w1, w3, w2,
                      final_gain=final_norm if last else None)
    return x
```

```python
import functools

import jax
import jax.numpy as jnp
from jax import lax
from jax.experimental import pallas as pl
from jax.experimental.pallas import tpu as pltpu

F32 = jnp.float32
BF16 = jnp.bfloat16

D_MODEL = 2048
DEPTH = 2
GROUP_W = 512
N_SUB = 3
D_FF = 5632
NORM_EPS = 1e-6
HG_HEADS = 4
RW_HEAD = 64
RW_HEADS = 8
RW_W_LORA = 32
RW_A_LORA = 32
RW_G_LORA = 96
RW_LN_EPS = 64e-5
RW_IN = 3 * GROUP_W + RW_W_LORA + RW_A_LORA + RW_G_LORA
ML_HEADS = 4
ML_DIM = 128
ML_IN = 3 * GROUP_W + 2 * ML_HEADS
NEG_BIG = -1e30
LRU_C = 8.0
HG_IN = 4 * GROUP_W
LRU_IN = 2 * GROUP_W

LANES = 128
CHUNK = 64
MIX_ROWS = 2
SUB = 16
RW_PAD = 1792
ML_PAD = 1664
U_HG = 0
U_RW = U_HG + HG_IN
U_ML = U_RW + RW_PAD
U_LRU = U_ML + ML_PAD
U_TOT = 6656
LORA_PAD = RW_PAD - 3 * GROUP_W
VMEM_LIMIT = 56 * 1024 * 1024
MIXER_VMEM_LIMIT = 60 * 1024 * 1024
PROJ_TILE = 512
FFN_NORM_ROWS = 64


def _sigmoid(x):
    return 0.5 * jnp.tanh(0.5 * x) + 0.5


def _silu(x):
    return x * _sigmoid(x)


def _softplus(x):
    return jnp.maximum(x, 0.0) + jnp.log(1.0 + jnp.exp(-jnp.abs(x)))


def _dot(a, b):
    return jnp.dot(a.astype(BF16), b.astype(BF16), preferred_element_type=F32)


def _dot_nt(a, b):
    return lax.dot_general(a.astype(BF16), b.astype(BF16), (((1,), (1,)), ((), ())),
                           preferred_element_type=F32)


def _dot_tn(a, b):
    return lax.dot_general(a.astype(BF16), b.astype(BF16), (((0,), (0,)), ((), ())),
                           preferred_element_type=F32)


def _split3(x):
    hi = x.astype(BF16)
    r1 = x - hi.astype(F32)
    mid = r1.astype(BF16)
    lo = (r1 - mid.astype(F32)).astype(BF16)
    return hi, mid, lo


def _cumsum_rows(x, tri):
    hi, mid, lo = _split3(x)
    d = functools.partial(jnp.dot, preferred_element_type=F32)
    return d(tri, hi) + d(tri, mid) + d(tri, lo)


def _seg_dot(x, seg):
    hi = x.astype(BF16)
    lo = (x - hi.astype(F32)).astype(BF16)
    d = functools.partial(jnp.dot, preferred_element_type=F32)
    return d(hi, seg) + d(lo, seg)


HIST = 8


def _push_history(hist_ref, x):
    hist_ref[0:HIST, :] = hist_ref[CHUNK:CHUNK + HIST, :]
    hist_ref[HIST:HIST + CHUNK, :] = x


def _delayed(hist_ref, d):
    return hist_ref[HIST - d:HIST - d + CHUNK, :]


def _modulated_norm(x, gain, shift, scale):
    y = x * lax.rsqrt(jnp.mean(x * x, axis=-1, keepdims=True) + NORM_EPS) * gain
    return y * (1.0 + scale) + shift


def _mod_kernel(c_ref, w_ref, b_ref, o_ref):
    o_ref[...] = _dot(_silu(c_ref[...]), w_ref[...]) + b_ref[...]


def _mod_call(c, mod_w, mod_b):
    depth, d, n = mod_w.shape
    bsz = c.shape[0]
    tn = 1024
    return pl.pallas_call(
        _mod_kernel,
        grid=(depth, n // tn),
        in_specs=[pl.BlockSpec((bsz, d), lambda l, j: (0, 0)),
                  pl.BlockSpec((None, d, tn), lambda l, j: (l, 0, j)),
                  pl.BlockSpec((None, 1, tn), lambda l, j: (l, 0, j))],
        out_specs=pl.BlockSpec((None, bsz, tn), lambda l, j: (l, 0, j)),
        out_shape=jax.ShapeDtypeStruct((depth, bsz, n), F32),
        compiler_params=pltpu.CompilerParams(
            dimension_semantics=("arbitrary", "arbitrary"), vmem_limit_bytes=VMEM_LIMIT),
        name="adaln_mod",
    )(c, mod_w, mod_b.reshape(depth, 1, n))


def _ffn_kernel(x_ref, xnext_ref, gain_ref, shift_ref, scale_ref, shift_next_ref, scale_next_ref,
                gate_ref, w1_ref, w3_ref, w2_ref, *rest, n_tiles, n_ff, final):
    if final:
        fgain_ref, o_ref, h_ref = rest
    else:
        o_ref, h_ref = rest
    j = pl.program_id(2)
    tile = pl.program_id(0) * n_tiles + pl.program_id(1)
    slot = tile % 2
    tm = x_ref.shape[0]
    n_blocks = tm // FFN_NORM_ROWS

    @pl.when((tile == 0) & (j == 0))
    def _():
        h = _modulated_norm(x_ref[...], gain_ref[...], shift_ref[...], scale_ref[...])
        h_ref[0] = h.astype(BF16)

    h = h_ref[slot]
    a = jnp.dot(h, w1_ref[...], preferred_element_type=F32)
    b = jnp.dot(h, w3_ref[...], preferred_element_type=F32)
    g = (_silu(a) * b).astype(BF16)
    acc = jnp.where(j == 0, 0.0, o_ref[...])
    o_ref[...] = acc + jnp.dot(g, w2_ref[...], preferred_element_type=F32)

    r0 = pl.multiple_of(jnp.minimum(j, n_blocks - 1) * FFN_NORM_ROWS, FFN_NORM_ROWS)
    rows = pl.ds(r0, FFN_NORM_ROWS)
    h_next = _modulated_norm(xnext_ref[rows, :], gain_ref[...], shift_next_ref[...], scale_next_ref[...])
    h_ref[1 - slot, rows, :] = h_next.astype(BF16)

    @pl.when(j == n_ff - 1)
    def _():
        xn = x_ref[...] + (0.5 * (1.0 + gate_ref[...])) * o_ref[...]
        if final:
            xn = xn * lax.rsqrt(jnp.mean(xn * xn, axis=-1, keepdims=True) + NORM_EPS) * fgain_ref[...]
        o_ref[...] = xn


def _ffn_call(x, gain, mod5, l, sub, s, w1, w3, w2, final_gain=None, tm=512, tf=512):
    bsz, seqlen, d = x.shape
    tm = min(tm, seqlen)
    n_tiles = seqlen // tm
    n_ff = w1.shape[-1] // tf
    assert tm // FFN_NORM_ROWS <= n_ff
    final = final_gain is not None

    def nxt(b, i):
        t = jnp.minimum(b * n_tiles + i + 1, bsz * n_tiles - 1)
        return t // n_tiles, t % n_tiles

    vec = lambda k: pl.BlockSpec((None, None, None, 1, d), lambda b, i, j: (l, b, 3 * sub + k, 0, 0))
    vec_next = lambda k: pl.BlockSpec((None, None, None, 1, d),
                                      lambda b, i, j: (l, nxt(b, i)[0], 3 * sub + k, 0, 0))
    in_specs = [pl.BlockSpec((None, tm, d), lambda b, i, j: (b, i, 0)),
                pl.BlockSpec((None, tm, d), lambda b, i, j: nxt(b, i) + (0,)),
                pl.BlockSpec((1, d), lambda b, i, j: (0, 0)),
                vec(0), vec(1), vec_next(0), vec_next(1), vec(2),
                pl.BlockSpec((None, None, d, tf), lambda b, i, j: (l, s, 0, j)),
                pl.BlockSpec((None, None, d, tf), lambda b, i, j: (l, s, 0, j)),
                pl.BlockSpec((None, None, tf, d), lambda b, i, j: (l, s, j, 0))]
    args = [x, x, gain.reshape(1, d), mod5, mod5, mod5, mod5, mod5, w1, w3, w2]
    if final:
        in_specs.append(pl.BlockSpec((1, d), lambda b, i, j: (0, 0)))
        args.append(final_gain.reshape(1, d))
    return pl.pallas_call(
        functools.partial(_ffn_kernel, n_tiles=n_tiles, n_ff=n_ff, final=final),
        grid=(bsz, n_tiles, n_ff),
        in_specs=in_specs,
        out_specs=pl.BlockSpec((None, tm, d), lambda b, i, j: (b, i, 0)),
        out_shape=jax.ShapeDtypeStruct(x.shape, F32),
        scratch_shapes=[pltpu.VMEM((2, tm, d), BF16)],
        compiler_params=pltpu.CompilerParams(
            dimension_semantics=("arbitrary", "arbitrary", "arbitrary"), vmem_limit_bytes=VMEM_LIMIT),
        name="ffn_final" if final else "ffn",
    )(*args)


def _outproj_kernel(x_ref, y_ref, gate_ref, w_ref, o_ref):
    o_ref[...] = x_ref[...] + (1.0 + gate_ref[...]) * jnp.dot(
        y_ref[...], w_ref[...], preferred_element_type=F32)


def _outproj_call(x, y, mod5, l, sub, w, tm=512):
    bsz, seqlen, d = x.shape
    tm = min(tm, seqlen)
    return pl.pallas_call(
        _outproj_kernel,
        grid=(bsz, seqlen // tm),
        in_specs=[pl.BlockSpec((None, tm, d), lambda b, i: (b, i, 0)),
                  pl.BlockSpec((None, tm, y.shape[-1]), lambda b, i: (b, i, 0)),
                  pl.BlockSpec((None, None, None, 1, d), lambda b, i: (l, b, 3 * sub + 2, 0, 0)),
                  pl.BlockSpec((None,) + w.shape[1:], lambda b, i: (l, 0, 0))],
        out_specs=pl.BlockSpec((None, tm, d), lambda b, i: (b, i, 0)),
        out_shape=jax.ShapeDtypeStruct(x.shape, F32),
        compiler_params=pltpu.CompilerParams(
            dimension_semantics=("parallel", "parallel"), vmem_limit_bytes=VMEM_LIMIT),
        name="out_proj",
    )(x, y, mod5, w)


def _hgrn2_chunk(u_ref, lb, gnorm, tri, st_ref, y_ref):
    gw = GROUP_W
    q = _silu(u_ref[:, U_HG:U_HG + gw])
    f_raw = u_ref[:, U_HG + gw:U_HG + 2 * gw]
    v = u_ref[:, U_HG + 2 * gw:U_HG + 3 * gw]
    g_raw = u_ref[:, U_HG + 3 * gw:U_HG + 4 * gw]
    e = jnp.exp(-f_raw)
    log_f = jnp.log(1.0 + lb * e) - jnp.log(1.0 + e)
    k = (1.0 - lb) * _sigmoid(-f_raw)
    yield
    bcum = _cumsum_rows(log_f, tri)
    yield
    sub_row = lax.broadcasted_iota(jnp.int32, (SUB, 1), 0)
    for h in range(HG_HEADS):
        sl = slice(h * LANES, (h + 1) * LANES)
        qh, kh, vh, bh = q[:, sl], k[:, sl], v[:, sl], bcum[:, sl]
        st = st_ref[h]
        o_inter = _dot_nt(qh * jnp.exp(bh), st)
        pieces = []
        for i in range(CHUNK // SUB):
            r0 = i * SUB
            qd, kd, vd, bd = qh[r0:r0 + SUB], kh[r0:r0 + SUB], vh[r0:r0 + SUB], bh[r0:r0 + SUB]
            oi = jnp.zeros((SUB, LANES), F32)
            if i > 0:
                bref = bh[r0 - 1:r0]
                qi = qd * jnp.exp(bd - bref)
                kp = kh[0:r0] * jnp.exp(bref - bh[0:r0])
                oi = _dot(_dot_nt(qi, kp), vh[0:r0])
            for j in range(SUB):
                w = jnp.exp(jnp.minimum(bd - bd[j:j + 1], 0.0))
                col = jnp.sum(qd * w * kd[j:j + 1], axis=-1, keepdims=True)
                col = jnp.where(sub_row >= j, col, 0.0)
                oi = oi + col * vd[j:j + 1]
            pieces.append(oi)
            yield
        o = o_inter + jnp.concatenate(pieces, axis=0)
        b_last = bh[CHUNK - 1:CHUNK]
        st_ref[h] = st * jnp.exp(b_last) + _dot_tn(vh, kh * jnp.exp(b_last - bh))
        on = o * lax.rsqrt(jnp.mean(o * o, axis=-1, keepdims=True) + NORM_EPS) * gnorm[:, sl]
        y_ref[:, sl] = (on * _silu(g_raw[:, sl])).astype(y_ref.dtype)
        yield


def _rwkv7_chunk(u_ref, p, tri, prev_ref, ht_ref, o_scr, y_ref):
    gw = GROUP_W
    u_raw = u_ref[:, U_RW:U_RW + RW_PAD]
    _push_history(prev_ref, u_raw)
    u = u_raw + (_delayed(prev_ref, 1) - u_raw) * p["mu"][...]
    yield
    r, k, v, lora = u[:, 0:gw], u[:, gw:2 * gw], u[:, 2 * gw:3 * gw], u[:, 3 * gw:RW_PAD]
    zw = p["w0"][...] + _dot(jnp.tanh(lora), p["w2"][...])
    lw = -jnp.exp(-_softplus(-zw) - 0.5)
    iclr = _sigmoid(p["a0"][...] + _dot(lora, p["a2"][...]))
    gate = _dot(_sigmoid(lora), p["g2"][...])
    yield
    kk = k * p["kk"][...]
    ss = _seg_dot(kk * kk, p["seg1"][...])
    kk = kk / jnp.maximum(jnp.sqrt(ss + 1e-12), 1e-6)
    k = k * (1.0 + (iclr - 1.0) * p["ka"][...])
    a_vec = -kk
    b_vec = kk * iclr
    yield

    lc = _cumsum_rows(lw, tri)
    l_last = lc[CHUNK - 1:CHUNK]
    g_in = jnp.exp(lc)
    g_out = jnp.exp(-lc)
    g_end = jnp.exp(l_last - lc)
    a_s = a_vec * jnp.exp(lc - lw)
    r_s = r * g_in
    b_s = b_vec * g_out
    k_s = k * g_out
    b_e = b_vec * g_end
    k_e = k * g_end
    decay_end = jnp.exp(l_last)
    yield

    lane = lax.broadcasted_iota(jnp.int32, (1, LANES), 1)
    m_a = (lane < RW_HEAD).astype(F32)
    m_b = 1.0 - m_a
    stack = lambda t: jnp.concatenate([t * m_a, t * m_b], axis=0)
    n2 = 2 * CHUNK
    row = lax.broadcasted_iota(jnp.int32, (n2, n2), 0)
    col = lax.broadcasted_iota(jnp.int32, (n2, n2), 1)
    same = (row // CHUNK) == (col // CHUNK)
    strict = same & (row > col)
    incl = same & (row >= col)
    pairs = range(RW_HEADS // 2)
    sls = [slice(pr * LANES, (pr + 1) * LANES) for pr in pairs]
    la = [stack(a_s[:, sl]) for sl in sls]
    lr = [stack(r_s[:, sl]) for sl in sls]
    v_st = [stack(v[:, sl]) for sl in sls]
    ht = [ht_ref[pr] for pr in pairs]
    sc = [_dot_nt(jnp.concatenate([la[pr], lr[pr]], axis=0),
                  jnp.concatenate([stack(b_s[:, sls[pr]]), stack(k_s[:, sls[pr]])], axis=0))
          for pr in pairs]
    yield
    pw = [jnp.where(strict, s[0:n2, 0:n2], 0.0) for s in sc]
    x = [_dot(jnp.where(strict, sc[pr][0:n2, n2:2 * n2], 0.0), v_st[pr]) + _dot_nt(la[pr], ht[pr])
         for pr in pairs]
    for step in range(6):
        x = [x[pr] + _dot(pw[pr], x[pr]) for pr in pairs]
        if step < 5:
            pw = [_dot(m, m) for m in pw]
        yield
    for pr in pairs:
        sl = sls[pr]
        a_rb = jnp.where(incl, sc[pr][n2:2 * n2, 0:n2], 0.0)
        a_rk = jnp.where(incl, sc[pr][n2:2 * n2, n2:2 * n2], 0.0)
        o_st = _dot_nt(lr[pr], ht[pr]) + _dot(a_rb, x[pr]) + _dot(a_rk, v_st[pr])
        o_scr[:, sl] = o_st[0:CHUNK] + o_st[CHUNK:n2]
        ht_ref[pr] = (ht[pr] * decay_end[:, sl] + _dot_tn(x[pr], stack(b_e[:, sl]))
                      + _dot_tn(v_st[pr], stack(k_e[:, sl])))
        yield

    o = o_scr[...]
    mean = _seg_dot(o, p["segm"][...])
    cen = o - mean
    var = _seg_dot(cen * cen, p["segm"][...])
    yield
    o = cen * lax.rsqrt(var + RW_LN_EPS) * p["ln_w"][...] + p["ln_b"][...]
    bonus = _seg_dot(r * k * p["rk"][...], p["seg1"][...]) * v
    y_ref[...] = ((o + bonus) * gate).astype(y_ref.dtype)


def _causal_conv4(x, hist_ref, w_ref, b_ref):
    _push_history(hist_ref, x)
    y = x * w_ref[3:4, :] + b_ref[...]
    for d in (1, 2, 3):
        y = y + _delayed(hist_ref, d) * w_ref[3 - d:4 - d, :]
    return y


def _mlstm_chunk(u_ref, p, tri, prev_ref, m_ref, n_ref, run_ref, y_ref):
    gw = GROUP_W
    xm = u_ref[:, U_ML:U_ML + gw]
    v = u_ref[:, U_ML + gw:U_ML + 2 * gw]
    o_raw = u_ref[:, U_ML + 2 * gw:U_ML + 3 * gw]
    gates = u_ref[:, U_ML + 3 * gw:U_ML + ML_PAD] + p["gate_b"][...]
    xc = _silu(_causal_conv4(xm, prev_ref, p["conv_w"], p["conv_b"]))
    q = _dot(xc, p["wq"][...]) * (ML_DIM ** -0.5)
    k = _dot(xc, p["wk"][...])
    yield
    log_f = jnp.minimum(gates, 0.0) - jnp.log(1.0 + jnp.exp(-jnp.abs(gates)))
    bcum = _cumsum_rows(log_f, tri)
    row = lax.broadcasted_iota(jnp.int32, (CHUNK, CHUNK), 0)
    col = lax.broadcasted_iota(jnp.int32, (CHUNK, CHUNK), 1)
    causal = row >= col
    eye = row == col
    to_row = lambda c: jnp.sum(jnp.where(eye, c, 0.0), axis=0, keepdims=True)
    ones = jnp.ones((CHUNK, LANES), BF16)
    for h in range(ML_HEADS):
        sl = slice(h * LANES, (h + 1) * LANES)
        qh, kh, vh = q[:, sl], k[:, sl], v[:, sl]
        b_col = bcum[:, ML_HEADS + h:ML_HEADS + h + 1]
        i_col = gates[:, h:h + 1]
        m_prev = run_ref[h][0:1, 0:1]
        n_prev = n_ref[h][0:1, :]
        mat = m_ref[h]
        lw = b_col - to_row(b_col) + to_row(i_col)
        lprev = b_col + m_prev
        m_t = jnp.maximum(lprev, jnp.max(jnp.where(causal, lw, NEG_BIG), axis=-1, keepdims=True))
        wts = jnp.where(causal, jnp.exp(jnp.where(causal, lw - m_t, 0.0)), 0.0)
        s = _dot_nt(qh, kh) * wts
        wp = jnp.exp(lprev - m_t)
        yield
        num = _dot(s, vh) + wp * _dot(qh, mat)
        den = _dot(s, ones)[:, 0:1] + wp * _dot_nt(qh, n_ref[h])[:, 0:1]
        hh = num / jnp.maximum(jnp.abs(den), jnp.exp(-m_t))
        yield
        m_new = m_t[CHUNK - 1:CHUNK]
        b_end = b_col[CHUNK - 1:CHUNK]
        wl = jnp.exp(b_end - b_col + i_col - m_new)
        dec = jnp.exp(b_end + m_prev - m_new)
        kw = kh * wl
        m_ref[h] = dec * mat + _dot_tn(kw, vh)
        n_ref[h] = jnp.broadcast_to(dec * n_prev + jnp.sum(kw, axis=0, keepdims=True), (8, LANES))
        run_ref[h] = jnp.broadcast_to(m_new, (8, LANES))
        hg = _sigmoid(o_raw[:, sl]) * hh
        hn = hg * lax.rsqrt(jnp.mean(hg * hg, axis=-1, keepdims=True) + NORM_EPS) * p["norm"][:, sl]
        y_ref[:, sl] = (hn + p["skip"][:, sl] * xc[:, sl]).astype(y_ref.dtype)
        yield


def _rglru_chunk(u_ref, p, prev_ref, h_ref, y_ref):
    gw = GROUP_W
    xb = u_ref[:, U_LRU:U_LRU + gw]
    gb = u_ref[:, U_LRU + gw:U_LRU + 2 * gw]
    xc = _causal_conv4(xb, prev_ref, p["conv_w"], p["conv_b"])
    r = _sigmoid(_dot(xc, p["wa"][...]) + p["ba"][...])
    i = _sigmoid(_dot(xc, p["wx"][...]) + p["bx"][...])
    yield
    log_a = -LRU_C * r * _softplus(-p["lam"][...])
    a = jnp.exp(log_a)
    bt = jnp.sqrt(1.0 - jnp.exp(2.0 * log_a)) * (i * xc)
    rows = lax.broadcasted_iota(jnp.int32, a.shape, 0)
    d = 1
    while d < CHUNK:
        keep = rows >= d
        bt = a * jnp.where(keep, pltpu.roll(bt, d, 0), 0.0) + bt
        a = a * jnp.where(keep, pltpu.roll(a, d, 0), 1.0)
        d *= 2
        yield
    hs = bt + a * h_ref[0:1, :]
    h_ref[...] = jnp.broadcast_to(hs[CHUNK - 1:CHUNK], h_ref.shape)
    gelu = 0.5 * gb * (1.0 + jnp.tanh(0.7978845608028654 * (gb + 0.044715 * (gb * gb * gb))))
    yl = hs * gelu
    y_ref[...] = (yl * lax.rsqrt(jnp.mean(yl * yl, axis=-1, keepdims=True) + NORM_EPS)
                  * p["norm"][...]).astype(y_ref.dtype)


def _round_robin(stages):
    live = [[g, 0, n] for g, n in stages]
    while live:
        entry = min(live, key=lambda e: (e[1] + 1) / e[2])
        try:
            next(entry[0])
            entry[1] += 1
        except StopIteration:
            live.remove(entry)


_HG_KEYS = ("lb", "gnorm")
_RW_KEYS = ("mu", "w0", "w2", "a0", "a2", "g2", "kk", "ka", "rk", "ln_w", "ln_b", "seg1", "segm")
_ML_KEYS = ("conv_w", "conv_b", "wq", "wk", "gate_b", "norm", "skip")
_LRU_KEYS = ("conv_w", "conv_b", "wa", "ba", "wx", "bx", "lam", "norm")


def _proj_stages(x_ref, t0, u_dst, gain, shift_ref, scale_ref, w_ref):
    rows = u_dst.shape[0]
    h = jnp.concatenate(
        [_modulated_norm(x_ref[r, t0:t0 + CHUNK, :], gain, shift_ref[r], scale_ref[r]).astype(BF16)
         for r in range(rows)], axis=0)
    yield
    for j in range(U_TOT // PROJ_TILE):
        cols = slice(j * PROJ_TILE, (j + 1) * PROJ_TILE)
        t = jnp.dot(h, w_ref[:, cols], preferred_element_type=F32)
        for r in range(rows):
            u_dst[r, :, cols] = t[r * CHUNK:(r + 1) * CHUNK]
        yield


def _mixer_kernel(*refs):
    it = iter(refs)
    x2_ref, xn_ref, gain_ref, shift_ref, scale_ref, w_ref = (next(it) for _ in range(6))
    hg = {k: next(it) for k in _HG_KEYS}
    rw = {k: next(it) for k in _RW_KEYS}
    ml = {k: next(it) for k in _ML_KEYS}
    lru = {k: next(it) for k in _LRU_KEYS}
    y_ref = next(it)
    (u_a, u_b, hg_st, rw_prev, rw_ht, rw_o, ml_prev, ml_m, ml_n, ml_run, lru_prev, lru_h) = it
    proj = functools.partial(_proj_stages, gain=gain_ref[...], shift_ref=shift_ref,
                             scale_ref=scale_ref, w_ref=w_ref)

    @pl.when(pl.program_id(1) == 0)
    def _():
        for ref in (hg_st, rw_prev, rw_ht, ml_prev, ml_m, ml_n, ml_run, lru_prev, lru_h):
            ref[...] = jnp.zeros_like(ref)
        for _ in proj(x2_ref, 0, u_a):
            pass

    r_i = lax.broadcasted_iota(jnp.int32, (CHUNK, CHUNK), 0)
    c_i = lax.broadcasted_iota(jnp.int32, (CHUNK, CHUNK), 1)
    tri = (r_i >= c_i).astype(BF16)
    gw = GROUP_W

    def mixers(u_buf, t0):
        stages = []
        for r in range(u_buf.shape[0]):
            u_r = u_buf.at[r]
            y_g = lambda g: y_ref.at[r, t0:t0 + CHUNK, g * gw:(g + 1) * gw]
            stages += [
                (_hgrn2_chunk(u_r, hg["lb"][...], hg["gnorm"][...], tri, hg_st.at[r], y_g(0)),
                 2 + HG_HEADS * (CHUNK // SUB + 1)),
                (_rwkv7_chunk(u_r, rw, tri, rw_prev.at[r], rw_ht.at[r], rw_o.at[r], y_g(1)),
                 12 + RW_HEADS // 2),
                (_mlstm_chunk(u_r, ml, tri, ml_prev.at[r], ml_m.at[r], ml_n.at[r], ml_run.at[r], y_g(2)),
                 1 + 3 * ML_HEADS),
                (_rglru_chunk(u_r, lru, lru_prev.at[r], lru_h.at[r], y_g(3)), 7)]
        return stages

    n_proj = 1 + U_TOT // PROJ_TILE
    _round_robin([(proj(x2_ref, CHUNK, u_b), n_proj)] + mixers(u_a, 0))
    _round_robin([(proj(xn_ref, 0, u_a), n_proj)] + mixers(u_b, CHUNK))


def _block_diag(blocks):
    n, a, b = blocks.shape
    tiled = jnp.tile(blocks.reshape(n * a, b), (1, n))
    same = (jnp.arange(n * a)[:, None] // a) == (jnp.arange(n * b)[None, :] // b)
    return jnp.where(same, tiled, jnp.zeros_like(tiled))


def _mixer_params(l, lower_bounds, hg_norm, rw_mu, rw_w0, rw_w2, rw_a0, rw_a2, rw_g2, rw_kk, rw_ka,
                  rw_rk, rw_ln_w, rw_ln_b, ml_conv_w, ml_conv_b, ml_wq, ml_wk, ml_i_b, ml_f_b,
                  ml_norm, ml_skip, lru_conv_w, lru_conv_b, lru_wa, lru_ba, lru_wx, lru_bx,
                  lru_lambda, lru_norm):
    gw = GROUP_W
    row = lambda a: a.reshape(1, -1).astype(F32)

    def lora_rows(w, start):
        z = jnp.zeros((LORA_PAD, gw), F32)
        return lax.dynamic_update_slice(z, w.astype(F32), (start, 0)).astype(BF16)

    head_id = jnp.arange(gw) // RW_HEAD
    seg1 = (head_id[:, None] == head_id[None, :]).astype(F32)
    hg = dict(lb=row(lower_bounds[l]), gnorm=row(hg_norm[l]))
    rw = dict(mu=row(jnp.pad(rw_mu[l], (0, RW_PAD - RW_IN))), w0=row(rw_w0[l]),
              w2=lora_rows(rw_w2[l], 0), a0=row(rw_a0[l]),
              a2=lora_rows(rw_a2[l], RW_W_LORA), g2=lora_rows(rw_g2[l], RW_W_LORA + RW_A_LORA),
              kk=row(rw_kk[l]), ka=row(rw_ka[l]), rk=row(rw_rk[l]), ln_w=row(rw_ln_w[l]),
              ln_b=row(rw_ln_b[l]), seg1=seg1.astype(BF16), segm=(seg1 / RW_HEAD).astype(BF16))
    gate_b = jnp.concatenate([ml_i_b[l], ml_f_b[l], jnp.zeros((LANES - 2 * ML_HEADS,), F32)])
    ml = dict(conv_w=ml_conv_w[l].astype(F32), conv_b=row(ml_conv_b[l]),
              wq=_block_diag(ml_wq[l]).astype(BF16), wk=_block_diag(ml_wk[l]).astype(BF16),
              gate_b=row(gate_b), norm=row(ml_norm[l]), skip=row(ml_skip[l]))
    lru = dict(conv_w=lru_conv_w[l].astype(F32), conv_b=row(lru_conv_b[l]),
               wa=_block_diag(lru_wa[l]).astype(BF16), ba=row(lru_ba[l]),
               wx=_block_diag(lru_wx[l]).astype(BF16), bx=row(lru_bx[l]),
               lam=row(lru_lambda[l]), norm=row(lru_norm[l]))
    return ([hg[k] for k in _HG_KEYS] + [rw[k] for k in _RW_KEYS]
            + [ml[k] for k in _ML_KEYS] + [lru[k] for k in _LRU_KEYS])


def _mixer_call(x, gain, mod5, l, sub, w_in_p, params):
    bsz, seqlen, d = x.shape
    d_mix = 4 * GROUP_W
    rows = MIX_ROWS if bsz % MIX_ROWS == 0 else 1
    last_chunk = seqlen // CHUNK - 1
    once = pl.Buffered(1)
    full = lambda a: pl.BlockSpec(a.shape, lambda b, c: (0,) * a.ndim, pipeline_mode=once)
    vec = lambda k: pl.BlockSpec((None, rows, None, 1, d), lambda b, c: (l, b, 3 * sub + k, 0, 0))
    scratch = [
        pltpu.VMEM((rows, CHUNK, U_TOT), F32),
        pltpu.VMEM((rows, CHUNK, U_TOT), F32),
        pltpu.VMEM((rows, HG_HEADS, LANES, LANES), F32),
        pltpu.VMEM((rows, HIST + CHUNK, RW_PAD), F32),
        pltpu.VMEM((rows, RW_HEADS // 2, LANES, LANES), F32),
        pltpu.VMEM((rows, CHUNK, GROUP_W), F32),
        pltpu.VMEM((rows, HIST + CHUNK, GROUP_W), F32),
        pltpu.VMEM((rows, ML_HEADS, LANES, LANES), F32),
        pltpu.VMEM((rows, ML_HEADS, 8, LANES), F32),
        pltpu.VMEM((rows, ML_HEADS, 8, LANES), F32),
        pltpu.VMEM((rows, HIST + CHUNK, GROUP_W), F32),
        pltpu.VMEM((rows, 8, GROUP_W), F32),
    ]
    return pl.pallas_call(
        _mixer_kernel,
        grid=(bsz // rows, seqlen // (2 * CHUNK)),
        in_specs=[pl.BlockSpec((rows, 2 * CHUNK, d), lambda b, c: (b, c, 0)),
                  pl.BlockSpec((rows, CHUNK, d), lambda b, c: (b, jnp.minimum(2 * c + 2, last_chunk), 0)),
                  pl.BlockSpec((1, d), lambda b, c: (0, 0)),
                  vec(0), vec(1),
                  pl.BlockSpec((None, d, U_TOT), lambda b, c: (l, 0, 0), pipeline_mode=once)]
        + [full(a) for a in params],
        out_specs=pl.BlockSpec((rows, 2 * CHUNK, d_mix), lambda b, c: (b, c, 0)),
        out_shape=jax.ShapeDtypeStruct((bsz, seqlen, d_mix), BF16),
        scratch_shapes=scratch,
        compiler_params=pltpu.CompilerParams(
            dimension_semantics=("parallel", "arbitrary"), vmem_limit_bytes=MIXER_VMEM_LIMIT),
        name="mixers",
    )(x, x, gain.reshape(1, d), mod5, mod5, w_in_p, *params)


def _pad_w_in(w):
    z = lambda n: jnp.zeros(w.shape[:-1] + (n,), BF16)
    c1, c2 = HG_IN + RW_IN, HG_IN + RW_IN + ML_IN
    wb = w.astype(BF16)
    return jnp.concatenate(
        [wb[..., :c1], z(RW_PAD - RW_IN), wb[..., c1:c2], z(ML_PAD - ML_IN), wb[..., c2:],
         z(U_TOT - U_LRU - LRU_IN)], axis=-1)


def kernel(x, c, norm_gain, mod_w, mod_b, ffn_w1, ffn_w3, ffn_w2, w_in, w_out, hg_lb_logits, hg_norm, rw_mu, rw_w0, rw_w2, rw_a0, rw_a2, rw_g2, rw_kk, rw_ka, rw_rk, rw_ln_w, rw_ln_b, ml_conv_w, ml_conv_b, ml_wq, ml_wk, ml_i_b, ml_f_b, ml_norm, ml_skip, lru_conv_w, lru_conv_b, lru_wa, lru_ba, lru_wx, lru_bx, lru_lambda, lru_norm, final_norm):
    bsz = x.shape[0]
    depth = mod_w.shape[0]
    lb_w = jax.nn.softmax(hg_lb_logits.astype(F32), axis=0)
    lower_bounds = jnp.cumsum(lb_w, axis=0) - lb_w[0]
    mod = _mod_call(c, mod_w, mod_b)
    mod5 = mod.reshape(depth, bsz, 3 * N_SUB, 1, D_MODEL)
    w1, w3, w2 = ffn_w1.astype(BF16), ffn_w3.astype(BF16), ffn_w2.astype(BF16)
    w_in_p, w_out_b = _pad_w_in(w_in), w_out.astype(BF16)
    for l in range(depth):
        x = _ffn_call(x, norm_gain[l, 0], mod5, l, 0, 0, w1, w3, w2)
        params = _mixer_params(
            l, lower_bounds, hg_norm, rw_mu, rw_w0, rw_w2, rw_a0, rw_a2, rw_g2, rw_kk, rw_ka, rw_rk,
            rw_ln_w, rw_ln_b, ml_conv_w, ml_conv_b, ml_wq, ml_wk, ml_i_b, ml_f_b, ml_norm, ml_skip,
            lru_conv_w, lru_conv_b, lru_wa, lru_ba, lru_wx, lru_bx, lru_lambda, lru_norm)
        y = _mixer_call(x, norm_gain[l, 1], mod5, l, 1, w_in_p, params)
        x = _outproj_call(x, y, mod5, l, 1, w_out_b)
        last = l == depth - 1
        x = _ffn_call(x, norm_gain[l, 2], mod5, l, 2, 1, w1, w3, w2,
                      final_gain=final_norm if last else None)
    return x
```

```python
import functools

import jax
import jax.numpy as jnp
from jax import lax
from jax.experimental import pallas as pl
from jax.experimental.pallas import tpu as pltpu

F32 = jnp.float32
BF16 = jnp.bfloat16

D_MODEL = 2048
DEPTH = 2
GROUP_W = 512
N_SUB = 3
D_FF = 5632
NORM_EPS = 1e-6
HG_HEADS = 4
RW_HEAD = 64
RW_HEADS = 8
RW_W_LORA = 32
RW_A_LORA = 32
RW_G_LORA = 96
RW_LN_EPS = 64e-5
RW_IN = 3 * GROUP_W + RW_W_LORA + RW_A_LORA + RW_G_LORA
ML_HEADS = 4
ML_DIM = 128
ML_IN = 3 * GROUP_W + 2 * ML_HEADS
NEG_BIG = -1e30
LRU_C = 8.0
HG_IN = 4 * GROUP_W
LRU_IN = 2 * GROUP_W

LANES = 128
CHUNK = 64
MIX_ROWS = 2
SUB = 16
RW_PAD = 1792
ML_PAD = 1664
U_HG = 0
U_RW = U_HG + HG_IN
U_ML = U_RW + RW_PAD
U_LRU = U_ML + ML_PAD
U_TOT = 6656
LORA_PAD = RW_PAD - 3 * GROUP_W
VMEM_LIMIT = 56 * 1024 * 1024
MIXER_VMEM_LIMIT = 60 * 1024 * 1024
PROJ_TILE = 512
FFN_VMEM_LIMIT = 62 * 1024 * 1024
FFN_ROW_BLOCK = 128


def _sigmoid(x):
    return 0.5 * jnp.tanh(0.5 * x) + 0.5


def _silu(x):
    return x * _sigmoid(x)


def _softplus(x):
    return jnp.maximum(x, 0.0) + jnp.log(1.0 + jnp.exp(-jnp.abs(x)))


def _dot(a, b):
    return jnp.dot(a.astype(BF16), b.astype(BF16), preferred_element_type=F32)


def _dot_nt(a, b):
    return lax.dot_general(a.astype(BF16), b.astype(BF16), (((1,), (1,)), ((), ())),
                           preferred_element_type=F32)


def _dot_tn(a, b):
    return lax.dot_general(a.astype(BF16), b.astype(BF16), (((0,), (0,)), ((), ())),
                           preferred_element_type=F32)


def _split3(x):
    hi = x.astype(BF16)
    r1 = x - hi.astype(F32)
    mid = r1.astype(BF16)
    lo = (r1 - mid.astype(F32)).astype(BF16)
    return hi, mid, lo


def _cumsum_rows(x, tri):
    hi, mid, lo = _split3(x)
    d = functools.partial(jnp.dot, preferred_element_type=F32)
    return d(tri, hi) + d(tri, mid) + d(tri, lo)


def _seg_dot(x, seg):
    hi = x.astype(BF16)
    lo = (x - hi.astype(F32)).astype(BF16)
    d = functools.partial(jnp.dot, preferred_element_type=F32)
    return d(hi, seg) + d(lo, seg)


HIST = 8


def _push_history(hist_ref, x):
    hist_ref[0:HIST, :] = hist_ref[CHUNK:CHUNK + HIST, :]
    hist_ref[HIST:HIST + CHUNK, :] = x


def _delayed(hist_ref, d):
    return hist_ref[HIST - d:HIST - d + CHUNK, :]


def _modulated_norm(x, gain, shift, scale):
    y = x * lax.rsqrt(jnp.mean(x * x, axis=-1, keepdims=True) + NORM_EPS) * gain
    return y * (1.0 + scale) + shift


def _mod_kernel(c_ref, w_ref, b_ref, o_ref):
    o_ref[...] = _dot(_silu(c_ref[...]), w_ref[...]) + b_ref[...]


def _mod_call(c, mod_w, mod_b):
    depth, d, n = mod_w.shape
    bsz = c.shape[0]
    tn = 1024
    return pl.pallas_call(
        _mod_kernel,
        grid=(depth, n // tn),
        in_specs=[pl.BlockSpec((bsz, d), lambda l, j: (0, 0)),
                  pl.BlockSpec((None, d, tn), lambda l, j: (l, 0, j)),
                  pl.BlockSpec((None, 1, tn), lambda l, j: (l, 0, j))],
        out_specs=pl.BlockSpec((None, bsz, tn), lambda l, j: (l, 0, j)),
        out_shape=jax.ShapeDtypeStruct((depth, bsz, n), F32),
        compiler_params=pltpu.CompilerParams(
            dimension_semantics=("arbitrary", "arbitrary"), vmem_limit_bytes=VMEM_LIMIT),
        name="adaln_mod",
    )(c, mod_w, mod_b.reshape(depth, 1, n))


def _ffn_kernel(x_ref, gain_ref, shift_ref, scale_ref, gate_ref, w1_ref, w3_ref, w2_ref, *rest,
                n_ff, final):
    if final:
        fgain_ref, o_ref, h_ref = rest
    else:
        o_ref, h_ref = rest
    j = pl.program_id(2)

    def row_blocks(fn):
        def body(k, carry):
            fn(pl.ds(pl.multiple_of(k * FFN_ROW_BLOCK, FFN_ROW_BLOCK), FFN_ROW_BLOCK))
            return carry
        lax.fori_loop(0, x_ref.shape[0] // FFN_ROW_BLOCK, body, 0)

    @pl.when(j == 0)
    def _():
        def norm(rows):
            h = _modulated_norm(x_ref[rows, :], gain_ref[...], shift_ref[...], scale_ref[...])
            h_ref[rows, :] = h.astype(BF16)
        row_blocks(norm)

    h = h_ref[...]
    a = jnp.dot(h, w1_ref[...], preferred_element_type=F32)
    b = jnp.dot(h, w3_ref[...], preferred_element_type=F32)
    g = (_silu(a) * b).astype(BF16)
    acc = jnp.where(j == 0, 0.0, o_ref[...])
    o_ref[...] = acc + jnp.dot(g, w2_ref[...], preferred_element_type=F32)

    @pl.when(j == n_ff - 1)
    def _():
        def residual(rows):
            xn = x_ref[rows, :] + (0.5 * (1.0 + gate_ref[...])) * o_ref[rows, :]
            if final:
                xn = xn * lax.rsqrt(jnp.mean(xn * xn, axis=-1, keepdims=True) + NORM_EPS) * fgain_ref[...]
            o_ref[rows, :] = xn
        row_blocks(residual)


def _ffn_call(x, gain, mod5, l, sub, s, w1, w3, w2, final_gain=None, tm=1024, tf=512):
    bsz, seqlen, d = x.shape
    tm = min(tm, seqlen)
    n_ff = w1.shape[-1] // tf
    final = final_gain is not None
    vec = lambda k: pl.BlockSpec((None, None, None, 1, d), lambda b, i, j: (l, b, 3 * sub + k, 0, 0))
    in_specs = [pl.BlockSpec((None, tm, d), lambda b, i, j: (b, i, 0)),
                pl.BlockSpec((1, d), lambda b, i, j: (0, 0)),
                vec(0), vec(1), vec(2),
                pl.BlockSpec((None, None, d, tf), lambda b, i, j: (l, s, 0, j)),
                pl.BlockSpec((None, None, d, tf), lambda b, i, j: (l, s, 0, j)),
                pl.BlockSpec((None, None, tf, d), lambda b, i, j: (l, s, j, 0))]
    args = [x, gain.reshape(1, d), mod5, mod5, mod5, w1, w3, w2]
    if final:
        in_specs.append(pl.BlockSpec((1, d), lambda b, i, j: (0, 0)))
        args.append(final_gain.reshape(1, d))
    return pl.pallas_call(
        functools.partial(_ffn_kernel, n_ff=n_ff, final=final),
        grid=(bsz, seqlen // tm, n_ff),
        in_specs=in_specs,
        out_specs=pl.BlockSpec((None, tm, d), lambda b, i, j: (b, i, 0)),
        out_shape=jax.ShapeDtypeStruct(x.shape, F32),
        scratch_shapes=[pltpu.VMEM((tm, d), BF16)],
        compiler_params=pltpu.CompilerParams(
            dimension_semantics=("parallel", "parallel", "arbitrary"), vmem_limit_bytes=FFN_VMEM_LIMIT),
        name="ffn_final" if final else "ffn",
    )(*args)


def _outproj_kernel(x_ref, y_ref, gate_ref, w_ref, o_ref):
    o_ref[...] = x_ref[...] + (1.0 + gate_ref[...]) * jnp.dot(
        y_ref[...], w_ref[...], preferred_element_type=F32)


def _outproj_call(x, y, mod5, l, sub, w, tm=512):
    bsz, seqlen, d = x.shape
    tm = min(tm, seqlen)
    return pl.pallas_call(
        _outproj_kernel,
        grid=(bsz, seqlen // tm),
        in_specs=[pl.BlockSpec((None, tm, d), lambda b, i: (b, i, 0)),
                  pl.BlockSpec((None, tm, y.shape[-1]), lambda b, i: (b, i, 0)),
                  pl.BlockSpec((None, None, None, 1, d), lambda b, i: (l, b, 3 * sub + 2, 0, 0)),
                  pl.BlockSpec((None,) + w.shape[1:], lambda b, i: (l, 0, 0))],
        out_specs=pl.BlockSpec((None, tm, d), lambda b, i: (b, i, 0)),
        out_shape=jax.ShapeDtypeStruct(x.shape, F32),
        compiler_params=pltpu.CompilerParams(
            dimension_semantics=("parallel", "parallel"), vmem_limit_bytes=VMEM_LIMIT),
        name="out_proj",
    )(x, y, mod5, w)


def _hgrn2_chunk(u_ref, lb, gnorm, tri, st_ref, y_ref):
    gw = GROUP_W
    q = _silu(u_ref[:, U_HG:U_HG + gw])
    f_raw = u_ref[:, U_HG + gw:U_HG + 2 * gw]
    v = u_ref[:, U_HG + 2 * gw:U_HG + 3 * gw]
    g_raw = u_ref[:, U_HG + 3 * gw:U_HG + 4 * gw]
    e = jnp.exp(-f_raw)
    log_f = jnp.log(1.0 + lb * e) - jnp.log(1.0 + e)
    k = (1.0 - lb) * _sigmoid(-f_raw)
    yield
    bcum = _cumsum_rows(log_f, tri)
    yield
    sub_row = lax.broadcasted_iota(jnp.int32, (SUB, 1), 0)
    for h in range(HG_HEADS):
        sl = slice(h * LANES, (h + 1) * LANES)
        qh, kh, vh, bh = q[:, sl], k[:, sl], v[:, sl], bcum[:, sl]
        st = st_ref[h]
        o_inter = _dot_nt(qh * jnp.exp(bh), st)
        pieces = []
        for i in range(CHUNK // SUB):
            r0 = i * SUB
            qd, kd, vd, bd = qh[r0:r0 + SUB], kh[r0:r0 + SUB], vh[r0:r0 + SUB], bh[r0:r0 + SUB]
            oi = jnp.zeros((SUB, LANES), F32)
            if i > 0:
                bref = bh[r0 - 1:r0]
                qi = qd * jnp.exp(bd - bref)
                kp = kh[0:r0] * jnp.exp(bref - bh[0:r0])
                oi = _dot(_dot_nt(qi, kp), vh[0:r0])
            for j in range(SUB):
                w = jnp.exp(jnp.minimum(bd - bd[j:j + 1], 0.0))
                col = jnp.sum(qd * w * kd[j:j + 1], axis=-1, keepdims=True)
                col = jnp.where(sub_row >= j, col, 0.0)
                oi = oi + col * vd[j:j + 1]
            pieces.append(oi)
            yield
        o = o_inter + jnp.concatenate(pieces, axis=0)
        b_last = bh[CHUNK - 1:CHUNK]
        st_ref[h] = st * jnp.exp(b_last) + _dot_tn(vh, kh * jnp.exp(b_last - bh))
        on = o * lax.rsqrt(jnp.mean(o * o, axis=-1, keepdims=True) + NORM_EPS) * gnorm[:, sl]
        y_ref[:, sl] = (on * _silu(g_raw[:, sl])).astype(y_ref.dtype)
        yield


def _rwkv7_chunk(u_ref, p, tri, prev_ref, ht_ref, o_scr, y_ref):
    gw = GROUP_W
    u_raw = u_ref[:, U_RW:U_RW + RW_PAD]
    _push_history(prev_ref, u_raw)
    u = u_raw + (_delayed(prev_ref, 1) - u_raw) * p["mu"][...]
    yield
    r, k, v, lora = u[:, 0:gw], u[:, gw:2 * gw], u[:, 2 * gw:3 * gw], u[:, 3 * gw:RW_PAD]
    zw = p["w0"][...] + _dot(jnp.tanh(lora), p["w2"][...])
    lw = -jnp.exp(-_softplus(-zw) - 0.5)
    iclr = _sigmoid(p["a0"][...] + _dot(lora, p["a2"][...]))
    gate = _dot(_sigmoid(lora), p["g2"][...])
    yield
    kk = k * p["kk"][...]
    ss = _seg_dot(kk * kk, p["seg1"][...])
    kk = kk / jnp.maximum(jnp.sqrt(ss + 1e-12), 1e-6)
    k = k * (1.0 + (iclr - 1.0) * p["ka"][...])
    a_vec = -kk
    b_vec = kk * iclr
    yield

    lc = _cumsum_rows(lw, tri)
    l_last = lc[CHUNK - 1:CHUNK]
    g_in = jnp.exp(lc)
    g_out = jnp.exp(-lc)
    g_end = jnp.exp(l_last - lc)
    a_s = a_vec * jnp.exp(lc - lw)
    r_s = r * g_in
    b_s = b_vec * g_out
    k_s = k * g_out
    b_e = b_vec * g_end
    k_e = k * g_end
    decay_end = jnp.exp(l_last)
    yield

    lane = lax.broadcasted_iota(jnp.int32, (1, LANES), 1)
    m_a = (lane < RW_HEAD).astype(F32)
    m_b = 1.0 - m_a
    stack = lambda t: jnp.concatenate([t * m_a, t * m_b], axis=0)
    n2 = 2 * CHUNK
    row = lax.broadcasted_iota(jnp.int32, (n2, n2), 0)
    col = lax.broadcasted_iota(jnp.int32, (n2, n2), 1)
    same = (row // CHUNK) == (col // CHUNK)
    strict = same & (row > col)
    incl = same & (row >= col)
    pairs = range(RW_HEADS // 2)
    sls = [slice(pr * LANES, (pr + 1) * LANES) for pr in pairs]
    la = [stack(a_s[:, sl]) for sl in sls]
    lr = [stack(r_s[:, sl]) for sl in sls]
    v_st = [stack(v[:, sl]) for sl in sls]
    ht = [ht_ref[pr] for pr in pairs]
    sc = [_dot_nt(jnp.concatenate([la[pr], lr[pr]], axis=0),
                  jnp.concatenate([stack(b_s[:, sls[pr]]), stack(k_s[:, sls[pr]])], axis=0))
          for pr in pairs]
    yield
    pw = [jnp.where(strict, s[0:n2, 0:n2], 0.0) for s in sc]
    x = [_dot(jnp.where(strict, sc[pr][0:n2, n2:2 * n2], 0.0), v_st[pr]) + _dot_nt(la[pr], ht[pr])
         for pr in pairs]
    for step in range(6):
        x = [x[pr] + _dot(pw[pr], x[pr]) for pr in pairs]
        if step < 5:
            pw = [_dot(m, m) for m in pw]
        yield
    for pr in pairs:
        sl = sls[pr]
        a_rb = jnp.where(incl, sc[pr][n2:2 * n2, 0:n2], 0.0)
        a_rk = jnp.where(incl, sc[pr][n2:2 * n2, n2:2 * n2], 0.0)
        o_st = _dot_nt(lr[pr], ht[pr]) + _dot(a_rb, x[pr]) + _dot(a_rk, v_st[pr])
        o_scr[:, sl] = o_st[0:CHUNK] + o_st[CHUNK:n2]
        ht_ref[pr] = (ht[pr] * decay_end[:, sl] + _dot_tn(x[pr], stack(b_e[:, sl]))
                      + _dot_tn(v_st[pr], stack(k_e[:, sl])))
        yield

    o = o_scr[...]
    mean = _seg_dot(o, p["segm"][...])
    cen = o - mean
    var = _seg_dot(cen * cen, p["segm"][...])
    yield
    o = cen * lax.rsqrt(var + RW_LN_EPS) * p["ln_w"][...] + p["ln_b"][...]
    bonus = _seg_dot(r * k * p["rk"][...], p["seg1"][...]) * v
    y_ref[...] = ((o + bonus) * gate).astype(y_ref.dtype)


def _causal_conv4(x, hist_ref, w_ref, b_ref):
    _push_history(hist_ref, x)
    y = x * w_ref[3:4, :] + b_ref[...]
    for d in (1, 2, 3):
        y = y + _delayed(hist_ref, d) * w_ref[3 - d:4 - d, :]
    return y


def _mlstm_chunk(u_ref, p, tri, prev_ref, m_ref, n_ref, run_ref, y_ref):
    gw = GROUP_W
    xm = u_ref[:, U_ML:U_ML + gw]
    v = u_ref[:, U_ML + gw:U_ML + 2 * gw]
    o_raw = u_ref[:, U_ML + 2 * gw:U_ML + 3 * gw]
    gates = u_ref[:, U_ML + 3 * gw:U_ML + ML_PAD] + p["gate_b"][...]
    xc = _silu(_causal_conv4(xm, prev_ref, p["conv_w"], p["conv_b"]))
    q = _dot(xc, p["wq"][...]) * (ML_DIM ** -0.5)
    k = _dot(xc, p["wk"][...])
    yield
    log_f = jnp.minimum(gates, 0.0) - jnp.log(1.0 + jnp.exp(-jnp.abs(gates)))
    bcum = _cumsum_rows(log_f, tri)
    row = lax.broadcasted_iota(jnp.int32, (CHUNK, CHUNK), 0)
    col = lax.broadcasted_iota(jnp.int32, (CHUNK, CHUNK), 1)
    causal = row >= col
    eye = row == col
    to_row = lambda c: jnp.sum(jnp.where(eye, c, 0.0), axis=0, keepdims=True)
    ones = jnp.ones((CHUNK, LANES), BF16)
    for h in range(ML_HEADS):
        sl = slice(h * LANES, (h + 1) * LANES)
        qh, kh, vh = q[:, sl], k[:, sl], v[:, sl]
        b_col = bcum[:, ML_HEADS + h:ML_HEADS + h + 1]
        i_col = gates[:, h:h + 1]
        m_prev = run_ref[h][0:1, 0:1]
        n_prev = n_ref[h][0:1, :]
        mat = m_ref[h]
        lw = b_col - to_row(b_col) + to_row(i_col)
        lprev = b_col + m_prev
        m_t = jnp.maximum(lprev, jnp.max(jnp.where(causal, lw, NEG_BIG), axis=-1, keepdims=True))
        wts = jnp.where(causal, jnp.exp(jnp.where(causal, lw - m_t, 0.0)), 0.0)
        s = _dot_nt(qh, kh) * wts
        wp = jnp.exp(lprev - m_t)
        yield
        num = _dot(s, vh) + wp * _dot(qh, mat)
        den = _dot(s, ones)[:, 0:1] + wp * _dot_nt(qh, n_ref[h])[:, 0:1]
        hh = num / jnp.maximum(jnp.abs(den), jnp.exp(-m_t))
        yield
        m_new = m_t[CHUNK - 1:CHUNK]
        b_end = b_col[CHUNK - 1:CHUNK]
        wl = jnp.exp(b_end - b_col + i_col - m_new)
        dec = jnp.exp(b_end + m_prev - m_new)
        kw = kh * wl
        m_ref[h] = dec * mat + _dot_tn(kw, vh)
        n_ref[h] = jnp.broadcast_to(dec * n_prev + jnp.sum(kw, axis=0, keepdims=True), (8, LANES))
        run_ref[h] = jnp.broadcast_to(m_new, (8, LANES))
        hg = _sigmoid(o_raw[:, sl]) * hh
        hn = hg * lax.rsqrt(jnp.mean(hg * hg, axis=-1, keepdims=True) + NORM_EPS) * p["norm"][:, sl]
        y_ref[:, sl] = (hn + p["skip"][:, sl] * xc[:, sl]).astype(y_ref.dtype)
        yield


def _rglru_chunk(u_ref, p, prev_ref, h_ref, y_ref):
    gw = GROUP_W
    xb = u_ref[:, U_LRU:U_LRU + gw]
    gb = u_ref[:, U_LRU + gw:U_LRU + 2 * gw]
    xc = _causal_conv4(xb, prev_ref, p["conv_w"], p["conv_b"])
    r = _sigmoid(_dot(xc, p["wa"][...]) + p["ba"][...])
    i = _sigmoid(_dot(xc, p["wx"][...]) + p["bx"][...])
    yield
    log_a = -LRU_C * r * _softplus(-p["lam"][...])
    a = jnp.exp(log_a)
    bt = jnp.sqrt(1.0 - jnp.exp(2.0 * log_a)) * (i * xc)
    rows = lax.broadcasted_iota(jnp.int32, a.shape, 0)
    d = 1
    while d < CHUNK:
        keep = rows >= d
        bt = a * jnp.where(keep, pltpu.roll(bt, d, 0), 0.0) + bt
        a = a * jnp.where(keep, pltpu.roll(a, d, 0), 1.0)
        d *= 2
        yield
    hs = bt + a * h_ref[0:1, :]
    h_ref[...] = jnp.broadcast_to(hs[CHUNK - 1:CHUNK], h_ref.shape)
    gelu = 0.5 * gb * (1.0 + jnp.tanh(0.7978845608028654 * (gb + 0.044715 * (gb * gb * gb))))
    yl = hs * gelu
    y_ref[...] = (yl * lax.rsqrt(jnp.mean(yl * yl, axis=-1, keepdims=True) + NORM_EPS)
                  * p["norm"][...]).astype(y_ref.dtype)


def _round_robin(stages):
    live = [[g, 0, n] for g, n in stages]
    while live:
        entry = min(live, key=lambda e: (e[1] + 1) / e[2])
        try:
            next(entry[0])
            entry[1] += 1
        except StopIteration:
            live.remove(entry)


_HG_KEYS = ("lb", "gnorm")
_RW_KEYS = ("mu", "w0", "w2", "a0", "a2", "g2", "kk", "ka", "rk", "ln_w", "ln_b", "seg1", "segm")
_ML_KEYS = ("conv_w", "conv_b", "wq", "wk", "gate_b", "norm", "skip")
_LRU_KEYS = ("conv_w", "conv_b", "wa", "ba", "wx", "bx", "lam", "norm")


def _proj_stages(x_ref, t0, u_dst, gain, shift_ref, scale_ref, w_ref):
    rows = u_dst.shape[0]
    h = jnp.concatenate(
        [_modulated_norm(x_ref[r, t0:t0 + CHUNK, :], gain, shift_ref[r], scale_ref[r]).astype(BF16)
         for r in range(rows)], axis=0)
    yield
    for j in range(U_TOT // PROJ_TILE):
        cols = slice(j * PROJ_TILE, (j + 1) * PROJ_TILE)
        t = jnp.dot(h, w_ref[:, cols], preferred_element_type=F32)
        for r in range(rows):
            u_dst[r, :, cols] = t[r * CHUNK:(r + 1) * CHUNK]
        yield


def _mixer_kernel(*refs):
    it = iter(refs)
    x2_ref, xn_ref, gain_ref, shift_ref, scale_ref, w_ref = (next(it) for _ in range(6))
    hg = {k: next(it) for k in _HG_KEYS}
    rw = {k: next(it) for k in _RW_KEYS}
    ml = {k: next(it) for k in _ML_KEYS}
    lru = {k: next(it) for k in _LRU_KEYS}
    y_ref = next(it)
    (u_a, u_b, hg_st, rw_prev, rw_ht, rw_o, ml_prev, ml_m, ml_n, ml_run, lru_prev, lru_h) = it
    proj = functools.partial(_proj_stages, gain=gain_ref[...], shift_ref=shift_ref,
                             scale_ref=scale_ref, w_ref=w_ref)

    @pl.when(pl.program_id(1) == 0)
    def _():
        for ref in (hg_st, rw_prev, rw_ht, ml_prev, ml_m, ml_n, ml_run, lru_prev, lru_h):
            ref[...] = jnp.zeros_like(ref)
        for _ in proj(x2_ref, 0, u_a):
            pass

    r_i = lax.broadcasted_iota(jnp.int32, (CHUNK, CHUNK), 0)
    c_i = lax.broadcasted_iota(jnp.int32, (CHUNK, CHUNK), 1)
    tri = (r_i >= c_i).astype(BF16)
    gw = GROUP_W

    def mixers(u_buf, t0):
        stages = []
        for r in range(u_buf.shape[0]):
            u_r = u_buf.at[r]
            y_g = lambda g: y_ref.at[r, t0:t0 + CHUNK, g * gw:(g + 1) * gw]
            stages += [
                (_hgrn2_chunk(u_r, hg["lb"][...], hg["gnorm"][...], tri, hg_st.at[r], y_g(0)),
                 2 + HG_HEADS * (CHUNK // SUB + 1)),
                (_rwkv7_chunk(u_r, rw, tri, rw_prev.at[r], rw_ht.at[r], rw_o.at[r], y_g(1)),
                 12 + RW_HEADS // 2),
                (_mlstm_chunk(u_r, ml, tri, ml_prev.at[r], ml_m.at[r], ml_n.at[r], ml_run.at[r], y_g(2)),
                 1 + 3 * ML_HEADS),
                (_rglru_chunk(u_r, lru, lru_prev.at[r], lru_h.at[r], y_g(3)), 7)]
        return stages

    n_proj = 1 + U_TOT // PROJ_TILE
    _round_robin([(proj(x2_ref, CHUNK, u_b), n_proj)] + mixers(u_a, 0))
    _round_robin([(proj(xn_ref, 0, u_a), n_proj)] + mixers(u_b, CHUNK))


def _block_diag(blocks):
    n, a, b = blocks.shape
    tiled = jnp.tile(blocks.reshape(n * a, b), (1, n))
    same = (jnp.arange(n * a)[:, None] // a) == (jnp.arange(n * b)[None, :] // b)
    return jnp.where(same, tiled, jnp.zeros_like(tiled))


def _mixer_params(l, lower_bounds, hg_norm, rw_mu, rw_w0, rw_w2, rw_a0, rw_a2, rw_g2, rw_kk, rw_ka,
                  rw_rk, rw_ln_w, rw_ln_b, ml_conv_w, ml_conv_b, ml_wq, ml_wk, ml_i_b, ml_f_b,
                  ml_norm, ml_skip, lru_conv_w, lru_conv_b, lru_wa, lru_ba, lru_wx, lru_bx,
                  lru_lambda, lru_norm):
    gw = GROUP_W
    row = lambda a: a.reshape(1, -1).astype(F32)

    def lora_rows(w, start):
        z = jnp.zeros((LORA_PAD, gw), F32)
        return lax.dynamic_update_slice(z, w.astype(F32), (start, 0)).astype(BF16)

    head_id = jnp.arange(gw) // RW_HEAD
    seg1 = (head_id[:, None] == head_id[None, :]).astype(F32)
    hg = dict(lb=row(lower_bounds[l]), gnorm=row(hg_norm[l]))
    rw = dict(mu=row(jnp.pad(rw_mu[l], (0, RW_PAD - RW_IN))), w0=row(rw_w0[l]),
              w2=lora_rows(rw_w2[l], 0), a0=row(rw_a0[l]),
              a2=lora_rows(rw_a2[l], RW_W_LORA), g2=lora_rows(rw_g2[l], RW_W_LORA + RW_A_LORA),
              kk=row(rw_kk[l]), ka=row(rw_ka[l]), rk=row(rw_rk[l]), ln_w=row(rw_ln_w[l]),
              ln_b=row(rw_ln_b[l]), seg1=seg1.astype(BF16), segm=(seg1 / RW_HEAD).astype(BF16))
    gate_b = jnp.concatenate([ml_i_b[l], ml_f_b[l], jnp.zeros((LANES - 2 * ML_HEADS,), F32)])
    ml = dict(conv_w=ml_conv_w[l].astype(F32), conv_b=row(ml_conv_b[l]),
              wq=_block_diag(ml_wq[l]).astype(BF16), wk=_block_diag(ml_wk[l]).astype(BF16),
              gate_b=row(gate_b), norm=row(ml_norm[l]), skip=row(ml_skip[l]))
    lru = dict(conv_w=lru_conv_w[l].astype(F32), conv_b=row(lru_conv_b[l]),
               wa=_block_diag(lru_wa[l]).astype(BF16), ba=row(lru_ba[l]),
               wx=_block_diag(lru_wx[l]).astype(BF16), bx=row(lru_bx[l]),
               lam=row(lru_lambda[l]), norm=row(lru_norm[l]))
    return ([hg[k] for k in _HG_KEYS] + [rw[k] for k in _RW_KEYS]
            + [ml[k] for k in _ML_KEYS] + [lru[k] for k in _LRU_KEYS])


def _mixer_call(x, gain, mod5, l, sub, w_in_p, params):
    bsz, seqlen, d = x.shape
    d_mix = 4 * GROUP_W
    rows = MIX_ROWS if bsz % MIX_ROWS == 0 else 1
    last_chunk = seqlen // CHUNK - 1
    once = pl.Buffered(1)
    full = lambda a: pl.BlockSpec(a.shape, lambda b, c: (0,) * a.ndim, pipeline_mode=once)
    vec = lambda k: pl.BlockSpec((None, rows, None, 1, d), lambda b, c: (l, b, 3 * sub + k, 0, 0))
    scratch = [
        pltpu.VMEM((rows, CHUNK, U_TOT), F32),
        pltpu.VMEM((rows, CHUNK, U_TOT), F32),
        pltpu.VMEM((rows, HG_HEADS, LANES, LANES), F32),
        pltpu.VMEM((rows, HIST + CHUNK, RW_PAD), F32),
        pltpu.VMEM((rows, RW_HEADS // 2, LANES, LANES), F32),
        pltpu.VMEM((rows, CHUNK, GROUP_W), F32),
        pltpu.VMEM((rows, HIST + CHUNK, GROUP_W), F32),
        pltpu.VMEM((rows, ML_HEADS, LANES, LANES), F32),
        pltpu.VMEM((rows, ML_HEADS, 8, LANES), F32),
        pltpu.VMEM((rows, ML_HEADS, 8, LANES), F32),
        pltpu.VMEM((rows, HIST + CHUNK, GROUP_W), F32),
        pltpu.VMEM((rows, 8, GROUP_W), F32),
    ]
    return pl.pallas_call(
        _mixer_kernel,
        grid=(bsz // rows, seqlen // (2 * CHUNK)),
        in_specs=[pl.BlockSpec((rows, 2 * CHUNK, d), lambda b, c: (b, c, 0)),
                  pl.BlockSpec((rows, CHUNK, d), lambda b, c: (b, jnp.minimum(2 * c + 2, last_chunk), 0)),
                  pl.BlockSpec((1, d), lambda b, c: (0, 0)),
                  vec(0), vec(1),
                  pl.BlockSpec((None, d, U_TOT), lambda b, c: (l, 0, 0), pipeline_mode=once)]
        + [full(a) for a in params],
        out_specs=pl.BlockSpec((rows, 2 * CHUNK, d_mix), lambda b, c: (b, c, 0)),
        out_shape=jax.ShapeDtypeStruct((bsz, seqlen, d_mix), BF16),
        scratch_shapes=scratch,
        compiler_params=pltpu.CompilerParams(
            dimension_semantics=("parallel", "arbitrary"), vmem_limit_bytes=MIXER_VMEM_LIMIT),
        name="mixers",
    )(x, x, gain.reshape(1, d), mod5, mod5, w_in_p, *params)


def _pad_w_in(w):
    z = lambda n: jnp.zeros(w.shape[:-1] + (n,), BF16)
    c1, c2 = HG_IN + RW_IN, HG_IN + RW_IN + ML_IN
    wb = w.astype(BF16)
    return jnp.concatenate(
        [wb[..., :c1], z(RW_PAD - RW_IN), wb[..., c1:c2], z(ML_PAD - ML_IN), wb[..., c2:],
         z(U_TOT - U_LRU - LRU_IN)], axis=-1)


def kernel(x, c, norm_gain, mod_w, mod_b, ffn_w1, ffn_w3, ffn_w2, w_in, w_out, hg_lb_logits, hg_norm, rw_mu, rw_w0, rw_w2, rw_a0, rw_a2, rw_g2, rw_kk, rw_ka, rw_rk, rw_ln_w, rw_ln_b, ml_conv_w, ml_conv_b, ml_wq, ml_wk, ml_i_b, ml_f_b, ml_norm, ml_skip, lru_conv_w, lru_conv_b, lru_wa, lru_ba, lru_wx, lru_bx, lru_lambda, lru_norm, final_norm):
    bsz = x.shape[0]
    depth = mod_w.shape[0]
    lb_w = jax.nn.softmax(hg_lb_logits.astype(F32), axis=0)
    lower_bounds = jnp.cumsum(lb_w, axis=0) - lb_w[0]
    mod = _mod_call(c, mod_w, mod_b)
    mod5 = mod.reshape(depth, bsz, 3 * N_SUB, 1, D_MODEL)
    w1, w3, w2 = ffn_w1.astype(BF16), ffn_w3.astype(BF16), ffn_w2.astype(BF16)
    w_in_p, w_out_b = _pad_w_in(w_in), w_out.astype(BF16)
    for l in range(depth):
        x = _ffn_call(x, norm_gain[l, 0], mod5, l, 0, 0, w1, w3, w2)
        params = _mixer_params(
            l, lower_bounds, hg_norm, rw_mu, rw_w0, rw_w2, rw_a0, rw_a2, rw_g2, rw_kk, rw_ka, rw_rk,
            rw_ln_w, rw_ln_b, ml_conv_w, ml_conv_b, ml_wq, ml_wk, ml_i_b, ml_f_b, ml_norm, ml_skip,
            lru_conv_w, lru_conv_b, lru_wa, lru_ba, lru_wx, lru_bx, lru_lambda, lru_norm)
        y = _mixer_call(x, norm_gain[l, 1], mod5, l, 1, w_in_p, params)
        x = _outproj_call(x, y, mod5, l, 1, w_out_b)
        last = l == depth - 1
        x = _ffn_call(x, norm_gain[l, 2], mod5, l, 2, 1, w1, w3, w2,
                      final_gain=final_norm if last else None)
    return x
```

```python
import functools

import jax
import jax.numpy as jnp
from jax import lax
from jax.experimental import pallas as pl
from jax.experimental.pallas import tpu as pltpu

F32 = jnp.float32
BF16 = jnp.bfloat16

D_MODEL = 2048
DEPTH = 2
GROUP_W = 512
N_SUB = 3
D_FF = 5632
NORM_EPS = 1e-6
HG_HEADS = 4
RW_HEAD = 64
RW_HEADS = 8
RW_W_LORA = 32
RW_A_LORA = 32
RW_G_LORA = 96
RW_LN_EPS = 64e-5
RW_IN = 3 * GROUP_W + RW_W_LORA + RW_A_LORA + RW_G_LORA
ML_HEADS = 4
ML_DIM = 128
ML_IN = 3 * GROUP_W + 2 * ML_HEADS
NEG_BIG = -1e30
LRU_C = 8.0
HG_IN = 4 * GROUP_W
LRU_IN = 2 * GROUP_W

LANES = 128
MXU_TILE = 256
CHUNK = 64
MIX_ROWS = 2
SUB = 16
RW_PAD = 1792
ML_PAD = 1664
U_HG = 0
U_RW = U_HG + HG_IN
U_ML = U_RW + RW_PAD
U_LRU = U_ML + ML_PAD
U_TOT = 6656
LORA_PAD = RW_PAD - 3 * GROUP_W
VMEM_LIMIT = 56 * 1024 * 1024
MIXER_VMEM_LIMIT = 60 * 1024 * 1024
PROJ_TILE = 512
FFN_VMEM_LIMIT = 62 * 1024 * 1024
FFN_ROW_BLOCK = 128


def _sigmoid(x):
    return 0.5 * jnp.tanh(0.5 * x) + 0.5


def _silu(x):
    return x * _sigmoid(x)


def _softplus(x):
    return jnp.maximum(x, 0.0) + jnp.log(1.0 + jnp.exp(-jnp.abs(x)))


def _dot(a, b):
    return jnp.dot(a.astype(BF16), b.astype(BF16), preferred_element_type=F32)


def _dot_nt(a, b):
    return lax.dot_general(a.astype(BF16), b.astype(BF16), (((1,), (1,)), ((), ())),
                           preferred_element_type=F32)


def _dot_tn(a, b):
    return lax.dot_general(a.astype(BF16), b.astype(BF16), (((0,), (0,)), ((), ())),
                           preferred_element_type=F32)


def _split3(x):
    hi = x.astype(BF16)
    r1 = x - hi.astype(F32)
    mid = r1.astype(BF16)
    lo = (r1 - mid.astype(F32)).astype(BF16)
    return hi, mid, lo


def _cumsum_rows(x, tri):
    hi, mid, lo = _split3(x)
    d = functools.partial(jnp.dot, preferred_element_type=F32)
    return d(tri, hi) + d(tri, mid) + d(tri, lo)


def _dot_halves(x, w_lo, w_hi):
    d = functools.partial(jnp.dot, preferred_element_type=F32)
    return jnp.concatenate([d(x[:, :MXU_TILE], w_lo), d(x[:, MXU_TILE:], w_hi)], axis=1)


def _seg_dot(x, seg):
    hi = x.astype(BF16)
    lo = (x - hi.astype(F32)).astype(BF16)
    return _dot_halves(hi, seg, seg) + _dot_halves(lo, seg, seg)


def _dot_bd(x, w_ref):
    return _dot_halves(x.astype(BF16), w_ref[0], w_ref[1])


HIST = 8


def _push_history(hist_ref, x):
    hist_ref[0:HIST, :] = hist_ref[CHUNK:CHUNK + HIST, :]
    hist_ref[HIST:HIST + CHUNK, :] = x


def _delayed(hist_ref, d):
    return hist_ref[HIST - d:HIST - d + CHUNK, :]


def _modulated_norm(x, gain, shift, scale):
    y = x * lax.rsqrt(jnp.mean(x * x, axis=-1, keepdims=True) + NORM_EPS) * gain
    return y * (1.0 + scale) + shift


def _mod_kernel(c_ref, w_ref, b_ref, o_ref):
    o_ref[...] = _dot(_silu(c_ref[...]), w_ref[...]) + b_ref[...]


def _mod_call(c, mod_w, mod_b):
    depth, d, n = mod_w.shape
    bsz = c.shape[0]
    tn = 1024
    return pl.pallas_call(
        _mod_kernel,
        grid=(depth, n // tn),
        in_specs=[pl.BlockSpec((bsz, d), lambda l, j: (0, 0)),
                  pl.BlockSpec((None, d, tn), lambda l, j: (l, 0, j)),
                  pl.BlockSpec((None, 1, tn), lambda l, j: (l, 0, j))],
        out_specs=pl.BlockSpec((None, bsz, tn), lambda l, j: (l, 0, j)),
        out_shape=jax.ShapeDtypeStruct((depth, bsz, n), F32),
        compiler_params=pltpu.CompilerParams(
            dimension_semantics=("arbitrary", "arbitrary"), vmem_limit_bytes=VMEM_LIMIT),
        name="adaln_mod",
    )(c, mod_w, mod_b.reshape(depth, 1, n))


def _ffn_kernel(x_ref, gain_ref, shift_ref, scale_ref, gate_ref, w1_ref, w3_ref, w2_ref, *rest,
                n_ff, final):
    if final:
        fgain_ref, o_ref, h_ref = rest
    else:
        o_ref, h_ref = rest
    j = pl.program_id(2)

    def row_blocks(fn):
        def body(k, carry):
            fn(pl.ds(pl.multiple_of(k * FFN_ROW_BLOCK, FFN_ROW_BLOCK), FFN_ROW_BLOCK))
            return carry
        lax.fori_loop(0, x_ref.shape[0] // FFN_ROW_BLOCK, body, 0)

    @pl.when(j == 0)
    def _():
        def norm(rows):
            h = _modulated_norm(x_ref[rows, :], gain_ref[...], shift_ref[...], scale_ref[...])
            h_ref[rows, :] = h.astype(BF16)
        row_blocks(norm)

    h = h_ref[...]
    a = jnp.dot(h, w1_ref[...], preferred_element_type=F32)
    b = jnp.dot(h, w3_ref[...], preferred_element_type=F32)
    g = (_silu(a) * b).astype(BF16)
    acc = jnp.where(j == 0, 0.0, o_ref[...])
    o_ref[...] = acc + jnp.dot(g, w2_ref[...], preferred_element_type=F32)

    @pl.when(j == n_ff - 1)
    def _():
        def residual(rows):
            xn = x_ref[rows, :] + (0.5 * (1.0 + gate_ref[...])) * o_ref[rows, :]
            if final:
                xn = xn * lax.rsqrt(jnp.mean(xn * xn, axis=-1, keepdims=True) + NORM_EPS) * fgain_ref[...]
            o_ref[rows, :] = xn
        row_blocks(residual)


def _ffn_call(x, gain, mod5, l, sub, s, w1, w3, w2, final_gain=None, tm=1024, tf=512):
    bsz, seqlen, d = x.shape
    tm = min(tm, seqlen)
    n_ff = w1.shape[-1] // tf
    final = final_gain is not None
    vec = lambda k: pl.BlockSpec((None, None, None, 1, d), lambda b, i, j: (l, b, 3 * sub + k, 0, 0))
    in_specs = [pl.BlockSpec((None, tm, d), lambda b, i, j: (b, i, 0)),
                pl.BlockSpec((1, d), lambda b, i, j: (0, 0)),
                vec(0), vec(1), vec(2),
                pl.BlockSpec((None, None, d, tf), lambda b, i, j: (l, s, 0, j)),
                pl.BlockSpec((None, None, d, tf), lambda b, i, j: (l, s, 0, j)),
                pl.BlockSpec((None, None, tf, d), lambda b, i, j: (l, s, j, 0))]
    args = [x, gain.reshape(1, d), mod5, mod5, mod5, w1, w3, w2]
    if final:
        in_specs.append(pl.BlockSpec((1, d), lambda b, i, j: (0, 0)))
        args.append(final_gain.reshape(1, d))
    return pl.pallas_call(
        functools.partial(_ffn_kernel, n_ff=n_ff, final=final),
        grid=(bsz, seqlen // tm, n_ff),
        in_specs=in_specs,
        out_specs=pl.BlockSpec((None, tm, d), lambda b, i, j: (b, i, 0)),
        out_shape=jax.ShapeDtypeStruct(x.shape, F32),
        scratch_shapes=[pltpu.VMEM((tm, d), BF16)],
        compiler_params=pltpu.CompilerParams(
            dimension_semantics=("parallel", "parallel", "arbitrary"), vmem_limit_bytes=FFN_VMEM_LIMIT),
        name="ffn_final" if final else "ffn",
    )(*args)


def _outproj_kernel(x_ref, y_ref, gate_ref, w_ref, o_ref):
    o_ref[...] = x_ref[...] + (1.0 + gate_ref[...]) * jnp.dot(
        y_ref[...], w_ref[...], preferred_element_type=F32)


def _outproj_call(x, y, mod5, l, sub, w, tm=512):
    bsz, seqlen, d = x.shape
    tm = min(tm, seqlen)
    return pl.pallas_call(
        _outproj_kernel,
        grid=(bsz, seqlen // tm),
        in_specs=[pl.BlockSpec((None, tm, d), lambda b, i: (b, i, 0)),
                  pl.BlockSpec((None, tm, y.shape[-1]), lambda b, i: (b, i, 0)),
                  pl.BlockSpec((None, None, None, 1, d), lambda b, i: (l, b, 3 * sub + 2, 0, 0)),
                  pl.BlockSpec((None,) + w.shape[1:], lambda b, i: (l, 0, 0))],
        out_specs=pl.BlockSpec((None, tm, d), lambda b, i: (b, i, 0)),
        out_shape=jax.ShapeDtypeStruct(x.shape, F32),
        compiler_params=pltpu.CompilerParams(
            dimension_semantics=("parallel", "parallel"), vmem_limit_bytes=VMEM_LIMIT),
        name="out_proj",
    )(x, y, mod5, w)


def _hgrn2_chunk(u_ref, lb, gnorm, tri, st_ref, y_ref):
    gw = GROUP_W
    q = _silu(u_ref[:, U_HG:U_HG + gw])
    f_raw = u_ref[:, U_HG + gw:U_HG + 2 * gw]
    v = u_ref[:, U_HG + 2 * gw:U_HG + 3 * gw]
    g_raw = u_ref[:, U_HG + 3 * gw:U_HG + 4 * gw]
    e = jnp.exp(-f_raw)
    log_f = jnp.log(1.0 + lb * e) - jnp.log(1.0 + e)
    k = (1.0 - lb) * _sigmoid(-f_raw)
    yield
    bcum = _cumsum_rows(log_f, tri)
    yield
    sub_row = lax.broadcasted_iota(jnp.int32, (SUB, 1), 0)
    for h in range(HG_HEADS):
        sl = slice(h * LANES, (h + 1) * LANES)
        qh, kh, vh, bh = q[:, sl], k[:, sl], v[:, sl], bcum[:, sl]
        st = st_ref[h]
        o_inter = _dot_nt(qh * jnp.exp(bh), st)
        pieces = []
        for i in range(CHUNK // SUB):
            r0 = i * SUB
            qd, kd, vd, bd = qh[r0:r0 + SUB], kh[r0:r0 + SUB], vh[r0:r0 + SUB], bh[r0:r0 + SUB]
            oi = jnp.zeros((SUB, LANES), F32)
            if i > 0:
                bref = bh[r0 - 1:r0]
                qi = qd * jnp.exp(bd - bref)
                kp = kh[0:r0] * jnp.exp(bref - bh[0:r0])
                oi = _dot(_dot_nt(qi, kp), vh[0:r0])
            for j in range(SUB):
                w = jnp.exp(jnp.minimum(bd - bd[j:j + 1], 0.0))
                col = jnp.sum(qd * w * kd[j:j + 1], axis=-1, keepdims=True)
                col = jnp.where(sub_row >= j, col, 0.0)
                oi = oi + col * vd[j:j + 1]
            pieces.append(oi)
            yield
        o = o_inter + jnp.concatenate(pieces, axis=0)
        b_last = bh[CHUNK - 1:CHUNK]
        st_ref[h] = st * jnp.exp(b_last) + _dot_tn(vh, kh * jnp.exp(b_last - bh))
        on = o * lax.rsqrt(jnp.mean(o * o, axis=-1, keepdims=True) + NORM_EPS) * gnorm[:, sl]
        y_ref[:, sl] = (on * _silu(g_raw[:, sl])).astype(y_ref.dtype)
        yield


def _rwkv7_chunk(u_ref, p, tri, prev_ref, ht_ref, o_scr, y_ref):
    gw = GROUP_W
    u_raw = u_ref[:, U_RW:U_RW + RW_PAD]
    _push_history(prev_ref, u_raw)
    u = u_raw + (_delayed(prev_ref, 1) - u_raw) * p["mu"][...]
    yield
    r, k, v, lora = u[:, 0:gw], u[:, gw:2 * gw], u[:, 2 * gw:3 * gw], u[:, 3 * gw:RW_PAD]
    zw = p["w0"][...] + _dot(jnp.tanh(lora), p["w2"][...])
    lw = -jnp.exp(-_softplus(-zw) - 0.5)
    iclr = _sigmoid(p["a0"][...] + _dot(lora, p["a2"][...]))
    gate = _dot(_sigmoid(lora), p["g2"][...])
    yield
    kk = k * p["kk"][...]
    ss = _seg_dot(kk * kk, p["seg1"][...])
    kk = kk / jnp.maximum(jnp.sqrt(ss + 1e-12), 1e-6)
    k = k * (1.0 + (iclr - 1.0) * p["ka"][...])
    a_vec = -kk
    b_vec = kk * iclr
    yield

    lc = _cumsum_rows(lw, tri)
    l_last = lc[CHUNK - 1:CHUNK]
    g_in = jnp.exp(lc)
    g_out = jnp.exp(-lc)
    g_end = jnp.exp(l_last - lc)
    a_s = a_vec * jnp.exp(lc - lw)
    r_s = r * g_in
    b_s = b_vec * g_out
    k_s = k * g_out
    b_e = b_vec * g_end
    k_e = k * g_end
    decay_end = jnp.exp(l_last)
    yield

    lane = lax.broadcasted_iota(jnp.int32, (1, LANES), 1)
    m_a = (lane < RW_HEAD).astype(F32)
    m_b = 1.0 - m_a
    stack = lambda t: jnp.concatenate([t * m_a, t * m_b], axis=0)
    n2 = 2 * CHUNK
    row = lax.broadcasted_iota(jnp.int32, (n2, n2), 0)
    col = lax.broadcasted_iota(jnp.int32, (n2, n2), 1)
    same = (row // CHUNK) == (col // CHUNK)
    strict = same & (row > col)
    incl = same & (row >= col)
    pairs = range(RW_HEADS // 2)
    sls = [slice(pr * LANES, (pr + 1) * LANES) for pr in pairs]
    la = [stack(a_s[:, sl]) for sl in sls]
    lr = [stack(r_s[:, sl]) for sl in sls]
    v_st = [stack(v[:, sl]) for sl in sls]
    ht = [ht_ref[pr] for pr in pairs]
    sc = [_dot_nt(jnp.concatenate([la[pr], lr[pr]], axis=0),
                  jnp.concatenate([stack(b_s[:, sls[pr]]), stack(k_s[:, sls[pr]])], axis=0))
          for pr in pairs]
    yield
    pw = [jnp.where(strict, s[0:n2, 0:n2], 0.0) for s in sc]
    x = [_dot(jnp.where(strict, sc[pr][0:n2, n2:2 * n2], 0.0), v_st[pr]) + _dot_nt(la[pr], ht[pr])
         for pr in pairs]
    for step in range(6):
        x = [x[pr] + _dot(pw[pr], x[pr]) for pr in pairs]
        if step < 5:
            pw = [_dot(m, m) for m in pw]
        yield
    for pr in pairs:
        sl = sls[pr]
        a_rb = jnp.where(incl, sc[pr][n2:2 * n2, 0:n2], 0.0)
        a_rk = jnp.where(incl, sc[pr][n2:2 * n2, n2:2 * n2], 0.0)
        o_st = _dot_nt(lr[pr], ht[pr]) + _dot(a_rb, x[pr]) + _dot(a_rk, v_st[pr])
        o_scr[:, sl] = o_st[0:CHUNK] + o_st[CHUNK:n2]
        ht_ref[pr] = (ht[pr] * decay_end[:, sl] + _dot_tn(x[pr], stack(b_e[:, sl]))
                      + _dot_tn(v_st[pr], stack(k_e[:, sl])))
        yield

    o = o_scr[...]
    mean = _seg_dot(o, p["segm"][...])
    cen = o - mean
    var = _seg_dot(cen * cen, p["segm"][...])
    yield
    o = cen * lax.rsqrt(var + RW_LN_EPS) * p["ln_w"][...] + p["ln_b"][...]
    bonus = _seg_dot(r * k * p["rk"][...], p["seg1"][...]) * v
    y_ref[...] = ((o + bonus) * gate).astype(y_ref.dtype)


def _causal_conv4(x, hist_ref, w_ref, b_ref):
    _push_history(hist_ref, x)
    y = x * w_ref[3:4, :] + b_ref[...]
    for d in (1, 2, 3):
        y = y + _delayed(hist_ref, d) * w_ref[3 - d:4 - d, :]
    return y


def _mlstm_chunk(u_ref, p, tri, prev_ref, m_ref, n_ref, run_ref, y_ref):
    gw = GROUP_W
    xm = u_ref[:, U_ML:U_ML + gw]
    v = u_ref[:, U_ML + gw:U_ML + 2 * gw]
    o_raw = u_ref[:, U_ML + 2 * gw:U_ML + 3 * gw]
    gates = u_ref[:, U_ML + 3 * gw:U_ML + ML_PAD] + p["gate_b"][...]
    xc = _silu(_causal_conv4(xm, prev_ref, p["conv_w"], p["conv_b"]))
    q = _dot_bd(xc, p["wq"]) * (ML_DIM ** -0.5)
    k = _dot_bd(xc, p["wk"])
    yield
    log_f = jnp.minimum(gates, 0.0) - jnp.log(1.0 + jnp.exp(-jnp.abs(gates)))
    bcum = _cumsum_rows(log_f, tri)
    row = lax.broadcasted_iota(jnp.int32, (CHUNK, CHUNK), 0)
    col = lax.broadcasted_iota(jnp.int32, (CHUNK, CHUNK), 1)
    causal = row >= col
    eye = row == col
    to_row = lambda c: jnp.sum(jnp.where(eye, c, 0.0), axis=0, keepdims=True)
    ones = jnp.ones((CHUNK, LANES), BF16)
    for h in range(ML_HEADS):
        sl = slice(h * LANES, (h + 1) * LANES)
        qh, kh, vh = q[:, sl], k[:, sl], v[:, sl]
        b_col = bcum[:, ML_HEADS + h:ML_HEADS + h + 1]
        i_col = gates[:, h:h + 1]
        m_prev = run_ref[h][0:1, 0:1]
        n_prev = n_ref[h][0:1, :]
        mat = m_ref[h]
        lw = b_col - to_row(b_col) + to_row(i_col)
        lprev = b_col + m_prev
        m_t = jnp.maximum(lprev, jnp.max(jnp.where(causal, lw, NEG_BIG), axis=-1, keepdims=True))
        wts = jnp.where(causal, jnp.exp(jnp.where(causal, lw - m_t, 0.0)), 0.0)
        s = _dot_nt(qh, kh) * wts
        wp = jnp.exp(lprev - m_t)
        yield
        sv = _dot(s, jnp.concatenate([vh.astype(BF16), ones], axis=1))
        num = sv[:, 0:LANES] + wp * _dot(qh, mat)
        den = sv[:, LANES:LANES + 1] + wp * _dot_nt(qh, n_ref[h])[:, 0:1]
        hh = num / jnp.maximum(jnp.abs(den), jnp.exp(-m_t))
        yield
        m_new = m_t[CHUNK - 1:CHUNK]
        b_end = b_col[CHUNK - 1:CHUNK]
        wl = jnp.exp(b_end - b_col + i_col - m_new)
        dec = jnp.exp(b_end + m_prev - m_new)
        kw = kh * wl
        m_ref[h] = dec * mat + _dot_tn(kw, vh)
        n_ref[h] = jnp.broadcast_to(dec * n_prev + jnp.sum(kw, axis=0, keepdims=True), (8, LANES))
        run_ref[h] = jnp.broadcast_to(m_new, (8, LANES))
        hg = _sigmoid(o_raw[:, sl]) * hh
        hn = hg * lax.rsqrt(jnp.mean(hg * hg, axis=-1, keepdims=True) + NORM_EPS) * p["norm"][:, sl]
        y_ref[:, sl] = (hn + p["skip"][:, sl] * xc[:, sl]).astype(y_ref.dtype)
        yield


def _rglru_chunk(u_ref, p, prev_ref, h_ref, y_ref):
    gw = GROUP_W
    xb = u_ref[:, U_LRU:U_LRU + gw]
    gb = u_ref[:, U_LRU + gw:U_LRU + 2 * gw]
    xc = _causal_conv4(xb, prev_ref, p["conv_w"], p["conv_b"])
    r = _sigmoid(_dot_bd(xc, p["wa"]) + p["ba"][...])
    i = _sigmoid(_dot_bd(xc, p["wx"]) + p["bx"][...])
    yield
    log_a = -LRU_C * r * _softplus(-p["lam"][...])
    a = jnp.exp(log_a)
    bt = jnp.sqrt(1.0 - jnp.exp(2.0 * log_a)) * (i * xc)
    rows = lax.broadcasted_iota(jnp.int32, a.shape, 0)
    d = 1
    while d < CHUNK:
        keep = rows >= d
        bt = a * jnp.where(keep, pltpu.roll(bt, d, 0), 0.0) + bt
        a = a * jnp.where(keep, pltpu.roll(a, d, 0), 1.0)
        d *= 2
        yield
    hs = bt + a * h_ref[0:1, :]
    h_ref[...] = jnp.broadcast_to(hs[CHUNK - 1:CHUNK], h_ref.shape)
    gelu = 0.5 * gb * (1.0 + jnp.tanh(0.7978845608028654 * (gb + 0.044715 * (gb * gb * gb))))
    yl = hs * gelu
    y_ref[...] = (yl * lax.rsqrt(jnp.mean(yl * yl, axis=-1, keepdims=True) + NORM_EPS)
                  * p["norm"][...]).astype(y_ref.dtype)


def _round_robin(stages):
    live = [[g, 0, n] for g, n in stages]
    while live:
        entry = min(live, key=lambda e: (e[1] + 1) / e[2])
        try:
            next(entry[0])
            entry[1] += 1
        except StopIteration:
            live.remove(entry)


_HG_KEYS = ("lb", "gnorm")
_RW_KEYS = ("mu", "w0", "w2", "a0", "a2", "g2", "kk", "ka", "rk", "ln_w", "ln_b", "seg1", "segm")
_ML_KEYS = ("conv_w", "conv_b", "wq", "wk", "gate_b", "norm", "skip")
_LRU_KEYS = ("conv_w", "conv_b", "wa", "ba", "wx", "bx", "lam", "norm")


def _proj_stages(x_ref, t0, u_dst, gain, shift_ref, scale_ref, w_ref):
    rows = u_dst.shape[0]
    h = jnp.concatenate(
        [_modulated_norm(x_ref[r, t0:t0 + CHUNK, :], gain, shift_ref[r], scale_ref[r]).astype(BF16)
         for r in range(rows)], axis=0)
    yield
    for j in range(U_TOT // PROJ_TILE):
        cols = slice(j * PROJ_TILE, (j + 1) * PROJ_TILE)
        t = jnp.dot(h, w_ref[:, cols], preferred_element_type=F32)
        for r in range(rows):
            u_dst[r, :, cols] = t[r * CHUNK:(r + 1) * CHUNK]
        yield


def _mixer_kernel(*refs):
    it = iter(refs)
    x2_ref, xn_ref, gain_ref, shift_ref, scale_ref, w_ref = (next(it) for _ in range(6))
    hg = {k: next(it) for k in _HG_KEYS}
    rw = {k: next(it) for k in _RW_KEYS}
    ml = {k: next(it) for k in _ML_KEYS}
    lru = {k: next(it) for k in _LRU_KEYS}
    y_ref = next(it)
    (u_a, u_b, hg_st, rw_prev, rw_ht, rw_o, ml_prev, ml_m, ml_n, ml_run, lru_prev, lru_h) = it
    proj = functools.partial(_proj_stages, gain=gain_ref[...], shift_ref=shift_ref,
                             scale_ref=scale_ref, w_ref=w_ref)

    @pl.when(pl.program_id(1) == 0)
    def _():
        for ref in (hg_st, rw_prev, rw_ht, ml_prev, ml_m, ml_n, ml_run, lru_prev, lru_h):
            ref[...] = jnp.zeros_like(ref)
        for _ in proj(x2_ref, 0, u_a):
            pass

    r_i = lax.broadcasted_iota(jnp.int32, (CHUNK, CHUNK), 0)
    c_i = lax.broadcasted_iota(jnp.int32, (CHUNK, CHUNK), 1)
    tri = (r_i >= c_i).astype(BF16)
    gw = GROUP_W

    def mixers(u_buf, t0):
        stages = []
        for r in range(u_buf.shape[0]):
            u_r = u_buf.at[r]
            y_g = lambda g: y_ref.at[r, t0:t0 + CHUNK, g * gw:(g + 1) * gw]
            stages += [
                (_hgrn2_chunk(u_r, hg["lb"][...], hg["gnorm"][...], tri, hg_st.at[r], y_g(0)),
                 2 + HG_HEADS * (CHUNK // SUB + 1)),
                (_rwkv7_chunk(u_r, rw, tri, rw_prev.at[r], rw_ht.at[r], rw_o.at[r], y_g(1)),
                 12 + RW_HEADS // 2),
                (_mlstm_chunk(u_r, ml, tri, ml_prev.at[r], ml_m.at[r], ml_n.at[r], ml_run.at[r], y_g(2)),
                 1 + 3 * ML_HEADS),
                (_rglru_chunk(u_r, lru, lru_prev.at[r], lru_h.at[r], y_g(3)), 7)]
        return stages

    n_proj = 1 + U_TOT // PROJ_TILE
    _round_robin([(proj(x2_ref, CHUNK, u_b), n_proj)] + mixers(u_a, 0))
    _round_robin([(proj(xn_ref, 0, u_a), n_proj)] + mixers(u_b, CHUNK))


def _block_diag(blocks):
    n, a, b = blocks.shape
    tiled = jnp.tile(blocks.reshape(n * a, b), (1, n))
    same = (jnp.arange(n * a)[:, None] // a) == (jnp.arange(n * b)[None, :] // b)
    return jnp.where(same, tiled, jnp.zeros_like(tiled))


def _mixer_params(l, lower_bounds, hg_norm, rw_mu, rw_w0, rw_w2, rw_a0, rw_a2, rw_g2, rw_kk, rw_ka,
                  rw_rk, rw_ln_w, rw_ln_b, ml_conv_w, ml_conv_b, ml_wq, ml_wk, ml_i_b, ml_f_b,
                  ml_norm, ml_skip, lru_conv_w, lru_conv_b, lru_wa, lru_ba, lru_wx, lru_bx,
                  lru_lambda, lru_norm):
    gw = GROUP_W
    row = lambda a: a.reshape(1, -1).astype(F32)

    def lora_rows(w, start):
        z = jnp.zeros((LORA_PAD, gw), F32)
        return lax.dynamic_update_slice(z, w.astype(F32), (start, 0)).astype(BF16)

    head_id = jnp.arange(MXU_TILE) // RW_HEAD
    seg1 = (head_id[:, None] == head_id[None, :]).astype(F32)

    def diag_tiles(blocks):
        half = blocks.shape[0] // 2
        return jnp.stack([_block_diag(blocks[:half]), _block_diag(blocks[half:])]).astype(BF16)
    hg = dict(lb=row(lower_bounds[l]), gnorm=row(hg_norm[l]))
    rw = dict(mu=row(jnp.pad(rw_mu[l], (0, RW_PAD - RW_IN))), w0=row(rw_w0[l]),
              w2=lora_rows(rw_w2[l], 0), a0=row(rw_a0[l]),
              a2=lora_rows(rw_a2[l], RW_W_LORA), g2=lora_rows(rw_g2[l], RW_W_LORA + RW_A_LORA),
              kk=row(rw_kk[l]), ka=row(rw_ka[l]), rk=row(rw_rk[l]), ln_w=row(rw_ln_w[l]),
              ln_b=row(rw_ln_b[l]), seg1=seg1.astype(BF16), segm=(seg1 / RW_HEAD).astype(BF16))
    gate_b = jnp.concatenate([ml_i_b[l], ml_f_b[l], jnp.zeros((LANES - 2 * ML_HEADS,), F32)])
    ml = dict(conv_w=ml_conv_w[l].astype(F32), conv_b=row(ml_conv_b[l]),
              wq=diag_tiles(ml_wq[l]), wk=diag_tiles(ml_wk[l]),
              gate_b=row(gate_b), norm=row(ml_norm[l]), skip=row(ml_skip[l]))
    lru = dict(conv_w=lru_conv_w[l].astype(F32), conv_b=row(lru_conv_b[l]),
               wa=diag_tiles(lru_wa[l]), ba=row(lru_ba[l]),
               wx=diag_tiles(lru_wx[l]), bx=row(lru_bx[l]),
               lam=row(lru_lambda[l]), norm=row(lru_norm[l]))
    return ([hg[k] for k in _HG_KEYS] + [rw[k] for k in _RW_KEYS]
            + [ml[k] for k in _ML_KEYS] + [lru[k] for k in _LRU_KEYS])


def _mixer_call(x, gain, mod5, l, sub, w_in_p, params):
    bsz, seqlen, d = x.shape
    d_mix = 4 * GROUP_W
    rows = MIX_ROWS if bsz % MIX_ROWS == 0 else 1
    last_chunk = seqlen // CHUNK - 1
    once = pl.Buffered(1)
    full = lambda a: pl.BlockSpec(a.shape, lambda b, c: (0,) * a.ndim, pipeline_mode=once)
    vec = lambda k: pl.BlockSpec((None, rows, None, 1, d), lambda b, c: (l, b, 3 * sub + k, 0, 0))
    scratch = [
        pltpu.VMEM((rows, CHUNK, U_TOT), F32),
        pltpu.VMEM((rows, CHUNK, U_TOT), F32),
        pltpu.VMEM((rows, HG_HEADS, LANES, LANES), F32),
        pltpu.VMEM((rows, HIST + CHUNK, RW_PAD), F32),
        pltpu.VMEM((rows, RW_HEADS // 2, LANES, LANES), F32),
        pltpu.VMEM((rows, CHUNK, GROUP_W), F32),
        pltpu.VMEM((rows, HIST + CHUNK, GROUP_W), F32),
        pltpu.VMEM((rows, ML_HEADS, LANES, LANES), F32),
        pltpu.VMEM((rows, ML_HEADS, 8, LANES), F32),
        pltpu.VMEM((rows, ML_HEADS, 8, LANES), F32),
        pltpu.VMEM((rows, HIST + CHUNK, GROUP_W), F32),
        pltpu.VMEM((rows, 8, GROUP_W), F32),
    ]
    return pl.pallas_call(
        _mixer_kernel,
        grid=(bsz // rows, seqlen // (2 * CHUNK)),
        in_specs=[pl.BlockSpec((rows, 2 * CHUNK, d), lambda b, c: (b, c, 0)),
                  pl.BlockSpec((rows, CHUNK, d), lambda b, c: (b, jnp.minimum(2 * c + 2, last_chunk), 0)),
                  pl.BlockSpec((1, d), lambda b, c: (0, 0)),
                  vec(0), vec(1),
                  pl.BlockSpec((None, d, U_TOT), lambda b, c: (l, 0, 0), pipeline_mode=once)]
        + [full(a) for a in params],
        out_specs=pl.BlockSpec((rows, 2 * CHUNK, d_mix), lambda b, c: (b, c, 0)),
        out_shape=jax.ShapeDtypeStruct((bsz, seqlen, d_mix), BF16),
        scratch_shapes=scratch,
        compiler_params=pltpu.CompilerParams(
            dimension_semantics=("parallel", "arbitrary"), vmem_limit_bytes=MIXER_VMEM_LIMIT),
        name="mixers",
    )(x, x, gain.reshape(1, d), mod5, mod5, w_in_p, *params)


def _pad_w_in(w):
    z = lambda n: jnp.zeros(w.shape[:-1] + (n,), BF16)
    c1, c2 = HG_IN + RW_IN, HG_IN + RW_IN + ML_IN
    wb = w.astype(BF16)
    return jnp.concatenate(
        [wb[..., :c1], z(RW_PAD - RW_IN), wb[..., c1:c2], z(ML_PAD - ML_IN), wb[..., c2:],
         z(U_TOT - U_LRU - LRU_IN)], axis=-1)


def kernel(x, c, norm_gain, mod_w, mod_b, ffn_w1, ffn_w3, ffn_w2, w_in, w_out, hg_lb_logits, hg_norm, rw_mu, rw_w0, rw_w2, rw_a0, rw_a2, rw_g2, rw_kk, rw_ka, rw_rk, rw_ln_w, rw_ln_b, ml_conv_w, ml_conv_b, ml_wq, ml_wk, ml_i_b, ml_f_b, ml_norm, ml_skip, lru_conv_w, lru_conv_b, lru_wa, lru_ba, lru_wx, lru_bx, lru_lambda, lru_norm, final_norm):
    bsz = x.shape[0]
    depth = mod_w.shape[0]
    lb_w = jax.nn.softmax(hg_lb_logits.astype(F32), axis=0)
    lower_bounds = jnp.cumsum(lb_w, axis=0) - lb_w[0]
    mod = _mod_call(c, mod_w, mod_b)
    mod5 = mod.reshape(depth, bsz, 3 * N_SUB, 1, D_MODEL)
    w1, w3, w2 = ffn_w1.astype(BF16), ffn_w3.astype(BF16), ffn_w2.astype(BF16)
    w_in_p, w_out_b = _pad_w_in(w_in), w_out.astype(BF16)
    for l in range(depth):
        x = _ffn_call(x, norm_gain[l, 0], mod5, l, 0, 0, w1, w3, w2)
        params = _mixer_params(
            l, lower_bounds, hg_norm, rw_mu, rw_w0, rw_w2, rw_a0, rw_a2, rw_g2, rw_kk, rw_ka, rw_rk,
            rw_ln_w, rw_ln_b, ml_conv_w, ml_conv_b, ml_wq, ml_wk, ml_i_b, ml_f_b, ml_norm, ml_skip,
            lru_conv_w, lru_conv_b, lru_wa, lru_ba, lru_wx, lru_bx, lru_lambda, lru_norm)
        y = _mixer_call(x, norm_gain[l, 1], mod5, l, 1, w_in_p, params)
        x = _outproj_call(x, y, mod5, l, 1, w_out_b)
        last = l == depth - 1
        x = _ffn_call(x, norm_gain[l, 2], mod5, l, 2, 1, w1, w3, w2,
                      final_gain=final_norm if last else None)
    return x
```

```python
import functools

import jax
import jax.numpy as jnp
from jax import lax
from jax.experimental import pallas as pl
from jax.experimental.pallas import tpu as pltpu

F32 = jnp.float32
BF16 = jnp.bfloat16

D_MODEL = 2048
DEPTH = 2
GROUP_W = 512
N_SUB = 3
D_FF = 5632
NORM_EPS = 1e-6
HG_HEADS = 4
RW_HEAD = 64
RW_HEADS = 8
RW_W_LORA = 32
RW_A_LORA = 32
RW_G_LORA = 96
RW_LN_EPS = 64e-5
RW_IN = 3 * GROUP_W + RW_W_LORA + RW_A_LORA + RW_G_LORA
ML_HEADS = 4
ML_DIM = 128
ML_IN = 3 * GROUP_W + 2 * ML_HEADS
NEG_BIG = -1e30
LRU_C = 8.0
HG_IN = 4 * GROUP_W
LRU_IN = 2 * GROUP_W

LANES = 128
MXU_TILE = 256
CHUNK = 64
MIX_ROWS = 2
SUB = 16
RW_PAD = 1792
ML_PAD = 1664
U_HG = 0
U_RW = U_HG + HG_IN
U_ML = U_RW + RW_PAD
U_LRU = U_ML + ML_PAD
U_TOT = 6656
LORA_PAD = RW_PAD - 3 * GROUP_W
VMEM_LIMIT = 56 * 1024 * 1024
MIXER_VMEM_LIMIT = 60 * 1024 * 1024
PROJ_TILE = 512
FFN_VMEM_LIMIT = 62 * 1024 * 1024
FFN_ROW_BLOCK = 128


def _sigmoid(x):
    return 0.5 * jnp.tanh(0.5 * x) + 0.5


def _silu(x):
    return x * _sigmoid(x)


def _softplus(x):
    return jnp.maximum(x, 0.0) + jnp.log(1.0 + jnp.exp(-jnp.abs(x)))


def _dot(a, b):
    return jnp.dot(a.astype(BF16), b.astype(BF16), preferred_element_type=F32)


def _dot_nt(a, b):
    return lax.dot_general(a.astype(BF16), b.astype(BF16), (((1,), (1,)), ((), ())),
                           preferred_element_type=F32)


def _dot_tn(a, b):
    return lax.dot_general(a.astype(BF16), b.astype(BF16), (((0,), (0,)), ((), ())),
                           preferred_element_type=F32)


def _split3(x):
    hi = x.astype(BF16)
    r1 = x - hi.astype(F32)
    mid = r1.astype(BF16)
    lo = (r1 - mid.astype(F32)).astype(BF16)
    return hi, mid, lo


def _cumsum_rows(x, tri):
    hi, mid, lo = _split3(x)
    d = functools.partial(jnp.dot, preferred_element_type=F32)
    return d(tri, hi) + d(tri, mid) + d(tri, lo)


def _dot_halves(x, w_lo, w_hi):
    d = functools.partial(jnp.dot, preferred_element_type=F32)
    return jnp.concatenate([d(x[:, :MXU_TILE], w_lo), d(x[:, MXU_TILE:], w_hi)], axis=1)


def _seg_dot(x, seg):
    hi = x.astype(BF16)
    lo = (x - hi.astype(F32)).astype(BF16)
    return _dot_halves(hi, seg, seg) + _dot_halves(lo, seg, seg)


def _dot_bd(x, w_ref):
    return _dot_halves(x.astype(BF16), w_ref[0], w_ref[1])


HIST = 8


def _push_history(hist_ref, x):
    hist_ref[0:HIST, :] = hist_ref[CHUNK:CHUNK + HIST, :]
    hist_ref[HIST:HIST + CHUNK, :] = x


def _delayed(hist_ref, d):
    return hist_ref[HIST - d:HIST - d + CHUNK, :]


def _modulated_norm(x, gain, shift, scale):
    y = x * lax.rsqrt(jnp.mean(x * x, axis=-1, keepdims=True) + NORM_EPS) * gain
    return y * (1.0 + scale) + shift


def _mod_kernel(c_ref, w_ref, b_ref, o_ref):
    o_ref[...] = _dot(_silu(c_ref[...]), w_ref[...]) + b_ref[...]


def _mod_call(c, mod_w, mod_b):
    depth, d, n = mod_w.shape
    bsz = c.shape[0]
    tn = 1024
    return pl.pallas_call(
        _mod_kernel,
        grid=(depth, n // tn),
        in_specs=[pl.BlockSpec((bsz, d), lambda l, j: (0, 0)),
                  pl.BlockSpec((None, d, tn), lambda l, j: (l, 0, j)),
                  pl.BlockSpec((None, 1, tn), lambda l, j: (l, 0, j))],
        out_specs=pl.BlockSpec((None, bsz, tn), lambda l, j: (l, 0, j)),
        out_shape=jax.ShapeDtypeStruct((depth, bsz, n), F32),
        compiler_params=pltpu.CompilerParams(
            dimension_semantics=("arbitrary", "arbitrary"), vmem_limit_bytes=VMEM_LIMIT),
        name="adaln_mod",
    )(c, mod_w, mod_b.reshape(depth, 1, n))


def _ffn_kernel(x_ref, gain_ref, shift_ref, scale_ref, gate_ref, w1_ref, w3_ref, w2_ref, *rest,
                n_ff, final):
    if final:
        fgain_ref, o_ref, h_ref = rest
    else:
        o_ref, h_ref = rest
    j = pl.program_id(2)

    def row_blocks(fn):
        def body(k, carry):
            fn(pl.ds(pl.multiple_of(k * FFN_ROW_BLOCK, FFN_ROW_BLOCK), FFN_ROW_BLOCK))
            return carry
        lax.fori_loop(0, x_ref.shape[0] // FFN_ROW_BLOCK, body, 0)

    @pl.when(j == 0)
    def _():
        def norm(rows):
            h = _modulated_norm(x_ref[rows, :], gain_ref[...], shift_ref[...], scale_ref[...])
            h_ref[rows, :] = h.astype(BF16)
        row_blocks(norm)

    h = h_ref[...]
    a = jnp.dot(h, w1_ref[...], preferred_element_type=F32)
    b = jnp.dot(h, w3_ref[...], preferred_element_type=F32)
    g = (_silu(a) * b).astype(BF16)
    acc = jnp.where(j == 0, 0.0, o_ref[...])
    o_ref[...] = acc + jnp.dot(g, w2_ref[...], preferred_element_type=F32)

    @pl.when(j == n_ff - 1)
    def _():
        def residual(rows):
            xn = x_ref[rows, :] + (0.5 * (1.0 + gate_ref[...])) * o_ref[rows, :]
            if final:
                xn = xn * lax.rsqrt(jnp.mean(xn * xn, axis=-1, keepdims=True) + NORM_EPS) * fgain_ref[...]
            o_ref[rows, :] = xn
        row_blocks(residual)


def _ffn_call(x, gain, mod5, l, sub, s, w1, w3, w2, final_gain=None, tm=1024, tf=512):
    bsz, seqlen, d = x.shape
    tm = min(tm, seqlen)
    n_ff = w1.shape[-1] // tf
    final = final_gain is not None
    vec = lambda k: pl.BlockSpec((None, None, None, 1, d), lambda b, i, j: (l, b, 3 * sub + k, 0, 0))
    in_specs = [pl.BlockSpec((None, tm, d), lambda b, i, j: (b, i, 0)),
                pl.BlockSpec((1, d), lambda b, i, j: (0, 0)),
                vec(0), vec(1), vec(2),
                pl.BlockSpec((None, None, d, tf), lambda b, i, j: (l, s, 0, j)),
                pl.BlockSpec((None, None, d, tf), lambda b, i, j: (l, s, 0, j)),
                pl.BlockSpec((None, None, tf, d), lambda b, i, j: (l, s, j, 0))]
    args = [x, gain.reshape(1, d), mod5, mod5, mod5, w1, w3, w2]
    if final:
        in_specs.append(pl.BlockSpec((1, d), lambda b, i, j: (0, 0)))
        args.append(final_gain.reshape(1, d))
    return pl.pallas_call(
        functools.partial(_ffn_kernel, n_ff=n_ff, final=final),
        grid=(bsz, seqlen // tm, n_ff),
        in_specs=in_specs,
        out_specs=pl.BlockSpec((None, tm, d), lambda b, i, j: (b, i, 0)),
        out_shape=jax.ShapeDtypeStruct(x.shape, F32),
        scratch_shapes=[pltpu.VMEM((tm, d), BF16)],
        compiler_params=pltpu.CompilerParams(
            dimension_semantics=("parallel", "parallel", "arbitrary"), vmem_limit_bytes=FFN_VMEM_LIMIT),
        name="ffn_final" if final else "ffn",
    )(*args)


def _outproj_kernel(x_ref, y_ref, gate_ref, w_ref, o_ref):
    o_ref[...] = x_ref[...] + (1.0 + gate_ref[...]) * jnp.dot(
        y_ref[...], w_ref[...], preferred_element_type=F32)


def _outproj_call(x, y, mod5, l, sub, w, tm=512):
    bsz, seqlen, d = x.shape
    tm = min(tm, seqlen)
    return pl.pallas_call(
        _outproj_kernel,
        grid=(bsz, seqlen // tm),
        in_specs=[pl.BlockSpec((None, tm, d), lambda b, i: (b, i, 0)),
                  pl.BlockSpec((None, tm, y.shape[-1]), lambda b, i: (b, i, 0)),
                  pl.BlockSpec((None, None, None, 1, d), lambda b, i: (l, b, 3 * sub + 2, 0, 0)),
                  pl.BlockSpec((None,) + w.shape[1:], lambda b, i: (l, 0, 0))],
        out_specs=pl.BlockSpec((None, tm, d), lambda b, i: (b, i, 0)),
        out_shape=jax.ShapeDtypeStruct(x.shape, F32),
        compiler_params=pltpu.CompilerParams(
            dimension_semantics=("parallel", "parallel"), vmem_limit_bytes=VMEM_LIMIT),
        name="out_proj",
    )(x, y, mod5, w)


def _hgrn2_chunk(u_ref, lb, gnorm, tri, st_ref, y_ref):
    gw = GROUP_W
    q = _silu(u_ref[:, U_HG:U_HG + gw])
    f_raw = u_ref[:, U_HG + gw:U_HG + 2 * gw]
    v = u_ref[:, U_HG + 2 * gw:U_HG + 3 * gw]
    g_raw = u_ref[:, U_HG + 3 * gw:U_HG + 4 * gw]
    e = jnp.exp(-f_raw)
    log_f = jnp.log(1.0 + lb * e) - jnp.log(1.0 + e)
    k = (1.0 - lb) * _sigmoid(-f_raw)
    yield
    bcum = _cumsum_rows(log_f, tri)
    yield
    sub_row = lax.broadcasted_iota(jnp.int32, (SUB, 1), 0)
    for h in range(HG_HEADS):
        sl = slice(h * LANES, (h + 1) * LANES)
        qh, kh, vh, bh = q[:, sl], k[:, sl], v[:, sl], bcum[:, sl]
        st = st_ref[h]
        o_inter = _dot_nt(qh * jnp.exp(bh), st)
        pieces = []
        for i in range(CHUNK // SUB):
            r0 = i * SUB
            qd, kd, vd, bd = qh[r0:r0 + SUB], kh[r0:r0 + SUB], vh[r0:r0 + SUB], bh[r0:r0 + SUB]
            oi = jnp.zeros((SUB, LANES), F32)
            if i > 0:
                bref = bh[r0 - 1:r0]
                qi = qd * jnp.exp(bd - bref)
                kp = kh[0:r0] * jnp.exp(bref - bh[0:r0])
                oi = _dot(_dot_nt(qi, kp), vh[0:r0])
            for j in range(SUB):
                w = jnp.exp(jnp.minimum(bd - bd[j:j + 1], 0.0))
                col = jnp.sum(qd * w * kd[j:j + 1], axis=-1, keepdims=True)
                col = jnp.where(sub_row >= j, col, 0.0)
                oi = oi + col * vd[j:j + 1]
            pieces.append(oi)
            yield
        o = o_inter + jnp.concatenate(pieces, axis=0)
        b_last = bh[CHUNK - 1:CHUNK]
        st_ref[h] = st * jnp.exp(b_last) + _dot_tn(vh, kh * jnp.exp(b_last - bh))
        on = o * lax.rsqrt(jnp.mean(o * o, axis=-1, keepdims=True) + NORM_EPS) * gnorm[:, sl]
        y_ref[:, sl] = (on * _silu(g_raw[:, sl])).astype(y_ref.dtype)
        yield


def _rwkv7_chunk(u_ref, p, tri, prev_ref, ht_ref, o_scr, y_ref):
    gw = GROUP_W
    u_raw = u_ref[:, U_RW:U_RW + RW_PAD]
    _push_history(prev_ref, u_raw)
    u = u_raw + (_delayed(prev_ref, 1) - u_raw) * p["mu"][...]
    yield
    r, k, v, lora = u[:, 0:gw], u[:, gw:2 * gw], u[:, 2 * gw:3 * gw], u[:, 3 * gw:RW_PAD]
    zw = p["w0"][...] + _dot(jnp.tanh(lora), p["w2"][...])
    lw = -jnp.exp(-_softplus(-zw) - 0.5)
    iclr = _sigmoid(p["a0"][...] + _dot(lora, p["a2"][...]))
    gate = _dot(_sigmoid(lora), p["g2"][...])
    yield
    kk = k * p["kk"][...]
    ss = _seg_dot(kk * kk, p["seg1"][...])
    kk = kk / jnp.maximum(jnp.sqrt(ss + 1e-12), 1e-6)
    k = k * (1.0 + (iclr - 1.0) * p["ka"][...])
    a_vec = -kk
    b_vec = kk * iclr
    yield

    lc = _cumsum_rows(lw, tri)
    l_last = lc[CHUNK - 1:CHUNK]
    g_in = jnp.exp(lc)
    g_out = jnp.exp(-lc)
    g_end = jnp.exp(l_last - lc)
    a_s = a_vec * jnp.exp(lc - lw)
    r_s = r * g_in
    b_s = b_vec * g_out
    k_s = k * g_out
    b_e = b_vec * g_end
    k_e = k * g_end
    decay_end = jnp.exp(l_last)
    yield

    lane = lax.broadcasted_iota(jnp.int32, (1, LANES), 1)
    m_a = (lane < RW_HEAD).astype(F32)
    m_b = 1.0 - m_a
    stack = lambda t: jnp.concatenate([t * m_a, t * m_b], axis=0)
    n2 = 2 * CHUNK
    row = lax.broadcasted_iota(jnp.int32, (n2, n2), 0)
    col = lax.broadcasted_iota(jnp.int32, (n2, n2), 1)
    same = (row // CHUNK) == (col // CHUNK)
    strict = same & (row > col)
    incl = same & (row >= col)
    pairs = range(RW_HEADS // 2)
    sls = [slice(pr * LANES, (pr + 1) * LANES) for pr in pairs]
    bf = lambda t: t.astype(BF16)
    cat_k = lambda lhs, rhs: jnp.dot(jnp.concatenate(lhs, axis=1), jnp.concatenate(rhs, axis=0),
                                     preferred_element_type=F32)
    la = [stack(a_s[:, sl]) for sl in sls]
    lr = [stack(r_s[:, sl]) for sl in sls]
    v_st = [bf(stack(v[:, sl])) for sl in sls]
    hs = [ht_ref[pr] for pr in pairs]
    hs_b = [bf(m) for m in hs]
    sc = [_dot_nt(jnp.concatenate([la[pr], lr[pr]], axis=0),
                  jnp.concatenate([stack(b_s[:, sls[pr]]), stack(k_s[:, sls[pr]])], axis=0))
          for pr in pairs]
    yield
    pw = [jnp.where(strict, s[0:n2, 0:n2], 0.0) for s in sc]
    x = [cat_k([bf(jnp.where(strict, sc[pr][0:n2, n2:2 * n2], 0.0)), bf(la[pr])], [v_st[pr], hs_b[pr]])
         for pr in pairs]
    for step in range(6):
        x = [x[pr] + _dot(pw[pr], x[pr]) for pr in pairs]
        if step < 5:
            pw = [_dot(m, m) for m in pw]
        yield
    eye2 = row == col
    for pr in pairs:
        sl = sls[pr]
        a_rb = jnp.where(incl, sc[pr][n2:2 * n2, 0:n2], 0.0)
        a_rk = jnp.where(incl, sc[pr][n2:2 * n2, n2:2 * n2], 0.0)
        xb = bf(x[pr])
        o_st = cat_k([bf(lr[pr]), bf(a_rb), bf(a_rk)], [hs_b[pr], xb, v_st[pr]])
        o_scr[:, sl] = o_st[0:CHUNK] + o_st[CHUNK:n2]
        dec_col = jnp.sum(jnp.where(eye2, decay_end[:, sl], 0.0), axis=1, keepdims=True)
        ht_ref[pr] = hs[pr] * dec_col + lax.dot_general(
            jnp.concatenate([bf(stack(b_e[:, sl])), bf(stack(k_e[:, sl]))], axis=0),
            jnp.concatenate([xb, v_st[pr]], axis=0), (((0,), (0,)), ((), ())),
            preferred_element_type=F32)
        yield

    o = o_scr[...]
    mean = _seg_dot(o, p["segm"][...])
    cen = o - mean
    var = _seg_dot(cen * cen, p["segm"][...])
    yield
    o = cen * lax.rsqrt(var + RW_LN_EPS) * p["ln_w"][...] + p["ln_b"][...]
    bonus = _seg_dot(r * k * p["rk"][...], p["seg1"][...]) * v
    y_ref[...] = ((o + bonus) * gate).astype(y_ref.dtype)


def _causal_conv4(x, hist_ref, w_ref, b_ref):
    _push_history(hist_ref, x)
    y = x * w_ref[3:4, :] + b_ref[...]
    for d in (1, 2, 3):
        y = y + _delayed(hist_ref, d) * w_ref[3 - d:4 - d, :]
    return y


def _mlstm_chunk(u_ref, p, tri, prev_ref, m_ref, n_ref, run_ref, y_ref):
    gw = GROUP_W
    xm = u_ref[:, U_ML:U_ML + gw]
    v = u_ref[:, U_ML + gw:U_ML + 2 * gw]
    o_raw = u_ref[:, U_ML + 2 * gw:U_ML + 3 * gw]
    gates = u_ref[:, U_ML + 3 * gw:U_ML + ML_PAD] + p["gate_b"][...]
    xc = _silu(_causal_conv4(xm, prev_ref, p["conv_w"], p["conv_b"]))
    q = _dot_bd(xc, p["wq"]) * (ML_DIM ** -0.5)
    k = _dot_bd(xc, p["wk"])
    yield
    log_f = jnp.minimum(gates, 0.0) - jnp.log(1.0 + jnp.exp(-jnp.abs(gates)))
    bcum = _cumsum_rows(log_f, tri)
    row = lax.broadcasted_iota(jnp.int32, (CHUNK, CHUNK), 0)
    col = lax.broadcasted_iota(jnp.int32, (CHUNK, CHUNK), 1)
    causal = row >= col
    eye = row == col
    to_row = lambda c: jnp.sum(jnp.where(eye, c, 0.0), axis=0, keepdims=True)
    ones = jnp.ones((CHUNK, LANES), BF16)
    for h in range(ML_HEADS):
        sl = slice(h * LANES, (h + 1) * LANES)
        qh, kh, vh = q[:, sl], k[:, sl], v[:, sl]
        b_col = bcum[:, ML_HEADS + h:ML_HEADS + h + 1]
        i_col = gates[:, h:h + 1]
        m_prev = run_ref[h][0:1, 0:1]
        n_prev = n_ref[h][0:1, :]
        mat = m_ref[h]
        lw = b_col - to_row(b_col) + to_row(i_col)
        lprev = b_col + m_prev
        m_t = jnp.maximum(lprev, jnp.max(jnp.where(causal, lw, NEG_BIG), axis=-1, keepdims=True))
        wts = jnp.where(causal, jnp.exp(jnp.where(causal, lw - m_t, 0.0)), 0.0)
        s = _dot_nt(qh, kh) * wts
        wp = jnp.exp(lprev - m_t)
        yield
        lhs = jnp.concatenate([(wp * qh).astype(BF16), s.astype(BF16)], axis=1)
        rhs = jnp.concatenate(
            [jnp.concatenate([mat.astype(BF16), jnp.zeros((LANES, LANES), BF16)], axis=1),
             jnp.concatenate([vh.astype(BF16), ones], axis=1)], axis=0)
        sv = jnp.dot(lhs, rhs, preferred_element_type=F32)
        num = sv[:, 0:LANES]
        den = sv[:, LANES:LANES + 1] + wp * _dot_nt(qh, n_ref[h])[:, 0:1]
        hh = num / jnp.maximum(jnp.abs(den), jnp.exp(-m_t))
        yield
        m_new = m_t[CHUNK - 1:CHUNK]
        b_end = b_col[CHUNK - 1:CHUNK]
        wl = jnp.exp(b_end - b_col + i_col - m_new)
        dec = jnp.exp(b_end + m_prev - m_new)
        kw = kh * wl
        m_ref[h] = dec * mat + _dot_tn(kw, vh)
        n_ref[h] = jnp.broadcast_to(dec * n_prev + jnp.sum(kw, axis=0, keepdims=True), (8, LANES))
        run_ref[h] = jnp.broadcast_to(m_new, (8, LANES))
        hg = _sigmoid(o_raw[:, sl]) * hh
        hn = hg * lax.rsqrt(jnp.mean(hg * hg, axis=-1, keepdims=True) + NORM_EPS) * p["norm"][:, sl]
        y_ref[:, sl] = (hn + p["skip"][:, sl] * xc[:, sl]).astype(y_ref.dtype)
        yield


def _rglru_chunk(u_ref, p, prev_ref, h_ref, y_ref):
    gw = GROUP_W
    xb = u_ref[:, U_LRU:U_LRU + gw]
    gb = u_ref[:, U_LRU + gw:U_LRU + 2 * gw]
    xc = _causal_conv4(xb, prev_ref, p["conv_w"], p["conv_b"])
    r = _sigmoid(_dot_bd(xc, p["wa"]) + p["ba"][...])
    i = _sigmoid(_dot_bd(xc, p["wx"]) + p["bx"][...])
    yield
    log_a = -LRU_C * r * _softplus(-p["lam"][...])
    a = jnp.exp(log_a)
    bt = jnp.sqrt(1.0 - jnp.exp(2.0 * log_a)) * (i * xc)
    rows = lax.broadcasted_iota(jnp.int32, a.shape, 0)
    d = 1
    while d < CHUNK:
        keep = rows >= d
        bt = a * jnp.where(keep, pltpu.roll(bt, d, 0), 0.0) + bt
        a = a * jnp.where(keep, pltpu.roll(a, d, 0), 1.0)
        d *= 2
        yield
    hs = bt + a * h_ref[0:1, :]
    h_ref[...] = jnp.broadcast_to(hs[CHUNK - 1:CHUNK], h_ref.shape)
    gelu = 0.5 * gb * (1.0 + jnp.tanh(0.7978845608028654 * (gb + 0.044715 * (gb * gb * gb))))
    yl = hs * gelu
    y_ref[...] = (yl * lax.rsqrt(jnp.mean(yl * yl, axis=-1, keepdims=True) + NORM_EPS)
                  * p["norm"][...]).astype(y_ref.dtype)


def _round_robin(stages):
    live = [[g, 0, n] for g, n in stages]
    while live:
        entry = min(live, key=lambda e: (e[1] + 1) / e[2])
        try:
            next(entry[0])
            entry[1] += 1
        except StopIteration:
            live.remove(entry)


_HG_KEYS = ("lb", "gnorm")
_RW_KEYS = ("mu", "w0", "w2", "a0", "a2", "g2", "kk", "ka", "rk", "ln_w", "ln_b", "seg1", "segm")
_ML_KEYS = ("conv_w", "conv_b", "wq", "wk", "gate_b", "norm", "skip")
_LRU_KEYS = ("conv_w", "conv_b", "wa", "ba", "wx", "bx", "lam", "norm")


def _proj_stages(x_ref, t0, u_dst, gain, shift_ref, scale_ref, w_ref):
    rows = u_dst.shape[0]
    h = jnp.concatenate(
        [_modulated_norm(x_ref[r, t0:t0 + CHUNK, :], gain, shift_ref[r], scale_ref[r]).astype(BF16)
         for r in range(rows)], axis=0)
    yield
    for j in range(U_TOT // PROJ_TILE):
        cols = slice(j * PROJ_TILE, (j + 1) * PROJ_TILE)
        t = jnp.dot(h, w_ref[:, cols], preferred_element_type=F32)
        for r in range(rows):
            u_dst[r, :, cols] = t[r * CHUNK:(r + 1) * CHUNK]
        yield


def _mixer_kernel(*refs):
    it = iter(refs)
    x2_ref, xn_ref, gain_ref, shift_ref, scale_ref, w_ref = (next(it) for _ in range(6))
    hg = {k: next(it) for k in _HG_KEYS}
    rw = {k: next(it) for k in _RW_KEYS}
    ml = {k: next(it) for k in _ML_KEYS}
    lru = {k: next(it) for k in _LRU_KEYS}
    y_ref = next(it)
    (u_a, u_b, hg_st, rw_prev, rw_ht, rw_o, ml_prev, ml_m, ml_n, ml_run, lru_prev, lru_h) = it
    proj = functools.partial(_proj_stages, gain=gain_ref[...], shift_ref=shift_ref,
                             scale_ref=scale_ref, w_ref=w_ref)

    @pl.when(pl.program_id(1) == 0)
    def _():
        for ref in (hg_st, rw_prev, rw_ht, ml_prev, ml_m, ml_n, ml_run, lru_prev, lru_h):
            ref[...] = jnp.zeros_like(ref)
        for _ in proj(x2_ref, 0, u_a):
            pass

    r_i = lax.broadcasted_iota(jnp.int32, (CHUNK, CHUNK), 0)
    c_i = lax.broadcasted_iota(jnp.int32, (CHUNK, CHUNK), 1)
    tri = (r_i >= c_i).astype(BF16)
    gw = GROUP_W

    def mixers(u_buf, t0):
        stages = []
        for r in range(u_buf.shape[0]):
            u_r = u_buf.at[r]
            y_g = lambda g: y_ref.at[r, t0:t0 + CHUNK, g * gw:(g + 1) * gw]
            stages += [
                (_hgrn2_chunk(u_r, hg["lb"][...], hg["gnorm"][...], tri, hg_st.at[r], y_g(0)),
                 2 + HG_HEADS * (CHUNK // SUB + 1)),
                (_rwkv7_chunk(u_r, rw, tri, rw_prev.at[r], rw_ht.at[r], rw_o.at[r], y_g(1)),
                 12 + RW_HEADS // 2),
                (_mlstm_chunk(u_r, ml, tri, ml_prev.at[r], ml_m.at[r], ml_n.at[r], ml_run.at[r], y_g(2)),
                 1 + 3 * ML_HEADS),
                (_rglru_chunk(u_r, lru, lru_prev.at[r], lru_h.at[r], y_g(3)), 7)]
        return stages

    n_proj = 1 + U_TOT // PROJ_TILE
    _round_robin([(proj(x2_ref, CHUNK, u_b), n_proj)] + mixers(u_a, 0))
    _round_robin([(proj(xn_ref, 0, u_a), n_proj)] + mixers(u_b, CHUNK))


def _block_diag(blocks):
    n, a, b = blocks.shape
    tiled = jnp.tile(blocks.reshape(n * a, b), (1, n))
    same = (jnp.arange(n * a)[:, None] // a) == (jnp.arange(n * b)[None, :] // b)
    return jnp.where(same, tiled, jnp.zeros_like(tiled))


def _mixer_params(l, lower_bounds, hg_norm, rw_mu, rw_w0, rw_w2, rw_a0, rw_a2, rw_g2, rw_kk, rw_ka,
                  rw_rk, rw_ln_w, rw_ln_b, ml_conv_w, ml_conv_b, ml_wq, ml_wk, ml_i_b, ml_f_b,
                  ml_norm, ml_skip, lru_conv_w, lru_conv_b, lru_wa, lru_ba, lru_wx, lru_bx,
                  lru_lambda, lru_norm):
    gw = GROUP_W
    row = lambda a: a.reshape(1, -1).astype(F32)

    def lora_rows(w, start):
        z = jnp.zeros((LORA_PAD, gw), F32)
        return lax.dynamic_update_slice(z, w.astype(F32), (start, 0)).astype(BF16)

    head_id = jnp.arange(MXU_TILE) // RW_HEAD
    seg1 = (head_id[:, None] == head_id[None, :]).astype(F32)

    def diag_tiles(blocks):
        half = blocks.shape[0] // 2
        return jnp.stack([_block_diag(blocks[:half]), _block_diag(blocks[half:])]).astype(BF16)
    hg = dict(lb=row(lower_bounds[l]), gnorm=row(hg_norm[l]))
    rw = dict(mu=row(jnp.pad(rw_mu[l], (0, RW_PAD - RW_IN))), w0=row(rw_w0[l]),
              w2=lora_rows(rw_w2[l], 0), a0=row(rw_a0[l]),
              a2=lora_rows(rw_a2[l], RW_W_LORA), g2=lora_rows(rw_g2[l], RW_W_LORA + RW_A_LORA),
              kk=row(rw_kk[l]), ka=row(rw_ka[l]), rk=row(rw_rk[l]), ln_w=row(rw_ln_w[l]),
              ln_b=row(rw_ln_b[l]), seg1=seg1.astype(BF16), segm=(seg1 / RW_HEAD).astype(BF16))
    gate_b = jnp.concatenate([ml_i_b[l], ml_f_b[l], jnp.zeros((LANES - 2 * ML_HEADS,), F32)])
    ml = dict(conv_w=ml_conv_w[l].astype(F32), conv_b=row(ml_conv_b[l]),
              wq=diag_tiles(ml_wq[l]), wk=diag_tiles(ml_wk[l]),
              gate_b=row(gate_b), norm=row(ml_norm[l]), skip=row(ml_skip[l]))
    lru = dict(conv_w=lru_conv_w[l].astype(F32), conv_b=row(lru_conv_b[l]),
               wa=diag_tiles(lru_wa[l]), ba=row(lru_ba[l]),
               wx=diag_tiles(lru_wx[l]), bx=row(lru_bx[l]),
               lam=row(lru_lambda[l]), norm=row(lru_norm[l]))
    return ([hg[k] for k in _HG_KEYS] + [rw[k] for k in _RW_KEYS]
            + [ml[k] for k in _ML_KEYS] + [lru[k] for k in _LRU_KEYS])


def _mixer_call(x, gain, mod5, l, sub, w_in_p, params):
    bsz, seqlen, d = x.shape
    d_mix = 4 * GROUP_W
    rows = MIX_ROWS if bsz % MIX_ROWS == 0 else 1
    last_chunk = seqlen // CHUNK - 1
    once = pl.Buffered(1)
    full = lambda a: pl.BlockSpec(a.shape, lambda b, c: (0,) * a.ndim, pipeline_mode=once)
    vec = lambda k: pl.BlockSpec((None, rows, None, 1, d), lambda b, c: (l, b, 3 * sub + k, 0, 0))
    scratch = [
        pltpu.VMEM((rows, CHUNK, U_TOT), F32),
        pltpu.VMEM((rows, CHUNK, U_TOT), F32),
        pltpu.VMEM((rows, HG_HEADS, LANES, LANES), F32),
        pltpu.VMEM((rows, HIST + CHUNK, RW_PAD), F32),
        pltpu.VMEM((rows, RW_HEADS // 2, LANES, LANES), F32),
        pltpu.VMEM((rows, CHUNK, GROUP_W), F32),
        pltpu.VMEM((rows, HIST + CHUNK, GROUP_W), F32),
        pltpu.VMEM((rows, ML_HEADS, LANES, LANES), F32),
        pltpu.VMEM((rows, ML_HEADS, 8, LANES), F32),
        pltpu.VMEM((rows, ML_HEADS, 8, LANES), F32),
        pltpu.VMEM((rows, HIST + CHUNK, GROUP_W), F32),
        pltpu.VMEM((rows, 8, GROUP_W), F32),
    ]
    return pl.pallas_call(
        _mixer_kernel,
        grid=(bsz // rows, seqlen // (2 * CHUNK)),
        in_specs=[pl.BlockSpec((rows, 2 * CHUNK, d), lambda b, c: (b, c, 0)),
                  pl.BlockSpec((rows, CHUNK, d), lambda b, c: (b, jnp.minimum(2 * c + 2, last_chunk), 0)),
                  pl.BlockSpec((1, d), lambda b, c: (0, 0)),
                  vec(0), vec(1),
                  pl.BlockSpec((None, d, U_TOT), lambda b, c: (l, 0, 0), pipeline_mode=once)]
        + [full(a) for a in params],
        out_specs=pl.BlockSpec((rows, 2 * CHUNK, d_mix), lambda b, c: (b, c, 0)),
        out_shape=jax.ShapeDtypeStruct((bsz, seqlen, d_mix), BF16),
        scratch_shapes=scratch,
        compiler_params=pltpu.CompilerParams(
            dimension_semantics=("parallel", "arbitrary"), vmem_limit_bytes=MIXER_VMEM_LIMIT),
        name="mixers",
    )(x, x, gain.reshape(1, d), mod5, mod5, w_in_p, *params)


def _pad_w_in(w):
    z = lambda n: jnp.zeros(w.shape[:-1] + (n,), BF16)
    c1, c2 = HG_IN + RW_IN, HG_IN + RW_IN + ML_IN
    wb = w.astype(BF16)
    return jnp.concatenate(
        [wb[..., :c1], z(RW_PAD - RW_IN), wb[..., c1:c2], z(ML_PAD - ML_IN), wb[..., c2:],
         z(U_TOT - U_LRU - LRU_IN)], axis=-1)


def kernel(x, c, norm_gain, mod_w, mod_b, ffn_w1, ffn_w3, ffn_w2, w_in, w_out, hg_lb_logits, hg_norm, rw_mu, rw_w0, rw_w2, rw_a0, rw_a2, rw_g2, rw_kk, rw_ka, rw_rk, rw_ln_w, rw_ln_b, ml_conv_w, ml_conv_b, ml_wq, ml_wk, ml_i_b, ml_f_b, ml_norm, ml_skip, lru_conv_w, lru_conv_b, lru_wa, lru_ba, lru_wx, lru_bx, lru_lambda, lru_norm, final_norm):
    bsz = x.shape[0]
    depth = mod_w.shape[0]
    lb_w = jax.nn.softmax(hg_lb_logits.astype(F32), axis=0)
    lower_bounds = jnp.cumsum(lb_w, axis=0) - lb_w[0]
    mod = _mod_call(c, mod_w, mod_b)
    mod5 = mod.reshape(depth, bsz, 3 * N_SUB, 1, D_MODEL)
    w1, w3, w2 = ffn_w1.astype(BF16), ffn_w3.astype(BF16), ffn_w2.astype(BF16)
    w_in_p, w_out_b = _pad_w_in(w_in), w_out.astype(BF16)
    for l in range(depth):
        x = _ffn_call(x, norm_gain[l, 0], mod5, l, 0, 0, w1, w3, w2)
        params = _mixer_params(
            l, lower_bounds, hg_norm, rw_mu, rw_w0, rw_w2, rw_a0, rw_a2, rw_g2, rw_kk, rw_ka, rw_rk,
            rw_ln_w, rw_ln_b, ml_conv_w, ml_conv_b, ml_wq, ml_wk, ml_i_b, ml_f_b, ml_norm, ml_skip,
            lru_conv_w, lru_conv_b, lru_wa, lru_ba, lru_wx, lru_bx, lru_lambda, lru_norm)
        y = _mixer_call(x, norm_gain[l, 1], mod5, l, 1, w_in_p, params)
        x = _outproj_call(x, y, mod5, l, 1, w_out_b)
        last = l == depth - 1
        x = _ffn_call(x, norm_gain[l, 2], mod5, l, 2, 1, w1, w3, w2,
                      final_gain=final_norm if last else None)
    return x
```

```python
import functools

import jax
import jax.numpy as jnp
from jax import lax
from jax.experimental import pallas as pl
from jax.experimental.pallas import tpu as pltpu

F32 = jnp.float32
BF16 = jnp.bfloat16

D_MODEL = 2048
DEPTH = 2
GROUP_W = 512
N_SUB = 3
D_FF = 5632
NORM_EPS = 1e-6
HG_HEADS = 4
RW_HEAD = 64
RW_HEADS = 8
RW_W_LORA = 32
RW_A_LORA = 32
RW_G_LORA = 96
RW_LN_EPS = 64e-5
RW_IN = 3 * GROUP_W + RW_W_LORA + RW_A_LORA + RW_G_LORA
ML_HEADS = 4
ML_DIM = 128
ML_IN = 3 * GROUP_W + 2 * ML_HEADS
NEG_BIG = -1e30
LRU_C = 8.0
HG_IN = 4 * GROUP_W
LRU_IN = 2 * GROUP_W

LANES = 128
MXU_TILE = 256
CHUNK = 64
MIX_ROWS = 2
SUB = 16
RW_PAD = 1792
ML_PAD = 1664
U_HG = 0
U_RW = U_HG + HG_IN
U_ML = U_RW + RW_PAD
U_LRU = U_ML + ML_PAD
U_TOT = 6656
LORA_PAD = RW_PAD - 3 * GROUP_W
VMEM_LIMIT = 56 * 1024 * 1024
MIXER_VMEM_LIMIT = 60 * 1024 * 1024
PROJ_TILE = 512
FFN_VMEM_LIMIT = 62 * 1024 * 1024
FFN_ROW_BLOCK = 128


def _sigmoid(x):
    return 0.5 * jnp.tanh(0.5 * x) + 0.5


def _silu(x):
    return x * _sigmoid(x)


def _softplus(x):
    return jnp.maximum(x, 0.0) + jnp.log(1.0 + jnp.exp(-jnp.abs(x)))


def _dot(a, b):
    return jnp.dot(a.astype(BF16), b.astype(BF16), preferred_element_type=F32)


def _dot_nt(a, b):
    return lax.dot_general(a.astype(BF16), b.astype(BF16), (((1,), (1,)), ((), ())),
                           preferred_element_type=F32)


def _dot_tn(a, b):
    return lax.dot_general(a.astype(BF16), b.astype(BF16), (((0,), (0,)), ((), ())),
                           preferred_element_type=F32)


def _cumsum_rows(x):
    rows = lax.broadcasted_iota(jnp.int32, x.shape, 0)
    d = 1
    while d < CHUNK:
        x = x + jnp.where(rows >= d, pltpu.roll(x, d, 0), 0.0)
        d *= 2
    return x


def _dot_halves(x, w_lo, w_hi):
    d = functools.partial(jnp.dot, preferred_element_type=F32)
    return jnp.concatenate([d(x[:, :MXU_TILE], w_lo), d(x[:, MXU_TILE:], w_hi)], axis=1)


def _seg_dot(x, seg, exact=False):
    hi = x.astype(BF16)
    out = _dot_halves(hi, seg, seg)
    if exact:
        out = out + _dot_halves((x - hi.astype(F32)).astype(BF16), seg, seg)
    return out


def _dot_bd(x, w_ref):
    return _dot_halves(x.astype(BF16), w_ref[0], w_ref[1])


HIST = 8


def _push_history(hist_ref, x):
    hist_ref[0:HIST, :] = hist_ref[CHUNK:CHUNK + HIST, :]
    hist_ref[HIST:HIST + CHUNK, :] = x


def _delayed(hist_ref, d):
    return hist_ref[HIST - d:HIST - d + CHUNK, :]


def _modulated_norm(x, gain, shift, scale):
    y = x * lax.rsqrt(jnp.mean(x * x, axis=-1, keepdims=True) + NORM_EPS) * gain
    return y * (1.0 + scale) + shift


def _mod_kernel(c_ref, w_ref, b_ref, o_ref):
    o_ref[...] = _dot(_silu(c_ref[...]), w_ref[...]) + b_ref[...]


def _mod_call(c, mod_w, mod_b):
    depth, d, n = mod_w.shape
    bsz = c.shape[0]
    tn = 1024
    return pl.pallas_call(
        _mod_kernel,
        grid=(depth, n // tn),
        in_specs=[pl.BlockSpec((bsz, d), lambda l, j: (0, 0)),
                  pl.BlockSpec((None, d, tn), lambda l, j: (l, 0, j)),
                  pl.BlockSpec((None, 1, tn), lambda l, j: (l, 0, j))],
        out_specs=pl.BlockSpec((None, bsz, tn), lambda l, j: (l, 0, j)),
        out_shape=jax.ShapeDtypeStruct((depth, bsz, n), F32),
        compiler_params=pltpu.CompilerParams(
            dimension_semantics=("arbitrary", "arbitrary"), vmem_limit_bytes=VMEM_LIMIT),
        name="adaln_mod",
    )(c, mod_w, mod_b.reshape(depth, 1, n))


def _ffn_kernel(x_ref, gain_ref, shift_ref, scale_ref, gate_ref, w1_ref, w3_ref, w2_ref, *rest,
                n_ff, final):
    if final:
        fgain_ref, o_ref, h_ref = rest
    else:
        o_ref, h_ref = rest
    j = pl.program_id(2)

    def row_blocks(fn):
        def body(k, carry):
            fn(pl.ds(pl.multiple_of(k * FFN_ROW_BLOCK, FFN_ROW_BLOCK), FFN_ROW_BLOCK))
            return carry
        lax.fori_loop(0, x_ref.shape[0] // FFN_ROW_BLOCK, body, 0)

    @pl.when(j == 0)
    def _():
        def norm(rows):
            h = _modulated_norm(x_ref[rows, :], gain_ref[...], shift_ref[...], scale_ref[...])
            h_ref[rows, :] = h.astype(BF16)
        row_blocks(norm)

    h = h_ref[...]
    a = jnp.dot(h, w1_ref[...], preferred_element_type=F32)
    b = jnp.dot(h, w3_ref[...], preferred_element_type=F32)
    g = (_silu(a) * b).astype(BF16)
    acc = jnp.where(j == 0, 0.0, o_ref[...])
    o_ref[...] = acc + jnp.dot(g, w2_ref[...], preferred_element_type=F32)

    @pl.when(j == n_ff - 1)
    def _():
        def residual(rows):
            xn = x_ref[rows, :] + (0.5 * (1.0 + gate_ref[...])) * o_ref[rows, :]
            if final:
                xn = xn * lax.rsqrt(jnp.mean(xn * xn, axis=-1, keepdims=True) + NORM_EPS) * fgain_ref[...]
            o_ref[rows, :] = xn
        row_blocks(residual)


def _ffn_call(x, gain, mod5, l, sub, s, w1, w3, w2, final_gain=None, tm=1024, tf=512):
    bsz, seqlen, d = x.shape
    tm = min(tm, seqlen)
    n_ff = w1.shape[-1] // tf
    final = final_gain is not None
    vec = lambda k: pl.BlockSpec((None, None, None, 1, d), lambda b, i, j: (l, b, 3 * sub + k, 0, 0))
    in_specs = [pl.BlockSpec((None, tm, d), lambda b, i, j: (b, i, 0)),
                pl.BlockSpec((1, d), lambda b, i, j: (0, 0)),
                vec(0), vec(1), vec(2),
                pl.BlockSpec((None, None, d, tf), lambda b, i, j: (l, s, 0, j)),
                pl.BlockSpec((None, None, d, tf), lambda b, i, j: (l, s, 0, j)),
                pl.BlockSpec((None, None, tf, d), lambda b, i, j: (l, s, j, 0))]
    args = [x, gain.reshape(1, d), mod5, mod5, mod5, w1, w3, w2]
    if final:
        in_specs.append(pl.BlockSpec((1, d), lambda b, i, j: (0, 0)))
        args.append(final_gain.reshape(1, d))
    return pl.pallas_call(
        functools.partial(_ffn_kernel, n_ff=n_ff, final=final),
        grid=(bsz, seqlen // tm, n_ff),
        in_specs=in_specs,
        out_specs=pl.BlockSpec((None, tm, d), lambda b, i, j: (b, i, 0)),
        out_shape=jax.ShapeDtypeStruct(x.shape, F32),
        scratch_shapes=[pltpu.VMEM((tm, d), BF16)],
        compiler_params=pltpu.CompilerParams(
            dimension_semantics=("parallel", "parallel", "arbitrary"), vmem_limit_bytes=FFN_VMEM_LIMIT),
        name="ffn_final" if final else "ffn",
    )(*args)


def _outproj_kernel(x_ref, y_ref, gate_ref, w_ref, o_ref):
    o_ref[...] = x_ref[...] + (1.0 + gate_ref[...]) * jnp.dot(
        y_ref[...], w_ref[...], preferred_element_type=F32)


def _outproj_call(x, y, mod5, l, sub, w, tm=512):
    bsz, seqlen, d = x.shape
    tm = min(tm, seqlen)
    return pl.pallas_call(
        _outproj_kernel,
        grid=(bsz, seqlen // tm),
        in_specs=[pl.BlockSpec((None, tm, d), lambda b, i: (b, i, 0)),
                  pl.BlockSpec((None, tm, y.shape[-1]), lambda b, i: (b, i, 0)),
                  pl.BlockSpec((None, None, None, 1, d), lambda b, i: (l, b, 3 * sub + 2, 0, 0)),
                  pl.BlockSpec((None,) + w.shape[1:], lambda b, i: (l, 0, 0))],
        out_specs=pl.BlockSpec((None, tm, d), lambda b, i: (b, i, 0)),
        out_shape=jax.ShapeDtypeStruct(x.shape, F32),
        compiler_params=pltpu.CompilerParams(
            dimension_semantics=("parallel", "parallel"), vmem_limit_bytes=VMEM_LIMIT),
        name="out_proj",
    )(x, y, mod5, w)


def _hgrn2_chunk(u_ref, lb, gnorm, st_ref, y_ref):
    gw = GROUP_W
    q = _silu(u_ref[:, U_HG:U_HG + gw])
    f_raw = u_ref[:, U_HG + gw:U_HG + 2 * gw]
    v = u_ref[:, U_HG + 2 * gw:U_HG + 3 * gw]
    g_raw = u_ref[:, U_HG + 3 * gw:U_HG + 4 * gw]
    e = jnp.exp(-f_raw)
    log_f = jnp.log(1.0 + lb * e) - jnp.log(1.0 + e)
    k = (1.0 - lb) * _sigmoid(-f_raw)
    yield
    bcum = _cumsum_rows(log_f)
    yield
    sub_row = lax.broadcasted_iota(jnp.int32, (SUB, 1), 0)
    for h in range(HG_HEADS):
        sl = slice(h * LANES, (h + 1) * LANES)
        qh, kh, vh, bh = q[:, sl], k[:, sl], v[:, sl], bcum[:, sl]
        st = st_ref[h]
        o_inter = _dot_nt(qh * jnp.exp(bh), st)
        pieces = []
        for i in range(CHUNK // SUB):
            r0 = i * SUB
            qd, kd, vd, bd = qh[r0:r0 + SUB], kh[r0:r0 + SUB], vh[r0:r0 + SUB], bh[r0:r0 + SUB]
            oi = jnp.zeros((SUB, LANES), F32)
            if i > 0:
                bref = bh[r0 - 1:r0]
                qi = qd * jnp.exp(bd - bref)
                kp = kh[0:r0] * jnp.exp(bref - bh[0:r0])
                oi = _dot(_dot_nt(qi, kp), vh[0:r0])
            for j in range(SUB):
                w = jnp.exp(jnp.minimum(bd - bd[j:j + 1], 0.0))
                col = jnp.sum(qd * w * kd[j:j + 1], axis=-1, keepdims=True)
                col = jnp.where(sub_row >= j, col, 0.0)
                oi = oi + col * vd[j:j + 1]
            pieces.append(oi)
            yield
        o = o_inter + jnp.concatenate(pieces, axis=0)
        b_last = bh[CHUNK - 1:CHUNK]
        st_ref[h] = st * jnp.exp(b_last) + _dot_tn(vh, kh * jnp.exp(b_last - bh))
        on = o * lax.rsqrt(jnp.mean(o * o, axis=-1, keepdims=True) + NORM_EPS) * gnorm[:, sl]
        y_ref[:, sl] = (on * _silu(g_raw[:, sl])).astype(y_ref.dtype)
        yield


def _rwkv7_chunk(u_ref, p, prev_ref, ht_ref, o_scr, y_ref):
    gw = GROUP_W
    u_raw = u_ref[:, U_RW:U_RW + RW_PAD]
    _push_history(prev_ref, u_raw)
    u = u_raw + (_delayed(prev_ref, 1) - u_raw) * p["mu"][...]
    yield
    r, k, v, lora = u[:, 0:gw], u[:, gw:2 * gw], u[:, 2 * gw:3 * gw], u[:, 3 * gw:RW_PAD]
    zw = p["w0"][...] + _dot(jnp.tanh(lora), p["w2"][...])
    lw = -jnp.exp(-_softplus(-zw) - 0.5)
    iclr = _sigmoid(p["a0"][...] + _dot(lora, p["a2"][...]))
    gate = _dot(_sigmoid(lora), p["g2"][...])
    yield
    kk = k * p["kk"][...]
    ss = _seg_dot(kk * kk, p["seg1"][...])
    kk = kk / jnp.maximum(jnp.sqrt(ss + 1e-12), 1e-6)
    k = k * (1.0 + (iclr - 1.0) * p["ka"][...])
    a_vec = -kk
    b_vec = kk * iclr
    yield

    lc = _cumsum_rows(lw)
    l_last = lc[CHUNK - 1:CHUNK]
    g_in = jnp.exp(lc)
    g_out = jnp.exp(-lc)
    g_end = jnp.exp(l_last - lc)
    a_s = a_vec * jnp.exp(lc - lw)
    r_s = r * g_in
    b_s = b_vec * g_out
    k_s = k * g_out
    b_e = b_vec * g_end
    k_e = k * g_end
    decay_end = jnp.exp(l_last)
    yield

    lane = lax.broadcasted_iota(jnp.int32, (1, LANES), 1)
    m_a = (lane < RW_HEAD).astype(F32)
    m_b = 1.0 - m_a
    stack = lambda t: jnp.concatenate([t * m_a, t * m_b], axis=0)
    n2 = 2 * CHUNK
    row = lax.broadcasted_iota(jnp.int32, (n2, n2), 0)
    col = lax.broadcasted_iota(jnp.int32, (n2, n2), 1)
    same = (row // CHUNK) == (col // CHUNK)
    strict = same & (row > col)
    incl = same & (row >= col)
    pairs = range(RW_HEADS // 2)
    sls = [slice(pr * LANES, (pr + 1) * LANES) for pr in pairs]
    bf = lambda t: t.astype(BF16)
    cat_k = lambda lhs, rhs: jnp.dot(jnp.concatenate(lhs, axis=1), jnp.concatenate(rhs, axis=0),
                                     preferred_element_type=F32)
    la = [stack(a_s[:, sl]) for sl in sls]
    lr = [stack(r_s[:, sl]) for sl in sls]
    v_st = [bf(stack(v[:, sl])) for sl in sls]
    hs = [ht_ref[pr] for pr in pairs]
    hs_b = [bf(m) for m in hs]
    sc = [_dot_nt(jnp.concatenate([la[pr], lr[pr]], axis=0),
                  jnp.concatenate([stack(b_s[:, sls[pr]]), stack(k_s[:, sls[pr]])], axis=0))
          for pr in pairs]
    yield
    pw = [jnp.where(strict, s[0:n2, 0:n2], 0.0) for s in sc]
    x = [cat_k([bf(jnp.where(strict, sc[pr][0:n2, n2:2 * n2], 0.0)), bf(la[pr])], [v_st[pr], hs_b[pr]])
         for pr in pairs]
    for step in range(6):
        x = [x[pr] + _dot(pw[pr], x[pr]) for pr in pairs]
        if step < 5:
            pw = [_dot(m, m) for m in pw]
        yield
    eye2 = row == col
    for pr in pairs:
        sl = sls[pr]
        a_rb = jnp.where(incl, sc[pr][n2:2 * n2, 0:n2], 0.0)
        a_rk = jnp.where(incl, sc[pr][n2:2 * n2, n2:2 * n2], 0.0)
        xb = bf(x[pr])
        o_st = cat_k([bf(lr[pr]), bf(a_rb), bf(a_rk)], [hs_b[pr], xb, v_st[pr]])
        o_scr[:, sl] = o_st[0:CHUNK] + o_st[CHUNK:n2]
        dec_col = jnp.sum(jnp.where(eye2, decay_end[:, sl], 0.0), axis=1, keepdims=True)
        ht_ref[pr] = hs[pr] * dec_col + lax.dot_general(
            jnp.concatenate([bf(stack(b_e[:, sl])), bf(stack(k_e[:, sl]))], axis=0),
            jnp.concatenate([xb, v_st[pr]], axis=0), (((0,), (0,)), ((), ())),
            preferred_element_type=F32)
        yield

    o = o_scr[...]
    mean = _seg_dot(o, p["segm"][...], exact=True)
    cen = o - mean
    var = _seg_dot(cen * cen, p["segm"][...])
    yield
    o = cen * lax.rsqrt(var + RW_LN_EPS) * p["ln_w"][...] + p["ln_b"][...]
    bonus = _seg_dot(r * k * p["rk"][...], p["seg1"][...]) * v
    y_ref[...] = ((o + bonus) * gate).astype(y_ref.dtype)


def _causal_conv4(x, hist_ref, w_ref, b_ref):
    _push_history(hist_ref, x)
    y = x * w_ref[3:4, :] + b_ref[...]
    for d in (1, 2, 3):
        y = y + _delayed(hist_ref, d) * w_ref[3 - d:4 - d, :]
    return y


def _mlstm_chunk(u_ref, p, prev_ref, m_ref, run_ref, y_ref):
    gw = GROUP_W
    xm = u_ref[:, U_ML:U_ML + gw]
    v = u_ref[:, U_ML + gw:U_ML + 2 * gw]
    o_raw = u_ref[:, U_ML + 2 * gw:U_ML + 3 * gw]
    gates = u_ref[:, U_ML + 3 * gw:U_ML + ML_PAD] + p["gate_b"][...]
    xc = _silu(_causal_conv4(xm, prev_ref, p["conv_w"], p["conv_b"]))
    q = _dot_bd(xc, p["wq"]) * (ML_DIM ** -0.5)
    k = _dot_bd(xc, p["wk"])
    yield
    log_f = jnp.minimum(gates, 0.0) - jnp.log(1.0 + jnp.exp(-jnp.abs(gates)))
    bcum = _cumsum_rows(log_f)
    row = lax.broadcasted_iota(jnp.int32, (CHUNK, CHUNK), 0)
    col = lax.broadcasted_iota(jnp.int32, (CHUNK, CHUNK), 1)
    causal = row >= col
    eye = row == col
    to_row = lambda c: jnp.sum(jnp.where(eye, c, 0.0), axis=0, keepdims=True)
    ones = jnp.ones((CHUNK, LANES), BF16)
    for h in range(ML_HEADS):
        sl = slice(h * LANES, (h + 1) * LANES)
        qh, kh, vh = q[:, sl], k[:, sl], v[:, sl]
        b_col = bcum[:, ML_HEADS + h:ML_HEADS + h + 1]
        i_col = gates[:, h:h + 1]
        m_prev = run_ref[h][0:1, 0:1]
        mem = m_ref[h]
        lw = b_col - to_row(b_col) + to_row(i_col)
        lprev = b_col + m_prev
        m_t = jnp.maximum(lprev, jnp.max(jnp.where(causal, lw, NEG_BIG), axis=-1, keepdims=True))
        wts = jnp.where(causal, jnp.exp(jnp.where(causal, lw - m_t, 0.0)), 0.0)
        s = _dot_nt(qh, kh) * wts
        wp = jnp.exp(lprev - m_t)
        yield
        v1 = jnp.concatenate([vh.astype(BF16), ones], axis=1)
        lhs = jnp.concatenate([(wp * qh).astype(BF16), s.astype(BF16)], axis=1)
        sv = jnp.dot(lhs, jnp.concatenate([mem.astype(BF16), v1], axis=0), preferred_element_type=F32)
        num = sv[:, 0:LANES]
        den = sv[:, LANES:LANES + 1]
        hh = num / jnp.maximum(jnp.abs(den), jnp.exp(-m_t))
        yield
        m_new = m_t[CHUNK - 1:CHUNK]
        b_end = b_col[CHUNK - 1:CHUNK]
        wl = jnp.exp(b_end - b_col + i_col - m_new)
        dec = jnp.exp(b_end + m_prev - m_new)
        kw = kh * wl
        m_ref[h] = dec * mem + lax.dot_general(kw.astype(BF16), v1, (((0,), (0,)), ((), ())),
                                               preferred_element_type=F32)
        run_ref[h] = jnp.broadcast_to(m_new, (8, LANES))
        hg = _sigmoid(o_raw[:, sl]) * hh
        hn = hg * lax.rsqrt(jnp.mean(hg * hg, axis=-1, keepdims=True) + NORM_EPS) * p["norm"][:, sl]
        y_ref[:, sl] = (hn + p["skip"][:, sl] * xc[:, sl]).astype(y_ref.dtype)
        yield


def _rglru_chunk(u_ref, p, prev_ref, h_ref, y_ref):
    gw = GROUP_W
    xb = u_ref[:, U_LRU:U_LRU + gw]
    gb = u_ref[:, U_LRU + gw:U_LRU + 2 * gw]
    xc = _causal_conv4(xb, prev_ref, p["conv_w"], p["conv_b"])
    r = _sigmoid(_dot_bd(xc, p["wa"]) + p["ba"][...])
    i = _sigmoid(_dot_bd(xc, p["wx"]) + p["bx"][...])
    yield
    log_a = -LRU_C * r * _softplus(-p["lam"][...])
    a = jnp.exp(log_a)
    bt = jnp.sqrt(1.0 - jnp.exp(2.0 * log_a)) * (i * xc)
    rows = lax.broadcasted_iota(jnp.int32, a.shape, 0)
    d = 1
    while d < CHUNK:
        keep = rows >= d
        bt = a * jnp.where(keep, pltpu.roll(bt, d, 0), 0.0) + bt
        a = a * jnp.where(keep, pltpu.roll(a, d, 0), 1.0)
        d *= 2
        yield
    hs = bt + a * h_ref[0:1, :]
    h_ref[...] = jnp.broadcast_to(hs[CHUNK - 1:CHUNK], h_ref.shape)
    gelu = 0.5 * gb * (1.0 + jnp.tanh(0.7978845608028654 * (gb + 0.044715 * (gb * gb * gb))))
    yl = hs * gelu
    y_ref[...] = (yl * lax.rsqrt(jnp.mean(yl * yl, axis=-1, keepdims=True) + NORM_EPS)
                  * p["norm"][...]).astype(y_ref.dtype)


def _round_robin(stages):
    live = [[g, 0, n] for g, n in stages]
    while live:
        entry = min(live, key=lambda e: (e[1] + 1) / e[2])
        try:
            next(entry[0])
            entry[1] += 1
        except StopIteration:
            live.remove(entry)


_HG_KEYS = ("lb", "gnorm")
_RW_KEYS = ("mu", "w0", "w2", "a0", "a2", "g2", "kk", "ka", "rk", "ln_w", "ln_b", "seg1", "segm")
_ML_KEYS = ("conv_w", "conv_b", "wq", "wk", "gate_b", "norm", "skip")
_LRU_KEYS = ("conv_w", "conv_b", "wa", "ba", "wx", "bx", "lam", "norm")


def _proj_stages(x_ref, t0, u_dst, gain, shift_ref, scale_ref, w_ref):
    rows = u_dst.shape[0]
    h = jnp.concatenate(
        [_modulated_norm(x_ref[r, t0:t0 + CHUNK, :], gain, shift_ref[r], scale_ref[r]).astype(BF16)
         for r in range(rows)], axis=0)
    yield
    for j in range(U_TOT // PROJ_TILE):
        cols = slice(j * PROJ_TILE, (j + 1) * PROJ_TILE)
        t = jnp.dot(h, w_ref[:, cols], preferred_element_type=F32)
        for r in range(rows):
            u_dst[r, :, cols] = t[r * CHUNK:(r + 1) * CHUNK]
        yield


def _mixer_kernel(*refs):
    it = iter(refs)
    x2_ref, xn_ref, gain_ref, shift_ref, scale_ref, w_ref = (next(it) for _ in range(6))
    hg = {k: next(it) for k in _HG_KEYS}
    rw = {k: next(it) for k in _RW_KEYS}
    ml = {k: next(it) for k in _ML_KEYS}
    lru = {k: next(it) for k in _LRU_KEYS}
    y_ref = next(it)
    (u_a, u_b, hg_st, rw_prev, rw_ht, rw_o, ml_prev, ml_m, ml_run, lru_prev, lru_h) = it
    proj = functools.partial(_proj_stages, gain=gain_ref[...], shift_ref=shift_ref,
                             scale_ref=scale_ref, w_ref=w_ref)

    @pl.when(pl.program_id(1) == 0)
    def _():
        for ref in (hg_st, rw_prev, rw_ht, ml_prev, ml_m, ml_run, lru_prev, lru_h):
            ref[...] = jnp.zeros_like(ref)
        for _ in proj(x2_ref, 0, u_a):
            pass

    gw = GROUP_W

    def mixers(u_buf, t0):
        stages = []
        for r in range(u_buf.shape[0]):
            u_r = u_buf.at[r]
            y_g = lambda g: y_ref.at[r, t0:t0 + CHUNK, g * gw:(g + 1) * gw]
            stages += [
                (_hgrn2_chunk(u_r, hg["lb"][...], hg["gnorm"][...], hg_st.at[r], y_g(0)),
                 2 + HG_HEADS * (CHUNK // SUB + 1)),
                (_rwkv7_chunk(u_r, rw, rw_prev.at[r], rw_ht.at[r], rw_o.at[r], y_g(1)),
                 12 + RW_HEADS // 2),
                (_mlstm_chunk(u_r, ml, ml_prev.at[r], ml_m.at[r], ml_run.at[r], y_g(2)),
                 1 + 3 * ML_HEADS),
                (_rglru_chunk(u_r, lru, lru_prev.at[r], lru_h.at[r], y_g(3)), 7)]
        return stages

    n_proj = 1 + U_TOT // PROJ_TILE
    _round_robin([(proj(x2_ref, CHUNK, u_b), n_proj)] + mixers(u_a, 0))
    _round_robin([(proj(xn_ref, 0, u_a), n_proj)] + mixers(u_b, CHUNK))


def _block_diag(blocks):
    n, a, b = blocks.shape
    tiled = jnp.tile(blocks.reshape(n * a, b), (1, n))
    same = (jnp.arange(n * a)[:, None] // a) == (jnp.arange(n * b)[None, :] // b)
    return jnp.where(same, tiled, jnp.zeros_like(tiled))


def _mixer_params(l, lower_bounds, hg_norm, rw_mu, rw_w0, rw_w2, rw_a0, rw_a2, rw_g2, rw_kk, rw_ka,
                  rw_rk, rw_ln_w, rw_ln_b, ml_conv_w, ml_conv_b, ml_wq, ml_wk, ml_i_b, ml_f_b,
                  ml_norm, ml_skip, lru_conv_w, lru_conv_b, lru_wa, lru_ba, lru_wx, lru_bx,
                  lru_lambda, lru_norm):
    gw = GROUP_W
    row = lambda a: a.reshape(1, -1).astype(F32)

    def lora_rows(w, start):
        z = jnp.zeros((LORA_PAD, gw), F32)
        return lax.dynamic_update_slice(z, w.astype(F32), (start, 0)).astype(BF16)

    head_id = jnp.arange(MXU_TILE) // RW_HEAD
    seg1 = (head_id[:, None] == head_id[None, :]).astype(F32)

    def diag_tiles(blocks):
        half = blocks.shape[0] // 2
        return jnp.stack([_block_diag(blocks[:half]), _block_diag(blocks[half:])]).astype(BF16)
    hg = dict(lb=row(lower_bounds[l]), gnorm=row(hg_norm[l]))
    rw = dict(mu=row(jnp.pad(rw_mu[l], (0, RW_PAD - RW_IN))), w0=row(rw_w0[l]),
              w2=lora_rows(rw_w2[l], 0), a0=row(rw_a0[l]),
              a2=lora_rows(rw_a2[l], RW_W_LORA), g2=lora_rows(rw_g2[l], RW_W_LORA + RW_A_LORA),
              kk=row(rw_kk[l]), ka=row(rw_ka[l]), rk=row(rw_rk[l]), ln_w=row(rw_ln_w[l]),
              ln_b=row(rw_ln_b[l]), seg1=seg1.astype(BF16), segm=(seg1 / RW_HEAD).astype(BF16))
    gate_b = jnp.concatenate([ml_i_b[l], ml_f_b[l], jnp.zeros((LANES - 2 * ML_HEADS,), F32)])
    ml = dict(conv_w=ml_conv_w[l].astype(F32), conv_b=row(ml_conv_b[l]),
              wq=diag_tiles(ml_wq[l]), wk=diag_tiles(ml_wk[l]),
              gate_b=row(gate_b), norm=row(ml_norm[l]), skip=row(ml_skip[l]))
    lru = dict(conv_w=lru_conv_w[l].astype(F32), conv_b=row(lru_conv_b[l]),
               wa=diag_tiles(lru_wa[l]), ba=row(lru_ba[l]),
               wx=diag_tiles(lru_wx[l]), bx=row(lru_bx[l]),
               lam=row(lru_lambda[l]), norm=row(lru_norm[l]))
    return ([hg[k] for k in _HG_KEYS] + [rw[k] for k in _RW_KEYS]
            + [ml[k] for k in _ML_KEYS] + [lru[k] for k in _LRU_KEYS])


def _mixer_call(x, gain, mod5, l, sub, w_in_p, params):
    bsz, seqlen, d = x.shape
    d_mix = 4 * GROUP_W
    rows = MIX_ROWS if bsz % MIX_ROWS == 0 else 1
    last_chunk = seqlen // CHUNK - 1
    once = pl.Buffered(1)
    full = lambda a: pl.BlockSpec(a.shape, lambda b, c: (0,) * a.ndim, pipeline_mode=once)
    vec = lambda k: pl.BlockSpec((None, rows, None, 1, d), lambda b, c: (l, b, 3 * sub + k, 0, 0))
    scratch = [
        pltpu.VMEM((rows, CHUNK, U_TOT), F32),
        pltpu.VMEM((rows, CHUNK, U_TOT), F32),
        pltpu.VMEM((rows, HG_HEADS, LANES, LANES), F32),
        pltpu.VMEM((rows, HIST + CHUNK, RW_PAD), F32),
        pltpu.VMEM((rows, RW_HEADS // 2, LANES, LANES), F32),
        pltpu.VMEM((rows, CHUNK, GROUP_W), F32),
        pltpu.VMEM((rows, HIST + CHUNK, GROUP_W), F32),
        pltpu.VMEM((rows, ML_HEADS, LANES, 2 * LANES), F32),
        pltpu.VMEM((rows, ML_HEADS, 8, LANES), F32),
        pltpu.VMEM((rows, HIST + CHUNK, GROUP_W), F32),
        pltpu.VMEM((rows, 8, GROUP_W), F32),
    ]
    return pl.pallas_call(
        _mixer_kernel,
        grid=(bsz // rows, seqlen // (2 * CHUNK)),
        in_specs=[pl.BlockSpec((rows, 2 * CHUNK, d), lambda b, c: (b, c, 0)),
                  pl.BlockSpec((rows, CHUNK, d), lambda b, c: (b, jnp.minimum(2 * c + 2, last_chunk), 0)),
                  pl.BlockSpec((1, d), lambda b, c: (0, 0)),
                  vec(0), vec(1),
                  pl.BlockSpec((None, d, U_TOT), lambda b, c: (l, 0, 0), pipeline_mode=once)]
        + [full(a) for a in params],
        out_specs=pl.BlockSpec((rows, 2 * CHUNK, d_mix), lambda b, c: (b, c, 0)),
        out_shape=jax.ShapeDtypeStruct((bsz, seqlen, d_mix), BF16),
        scratch_shapes=scratch,
        compiler_params=pltpu.CompilerParams(
            dimension_semantics=("parallel", "arbitrary"), vmem_limit_bytes=MIXER_VMEM_LIMIT),
        name="mixers",
    )(x, x, gain.reshape(1, d), mod5, mod5, w_in_p, *params)


def _pad_w_in(w):
    z = lambda n: jnp.zeros(w.shape[:-1] + (n,), BF16)
    c1, c2 = HG_IN + RW_IN, HG_IN + RW_IN + ML_IN
    wb = w.astype(BF16)
    return jnp.concatenate(
        [wb[..., :c1], z(RW_PAD - RW_IN), wb[..., c1:c2], z(ML_PAD - ML_IN), wb[..., c2:],
         z(U_TOT - U_LRU - LRU_IN)], axis=-1)


def kernel(x, c, norm_gain, mod_w, mod_b, ffn_w1, ffn_w3, ffn_w2, w_in, w_out, hg_lb_logits, hg_norm, rw_mu, rw_w0, rw_w2, rw_a0, rw_a2, rw_g2, rw_kk, rw_ka, rw_rk, rw_ln_w, rw_ln_b, ml_conv_w, ml_conv_b, ml_wq, ml_wk, ml_i_b, ml_f_b, ml_norm, ml_skip, lru_conv_w, lru_conv_b, lru_wa, lru_ba, lru_wx, lru_bx, lru_lambda, lru_norm, final_norm):
    bsz = x.shape[0]
    depth = mod_w.shape[0]
    lb_w = jax.nn.softmax(hg_lb_logits.astype(F32), axis=0)
    lower_bounds = jnp.cumsum(lb_w, axis=0) - lb_w[0]
    mod = _mod_call(c, mod_w, mod_b)
    mod5 = mod.reshape(depth, bsz, 3 * N_SUB, 1, D_MODEL)
    w1, w3, w2 = ffn_w1.astype(BF16), ffn_w3.astype(BF16), ffn_w2.astype(BF16)
    w_in_p, w_out_b = _pad_w_in(w_in), w_out.astype(BF16)
    for l in range(depth):
        x = _ffn_call(x, norm_gain[l, 0], mod5, l, 0, 0, w1, w3, w2)
        params = _mixer_params(
            l, lower_bounds, hg_norm, rw_mu, rw_w0, rw_w2, rw_a0, rw_a2, rw_g2, rw_kk, rw_ka, rw_rk,
            rw_ln_w, rw_ln_b, ml_conv_w, ml_conv_b, ml_wq, ml_wk, ml_i_b, ml_f_b, ml_norm, ml_skip,
            lru_conv_w, lru_conv_b, lru_wa, lru_ba, lru_wx, lru_bx, lru_lambda, lru_norm)
        y = _mixer_call(x, norm_gain[l, 1], mod5, l, 1, w_in_p, params)
        x = _outproj_call(x, y, mod5, l, 1, w_out_b)
        last = l == depth - 1
        x = _ffn_call(x, norm_gain[l, 2], mod5, l, 2, 1, w1, w3, w2,
                      final_gain=final_norm if last else None)
    return x
```

```python
import functools

import jax
import jax.numpy as jnp
from jax import lax
from jax.experimental import pallas as pl
from jax.experimental.pallas import tpu as pltpu

F32 = jnp.float32
BF16 = jnp.bfloat16

D_MODEL = 2048
DEPTH = 2
GROUP_W = 512
N_SUB = 3
D_FF = 5632
NORM_EPS = 1e-6
HG_HEADS = 4
RW_HEAD = 64
RW_HEADS = 8
RW_W_LORA = 32
RW_A_LORA = 32
RW_G_LORA = 96
RW_LN_EPS = 64e-5
RW_IN = 3 * GROUP_W + RW_W_LORA + RW_A_LORA + RW_G_LORA
ML_HEADS = 4
ML_DIM = 128
ML_IN = 3 * GROUP_W + 2 * ML_HEADS
NEG_BIG = -1e30
LRU_C = 8.0
HG_IN = 4 * GROUP_W
LRU_IN = 2 * GROUP_W

LANES = 128
MXU_TILE = 256
CHUNK = 64
MIX_ROWS = 2
SUB = 16
HIST = 8
RW_PAD = 1792
ML_PAD = 1664
U_HG = 0
U_RW = U_HG + HG_IN
U_ML = U_RW + RW_PAD
U_LRU = U_ML + ML_PAD
U_TOT = U_LRU + LRU_IN
LORA_PAD = RW_PAD - 3 * GROUP_W
VMEM_LIMIT = 56 * 1024 * 1024
MIXER_VMEM_LIMIT = 60 * 1024 * 1024
PROJ_TILE = 512
FFN_VMEM_LIMIT = 62 * 1024 * 1024
FFN_ROW_BLOCK = 128


def _sigmoid(x):
    return 0.5 * jnp.tanh(0.5 * x) + 0.5


def _silu(x):
    return x * _sigmoid(x)


def _softplus(x):
    return jnp.maximum(x, 0.0) + jnp.log(1.0 + jnp.exp(-jnp.abs(x)))


def _dot(a, b):
    return jnp.dot(a.astype(BF16), b.astype(BF16), preferred_element_type=F32)


def _dot_nt(a, b):
    return lax.dot_general(a.astype(BF16), b.astype(BF16), (((1,), (1,)), ((), ())),
                           preferred_element_type=F32)


def _dot_tn(a, b):
    return lax.dot_general(a.astype(BF16), b.astype(BF16), (((0,), (0,)), ((), ())),
                           preferred_element_type=F32)


def _cumsum_rows(x):
    rows = lax.broadcasted_iota(jnp.int32, x.shape, 0)
    d = 1
    while d < CHUNK:
        x = x + jnp.where(rows >= d, pltpu.roll(x, d, 0), 0.0)
        d *= 2
    return x


def _dot_halves(x, w_lo, w_hi):
    d = functools.partial(jnp.dot, preferred_element_type=F32)
    return jnp.concatenate([d(x[:, :MXU_TILE], w_lo), d(x[:, MXU_TILE:], w_hi)], axis=1)


def _seg_dot(x, seg, exact=False):
    hi = x.astype(BF16)
    out = _dot_halves(hi, seg, seg)
    if exact:
        out = out + _dot_halves((x - hi.astype(F32)).astype(BF16), seg, seg)
    return out


def _dot_bd(x, w_ref):
    return _dot_halves(x.astype(BF16), w_ref[0], w_ref[1])


def _push_history(hist_ref, x):
    hist_ref[0:HIST, :] = hist_ref[CHUNK:CHUNK + HIST, :]
    hist_ref[HIST:HIST + CHUNK, :] = x


def _delayed(hist_ref, d):
    return hist_ref[HIST - d:HIST - d + CHUNK, :]


def _modulated_norm(x, gain, shift, scale):
    y = x * lax.rsqrt(jnp.mean(x * x, axis=-1, keepdims=True) + NORM_EPS) * gain
    return y * (1.0 + scale) + shift


def _mod_kernel(c_ref, w_ref, b_ref, o_ref):
    o_ref[...] = _dot(_silu(c_ref[...]), w_ref[...]) + b_ref[...]


def _mod_call(c, mod_w, mod_b):
    depth, d, n = mod_w.shape
    bsz = c.shape[0]
    tn = 1024
    return pl.pallas_call(
        _mod_kernel,
        grid=(depth, n // tn),
        in_specs=[pl.BlockSpec((bsz, d), lambda l, j: (0, 0)),
                  pl.BlockSpec((None, d, tn), lambda l, j: (l, 0, j)),
                  pl.BlockSpec((None, 1, tn), lambda l, j: (l, 0, j))],
        out_specs=pl.BlockSpec((None, bsz, tn), lambda l, j: (l, 0, j)),
        out_shape=jax.ShapeDtypeStruct((depth, bsz, n), F32),
        compiler_params=pltpu.CompilerParams(
            dimension_semantics=("arbitrary", "arbitrary"), vmem_limit_bytes=VMEM_LIMIT),
        name="adaln_mod",
    )(c, mod_w, mod_b.reshape(depth, 1, n))


def _ffn_kernel(x_ref, gain_ref, shift_ref, scale_ref, gate_ref, w1_ref, w3_ref, w2_ref, *rest,
                n_ff, final):
    if final:
        fgain_ref, o_ref, h_ref = rest
    else:
        o_ref, h_ref = rest
    j = pl.program_id(2)

    def row_blocks(fn):
        def body(k, carry):
            fn(pl.ds(pl.multiple_of(k * FFN_ROW_BLOCK, FFN_ROW_BLOCK), FFN_ROW_BLOCK))
            return carry
        lax.fori_loop(0, x_ref.shape[0] // FFN_ROW_BLOCK, body, 0)

    @pl.when(j == 0)
    def _():
        def norm(rows):
            h = _modulated_norm(x_ref[rows, :], gain_ref[...], shift_ref[...], scale_ref[...])
            h_ref[rows, :] = h.astype(BF16)
        row_blocks(norm)

    h = h_ref[...]
    a = jnp.dot(h, w1_ref[...], preferred_element_type=F32)
    b = jnp.dot(h, w3_ref[...], preferred_element_type=F32)
    g = (_silu(a) * b).astype(BF16)
    acc = jnp.where(j == 0, 0.0, o_ref[...])
    o_ref[...] = acc + jnp.dot(g, w2_ref[...], preferred_element_type=F32)

    @pl.when(j == n_ff - 1)
    def _():
        def residual(rows):
            xn = x_ref[rows, :] + (0.5 * (1.0 + gate_ref[...])) * o_ref[rows, :]
            if final:
                xn = xn * lax.rsqrt(jnp.mean(xn * xn, axis=-1, keepdims=True) + NORM_EPS) * fgain_ref[...]
            o_ref[rows, :] = xn
        row_blocks(residual)


def _ffn_call(x, gain, mod5, l, sub, s, w1, w3, w2, final_gain=None, tm=1024, tf=512):
    bsz, seqlen, d = x.shape
    tm = min(tm, seqlen)
    n_ff = w1.shape[-1] // tf
    final = final_gain is not None
    vec = lambda k: pl.BlockSpec((None, None, None, 1, d), lambda b, i, j: (l, b, 3 * sub + k, 0, 0))
    in_specs = [pl.BlockSpec((None, tm, d), lambda b, i, j: (b, i, 0)),
                pl.BlockSpec((1, d), lambda b, i, j: (0, 0)),
                vec(0), vec(1), vec(2),
                pl.BlockSpec((None, None, d, tf), lambda b, i, j: (l, s, 0, j)),
                pl.BlockSpec((None, None, d, tf), lambda b, i, j: (l, s, 0, j)),
                pl.BlockSpec((None, None, tf, d), lambda b, i, j: (l, s, j, 0))]
    args = [x, gain.reshape(1, d), mod5, mod5, mod5, w1, w3, w2]
    if final:
        in_specs.append(pl.BlockSpec((1, d), lambda b, i, j: (0, 0)))
        args.append(final_gain.reshape(1, d))
    return pl.pallas_call(
        functools.partial(_ffn_kernel, n_ff=n_ff, final=final),
        grid=(bsz, seqlen // tm, n_ff),
        in_specs=in_specs,
        out_specs=pl.BlockSpec((None, tm, d), lambda b, i, j: (b, i, 0)),
        out_shape=jax.ShapeDtypeStruct(x.shape, F32),
        scratch_shapes=[pltpu.VMEM((tm, d), BF16)],
        compiler_params=pltpu.CompilerParams(
            dimension_semantics=("parallel", "parallel", "arbitrary"), vmem_limit_bytes=FFN_VMEM_LIMIT),
        name="ffn_final" if final else "ffn",
    )(*args)


def _outproj_kernel(x_ref, y_ref, gate_ref, w_ref, o_ref):
    o_ref[...] = x_ref[...] + (1.0 + gate_ref[...]) * jnp.dot(
        y_ref[...], w_ref[...], preferred_element_type=F32)


def _outproj_call(x, y, mod5, l, sub, w, tm=512):
    bsz, seqlen, d = x.shape
    tm = min(tm, seqlen)
    return pl.pallas_call(
        _outproj_kernel,
        grid=(bsz, seqlen // tm),
        in_specs=[pl.BlockSpec((None, tm, d), lambda b, i: (b, i, 0)),
                  pl.BlockSpec((None, tm, y.shape[-1]), lambda b, i: (b, i, 0)),
                  pl.BlockSpec((None, None, None, 1, d), lambda b, i: (l, b, 3 * sub + 2, 0, 0)),
                  pl.BlockSpec((None,) + w.shape[1:], lambda b, i: (l, 0, 0))],
        out_specs=pl.BlockSpec((None, tm, d), lambda b, i: (b, i, 0)),
        out_shape=jax.ShapeDtypeStruct(x.shape, F32),
        compiler_params=pltpu.CompilerParams(
            dimension_semantics=("parallel", "parallel"), vmem_limit_bytes=VMEM_LIMIT),
        name="out_proj",
    )(x, y, mod5, w)


def _hgrn2_chunk(u_ref, lb, gnorm, st_ref, y_ref):
    gw = GROUP_W
    q = _silu(u_ref[:, U_HG:U_HG + gw])
    f_raw = u_ref[:, U_HG + gw:U_HG + 2 * gw]
    v = u_ref[:, U_HG + 2 * gw:U_HG + 3 * gw]
    g_raw = u_ref[:, U_HG + 3 * gw:U_HG + 4 * gw]
    e = jnp.exp(-f_raw)
    log_f = jnp.log(1.0 + lb * e) - jnp.log(1.0 + e)
    k = (1.0 - lb) * _sigmoid(-f_raw)
    yield
    bcum = _cumsum_rows(log_f)
    yield
    sub_row = lax.broadcasted_iota(jnp.int32, (SUB, 1), 0)
    for h in range(HG_HEADS):
        sl = slice(h * LANES, (h + 1) * LANES)
        qh, kh, vh, bh = q[:, sl], k[:, sl], v[:, sl], bcum[:, sl]
        st = st_ref[h]
        o_inter = _dot_nt(qh * jnp.exp(bh), st)
        pieces = []
        for i in range(CHUNK // SUB):
            r0 = i * SUB
            qd, kd, vd, bd = qh[r0:r0 + SUB], kh[r0:r0 + SUB], vh[r0:r0 + SUB], bh[r0:r0 + SUB]
            oi = jnp.zeros((SUB, LANES), F32)
            if i > 0:
                bref = bh[r0 - 1:r0]
                qi = qd * jnp.exp(bd - bref)
                kp = kh[0:r0] * jnp.exp(bref - bh[0:r0])
                oi = _dot(_dot_nt(qi, kp), vh[0:r0])
            for j in range(SUB):
                w = jnp.exp(jnp.minimum(bd - bd[j:j + 1], 0.0))
                col = jnp.sum(qd * w * kd[j:j + 1], axis=-1, keepdims=True)
                col = jnp.where(sub_row >= j, col, 0.0)
                oi = oi + col * vd[j:j + 1]
            pieces.append(oi)
            yield
        o = o_inter + jnp.concatenate(pieces, axis=0)
        b_last = bh[CHUNK - 1:CHUNK]
        st_ref[h] = st * jnp.exp(b_last) + _dot_tn(vh, kh * jnp.exp(b_last - bh))
        on = o * lax.rsqrt(jnp.mean(o * o, axis=-1, keepdims=True) + NORM_EPS) * gnorm[:, sl]
        y_ref[:, sl] = (on * _silu(g_raw[:, sl])).astype(y_ref.dtype)
        yield


def _rwkv7_chunk(u_ref, p, prev_ref, ht_ref, o_scr, y_ref):
    gw = GROUP_W
    u_raw = u_ref[:, U_RW:U_RW + RW_PAD]
    _push_history(prev_ref, u_raw)
    u = u_raw + (_delayed(prev_ref, 1) - u_raw) * p["mu"][...]
    yield
    r, k, v, lora = u[:, 0:gw], u[:, gw:2 * gw], u[:, 2 * gw:3 * gw], u[:, 3 * gw:RW_PAD]
    zw = p["w0"][...] + _dot(jnp.tanh(lora), p["w2"][...])
    lw = -jnp.exp(-_softplus(-zw) - 0.5)
    iclr = _sigmoid(p["a0"][...] + _dot(lora, p["a2"][...]))
    gate = _dot(_sigmoid(lora), p["g2"][...])
    yield
    kk = k * p["kk"][...]
    ss = _seg_dot(kk * kk, p["seg1"][...])
    kk = kk / jnp.maximum(jnp.sqrt(ss + 1e-12), 1e-6)
    k = k * (1.0 + (iclr - 1.0) * p["ka"][...])
    a_vec = -kk
    b_vec = kk * iclr
    yield

    lc = _cumsum_rows(lw)
    l_last = lc[CHUNK - 1:CHUNK]
    g_in = jnp.exp(lc)
    g_out = jnp.exp(-lc)
    g_end = jnp.exp(l_last - lc)
    a_s = a_vec * jnp.exp(lc - lw)
    r_s = r * g_in
    b_s = b_vec * g_out
    k_s = k * g_out
    b_e = b_vec * g_end
    k_e = k * g_end
    decay_end = jnp.exp(l_last)
    yield

    lane = lax.broadcasted_iota(jnp.int32, (1, LANES), 1)
    m_a = (lane < RW_HEAD).astype(F32)
    m_b = 1.0 - m_a
    stack = lambda t: jnp.concatenate([t * m_a, t * m_b], axis=0)
    n2 = 2 * CHUNK
    row = lax.broadcasted_iota(jnp.int32, (n2, n2), 0)
    col = lax.broadcasted_iota(jnp.int32, (n2, n2), 1)
    same = (row // CHUNK) == (col // CHUNK)
    strict = same & (row > col)
    incl = same & (row >= col)
    pairs = range(RW_HEADS // 2)
    sls = [slice(pr * LANES, (pr + 1) * LANES) for pr in pairs]
    bf = lambda t: t.astype(BF16)
    cat_k = lambda lhs, rhs: jnp.dot(jnp.concatenate(lhs, axis=1), jnp.concatenate(rhs, axis=0),
                                     preferred_element_type=F32)
    la = [stack(a_s[:, sl]) for sl in sls]
    lr = [stack(r_s[:, sl]) for sl in sls]
    v_st = [bf(stack(v[:, sl])) for sl in sls]
    hs = [ht_ref[pr] for pr in pairs]
    hs_b = [bf(m) for m in hs]
    sc = [_dot_nt(jnp.concatenate([la[pr], lr[pr]], axis=0),
                  jnp.concatenate([stack(b_s[:, sls[pr]]), stack(k_s[:, sls[pr]])], axis=0))
          for pr in pairs]
    yield
    pw = [jnp.where(strict, s[0:n2, 0:n2], 0.0) for s in sc]
    x = [cat_k([bf(jnp.where(strict, sc[pr][0:n2, n2:2 * n2], 0.0)), bf(la[pr])], [v_st[pr], hs_b[pr]])
         for pr in pairs]
    for step in range(6):
        x = [x[pr] + _dot(pw[pr], x[pr]) for pr in pairs]
        if step < 5:
            pw = [_dot(m, m) for m in pw]
        yield
    eye2 = row == col
    for pr in pairs:
        sl = sls[pr]
        a_rb = jnp.where(incl, sc[pr][n2:2 * n2, 0:n2], 0.0)
        a_rk = jnp.where(incl, sc[pr][n2:2 * n2, n2:2 * n2], 0.0)
        xb = bf(x[pr])
        o_st = cat_k([bf(lr[pr]), bf(a_rb), bf(a_rk)], [hs_b[pr], xb, v_st[pr]])
        o_scr[:, sl] = o_st[0:CHUNK] + o_st[CHUNK:n2]
        dec_col = jnp.sum(jnp.where(eye2, decay_end[:, sl], 0.0), axis=1, keepdims=True)
        ht_ref[pr] = hs[pr] * dec_col + lax.dot_general(
            jnp.concatenate([bf(stack(b_e[:, sl])), bf(stack(k_e[:, sl]))], axis=0),
            jnp.concatenate([xb, v_st[pr]], axis=0), (((0,), (0,)), ((), ())),
            preferred_element_type=F32)
        yield

    o = o_scr[...]
    mean = _seg_dot(o, p["segm"][...], exact=True)
    cen = o - mean
    var = _seg_dot(cen * cen, p["segm"][...])
    yield
    o = cen * lax.rsqrt(var + RW_LN_EPS) * p["ln_w"][...] + p["ln_b"][...]
    bonus = _seg_dot(r * k * p["rk"][...], p["seg1"][...]) * v
    y_ref[...] = ((o + bonus) * gate).astype(y_ref.dtype)


def _causal_conv4(x, hist_ref, w_ref, b_ref):
    _push_history(hist_ref, x)
    y = x * w_ref[3:4, :] + b_ref[...]
    for d in (1, 2, 3):
        y = y + _delayed(hist_ref, d) * w_ref[3 - d:4 - d, :]
    return y


def _mlstm_chunk(u_ref, p, prev_ref, m_ref, run_ref, y_ref):
    gw = GROUP_W
    xm = u_ref[:, U_ML:U_ML + gw]
    v = u_ref[:, U_ML + gw:U_ML + 2 * gw]
    o_raw = u_ref[:, U_ML + 2 * gw:U_ML + 3 * gw]
    gates = u_ref[:, U_ML + 3 * gw:U_ML + ML_PAD] + p["gate_b"][...]
    xc = _silu(_causal_conv4(xm, prev_ref, p["conv_w"], p["conv_b"]))
    q = _dot_bd(xc, p["wq"]) * (ML_DIM ** -0.5)
    k = _dot_bd(xc, p["wk"])
    yield
    log_f = jnp.minimum(gates, 0.0) - jnp.log(1.0 + jnp.exp(-jnp.abs(gates)))
    bcum = _cumsum_rows(log_f)
    row = lax.broadcasted_iota(jnp.int32, (CHUNK, CHUNK), 0)
    col = lax.broadcasted_iota(jnp.int32, (CHUNK, CHUNK), 1)
    causal = row >= col
    eye = row == col
    to_row = lambda c: jnp.sum(jnp.where(eye, c, 0.0), axis=0, keepdims=True)
    ones = jnp.ones((CHUNK, LANES), BF16)
    for h in range(ML_HEADS):
        sl = slice(h * LANES, (h + 1) * LANES)
        qh, kh, vh = q[:, sl], k[:, sl], v[:, sl]
        b_col = bcum[:, ML_HEADS + h:ML_HEADS + h + 1]
        i_col = gates[:, h:h + 1]
        m_prev = run_ref[h][0:1, 0:1]
        mem = m_ref[h]
        lw = b_col - to_row(b_col) + to_row(i_col)
        lprev = b_col + m_prev
        m_t = jnp.maximum(lprev, jnp.max(jnp.where(causal, lw, NEG_BIG), axis=-1, keepdims=True))
        wts = jnp.where(causal, jnp.exp(jnp.where(causal, lw - m_t, 0.0)), 0.0)
        s = _dot_nt(qh, kh) * wts
        wp = jnp.exp(lprev - m_t)
        yield
        v1 = jnp.concatenate([vh.astype(BF16), ones], axis=1)
        lhs = jnp.concatenate([(wp * qh).astype(BF16), s.astype(BF16)], axis=1)
        sv = jnp.dot(lhs, jnp.concatenate([mem.astype(BF16), v1], axis=0), preferred_element_type=F32)
        num = sv[:, 0:LANES]
        den = sv[:, LANES:LANES + 1]
        hh = num / jnp.maximum(jnp.abs(den), jnp.exp(-m_t))
        yield
        m_new = m_t[CHUNK - 1:CHUNK]
        b_end = b_col[CHUNK - 1:CHUNK]
        wl = jnp.exp(b_end - b_col + i_col - m_new)
        dec = jnp.exp(b_end + m_prev - m_new)
        kw = kh * wl
        m_ref[h] = dec * mem + lax.dot_general(kw.astype(BF16), v1, (((0,), (0,)), ((), ())),
                                               preferred_element_type=F32)
        run_ref[h] = jnp.broadcast_to(m_new, (8, LANES))
        hg = _sigmoid(o_raw[:, sl]) * hh
        hn = hg * lax.rsqrt(jnp.mean(hg * hg, axis=-1, keepdims=True) + NORM_EPS) * p["norm"][:, sl]
        y_ref[:, sl] = (hn + p["skip"][:, sl] * xc[:, sl]).astype(y_ref.dtype)
        yield


def _rglru_chunk(u_ref, p, prev_ref, h_ref, y_ref):
    gw = GROUP_W
    xb = u_ref[:, U_LRU:U_LRU + gw]
    gb = u_ref[:, U_LRU + gw:U_LRU + 2 * gw]
    xc = _causal_conv4(xb, prev_ref, p["conv_w"], p["conv_b"])
    r = _sigmoid(_dot_bd(xc, p["wa"]) + p["ba"][...])
    i = _sigmoid(_dot_bd(xc, p["wx"]) + p["bx"][...])
    yield
    log_a = -LRU_C * r * _softplus(-p["lam"][...])
    a = jnp.exp(log_a)
    bt = jnp.sqrt(1.0 - jnp.exp(2.0 * log_a)) * (i * xc)
    rows = lax.broadcasted_iota(jnp.int32, a.shape, 0)
    d = 1
    while d < CHUNK:
        keep = rows >= d
        bt = a * jnp.where(keep, pltpu.roll(bt, d, 0), 0.0) + bt
        a = a * jnp.where(keep, pltpu.roll(a, d, 0), 1.0)
        d *= 2
        yield
    hs = bt + a * h_ref[0:1, :]
    h_ref[...] = jnp.broadcast_to(hs[CHUNK - 1:CHUNK], h_ref.shape)
    gelu = 0.5 * gb * (1.0 + jnp.tanh(0.7978845608028654 * (gb + 0.044715 * (gb * gb * gb))))
    yl = hs * gelu
    y_ref[...] = (yl * lax.rsqrt(jnp.mean(yl * yl, axis=-1, keepdims=True) + NORM_EPS)
                  * p["norm"][...]).astype(y_ref.dtype)


def _round_robin(stages):
    live = [[g, 0, n] for g, n in stages]
    while live:
        entry = min(live, key=lambda e: (e[1] + 1) / e[2])
        try:
            next(entry[0])
            entry[1] += 1
        except StopIteration:
            live.remove(entry)


_HG_KEYS = ("lb", "gnorm")
_RW_KEYS = ("mu", "w0", "w2", "a0", "a2", "g2", "kk", "ka", "rk", "ln_w", "ln_b", "seg1", "segm")
_ML_KEYS = ("conv_w", "conv_b", "wq", "wk", "gate_b", "norm", "skip")
_LRU_KEYS = ("conv_w", "conv_b", "wa", "ba", "wx", "bx", "lam", "norm")


def _proj_stages(x_ref, t0, u_dst, gain, shift_ref, scale_ref, w_ref):
    rows = u_dst.shape[0]
    h = jnp.concatenate(
        [_modulated_norm(x_ref[r, t0:t0 + CHUNK, :], gain, shift_ref[r], scale_ref[r]).astype(BF16)
         for r in range(rows)], axis=0)
    yield
    for c0 in range(0, U_TOT, PROJ_TILE):
        cols = slice(c0, min(c0 + PROJ_TILE, U_TOT))
        t = jnp.dot(h, w_ref[:, cols], preferred_element_type=F32)
        for r in range(rows):
            u_dst[r, :, cols] = t[r * CHUNK:(r + 1) * CHUNK]
        yield


def _mixer_kernel(*refs):
    it = iter(refs)
    x2_ref, xn_ref, gain_ref, shift_ref, scale_ref, w_ref = (next(it) for _ in range(6))
    hg = {k: next(it) for k in _HG_KEYS}
    rw = {k: next(it) for k in _RW_KEYS}
    ml = {k: next(it) for k in _ML_KEYS}
    lru = {k: next(it) for k in _LRU_KEYS}
    y_ref = next(it)
    (u_a, u_b, hg_st, rw_prev, rw_ht, rw_o, ml_prev, ml_m, ml_run, lru_prev, lru_h) = it
    proj = functools.partial(_proj_stages, gain=gain_ref[...], shift_ref=shift_ref,
                             scale_ref=scale_ref, w_ref=w_ref)

    @pl.when(pl.program_id(1) == 0)
    def _():
        for ref in (hg_st, rw_prev, rw_ht, ml_prev, ml_m, ml_run, lru_prev, lru_h):
            ref[...] = jnp.zeros_like(ref)
        for _ in proj(x2_ref, 0, u_a):
            pass

    gw = GROUP_W

    def mixers(u_buf, t0):
        stages = []
        for r in range(u_buf.shape[0]):
            u_r = u_buf.at[r]
            y_g = lambda g: y_ref.at[r, t0:t0 + CHUNK, g * gw:(g + 1) * gw]
            stages += [
                (_hgrn2_chunk(u_r, hg["lb"][...], hg["gnorm"][...], hg_st.at[r], y_g(0)),
                 2 + HG_HEADS * (CHUNK // SUB + 1)),
                (_rwkv7_chunk(u_r, rw, rw_prev.at[r], rw_ht.at[r], rw_o.at[r], y_g(1)),
                 12 + RW_HEADS // 2),
                (_mlstm_chunk(u_r, ml, ml_prev.at[r], ml_m.at[r], ml_run.at[r], y_g(2)),
                 1 + 3 * ML_HEADS),
                (_rglru_chunk(u_r, lru, lru_prev.at[r], lru_h.at[r], y_g(3)), 7)]
        return stages

    n_proj = 1 + pl.cdiv(U_TOT, PROJ_TILE)
    _round_robin([(proj(x2_ref, CHUNK, u_b), n_proj)] + mixers(u_a, 0))
    _round_robin([(proj(xn_ref, 0, u_a), n_proj)] + mixers(u_b, CHUNK))


def _block_diag(blocks):
    n, a, b = blocks.shape
    tiled = jnp.tile(blocks.reshape(n * a, b), (1, n))
    same = (jnp.arange(n * a)[:, None] // a) == (jnp.arange(n * b)[None, :] // b)
    return jnp.where(same, tiled, jnp.zeros_like(tiled))


def _mixer_params(l, lower_bounds, hg_norm, rw_mu, rw_w0, rw_w2, rw_a0, rw_a2, rw_g2, rw_kk, rw_ka,
                  rw_rk, rw_ln_w, rw_ln_b, ml_conv_w, ml_conv_b, ml_wq, ml_wk, ml_i_b, ml_f_b,
                  ml_norm, ml_skip, lru_conv_w, lru_conv_b, lru_wa, lru_ba, lru_wx, lru_bx,
                  lru_lambda, lru_norm):
    gw = GROUP_W
    row = lambda a: a.reshape(1, -1).astype(F32)

    def lora_rows(w, start):
        z = jnp.zeros((LORA_PAD, gw), F32)
        return lax.dynamic_update_slice(z, w.astype(F32), (start, 0)).astype(BF16)

    head_id = jnp.arange(MXU_TILE) // RW_HEAD
    seg1 = (head_id[:, None] == head_id[None, :]).astype(F32)

    def diag_tiles(blocks):
        half = blocks.shape[0] // 2
        return jnp.stack([_block_diag(blocks[:half]), _block_diag(blocks[half:])]).astype(BF16)
    hg = dict(lb=row(lower_bounds[l]), gnorm=row(hg_norm[l]))
    rw = dict(mu=row(jnp.pad(rw_mu[l], (0, RW_PAD - RW_IN))), w0=row(rw_w0[l]),
              w2=lora_rows(rw_w2[l], 0), a0=row(rw_a0[l]),
              a2=lora_rows(rw_a2[l], RW_W_LORA), g2=lora_rows(rw_g2[l], RW_W_LORA + RW_A_LORA),
              kk=row(rw_kk[l]), ka=row(rw_ka[l]), rk=row(rw_rk[l]), ln_w=row(rw_ln_w[l]),
              ln_b=row(rw_ln_b[l]), seg1=seg1.astype(BF16), segm=(seg1 / RW_HEAD).astype(BF16))
    gate_b = jnp.concatenate([ml_i_b[l], ml_f_b[l], jnp.zeros((LANES - 2 * ML_HEADS,), F32)])
    ml = dict(conv_w=ml_conv_w[l].astype(F32), conv_b=row(ml_conv_b[l]),
              wq=diag_tiles(ml_wq[l]), wk=diag_tiles(ml_wk[l]),
              gate_b=row(gate_b), norm=row(ml_norm[l]), skip=row(ml_skip[l]))
    lru = dict(conv_w=lru_conv_w[l].astype(F32), conv_b=row(lru_conv_b[l]),
               wa=diag_tiles(lru_wa[l]), ba=row(lru_ba[l]),
               wx=diag_tiles(lru_wx[l]), bx=row(lru_bx[l]),
               lam=row(lru_lambda[l]), norm=row(lru_norm[l]))
    return ([hg[k] for k in _HG_KEYS] + [rw[k] for k in _RW_KEYS]
            + [ml[k] for k in _ML_KEYS] + [lru[k] for k in _LRU_KEYS])


def _mixer_call(x, gain, mod5, l, sub, w_in_p, params):
    bsz, seqlen, d = x.shape
    d_mix = 4 * GROUP_W
    rows = MIX_ROWS if bsz % MIX_ROWS == 0 else 1
    last_chunk = seqlen // CHUNK - 1
    once = pl.Buffered(1)
    full = lambda a: pl.BlockSpec(a.shape, lambda b, c: (0,) * a.ndim, pipeline_mode=once)
    vec = lambda k: pl.BlockSpec((None, rows, None, 1, d), lambda b, c: (l, b, 3 * sub + k, 0, 0))
    scratch = [
        pltpu.VMEM((rows, CHUNK, U_TOT), F32),
        pltpu.VMEM((rows, CHUNK, U_TOT), F32),
        pltpu.VMEM((rows, HG_HEADS, LANES, LANES), F32),
        pltpu.VMEM((rows, HIST + CHUNK, RW_PAD), F32),
        pltpu.VMEM((rows, RW_HEADS // 2, LANES, LANES), F32),
        pltpu.VMEM((rows, CHUNK, GROUP_W), F32),
        pltpu.VMEM((rows, HIST + CHUNK, GROUP_W), F32),
        pltpu.VMEM((rows, ML_HEADS, LANES, 2 * LANES), F32),
        pltpu.VMEM((rows, ML_HEADS, 8, LANES), F32),
        pltpu.VMEM((rows, HIST + CHUNK, GROUP_W), F32),
        pltpu.VMEM((rows, 8, GROUP_W), F32),
    ]
    return pl.pallas_call(
        _mixer_kernel,
        grid=(bsz // rows, seqlen // (2 * CHUNK)),
        in_specs=[pl.BlockSpec((rows, 2 * CHUNK, d), lambda b, c: (b, c, 0)),
                  pl.BlockSpec((rows, CHUNK, d), lambda b, c: (b, jnp.minimum(2 * c + 2, last_chunk), 0)),
                  pl.BlockSpec((1, d), lambda b, c: (0, 0)),
                  vec(0), vec(1),
                  pl.BlockSpec((None, d, U_TOT), lambda b, c: (l, 0, 0), pipeline_mode=once)]
        + [full(a) for a in params],
        out_specs=pl.BlockSpec((rows, 2 * CHUNK, d_mix), lambda b, c: (b, c, 0)),
        out_shape=jax.ShapeDtypeStruct((bsz, seqlen, d_mix), BF16),
        scratch_shapes=scratch,
        compiler_params=pltpu.CompilerParams(
            dimension_semantics=("parallel", "arbitrary"), vmem_limit_bytes=MIXER_VMEM_LIMIT),
        name="mixers",
    )(x, x, gain.reshape(1, d), mod5, mod5, w_in_p, *params)


def _pad_w_in(w):
    z = lambda n: jnp.zeros(w.shape[:-1] + (n,), BF16)
    c1, c2 = HG_IN + RW_IN, HG_IN + RW_IN + ML_IN
    wb = w.astype(BF16)
    return jnp.concatenate(
        [wb[..., :c1], z(RW_PAD - RW_IN), wb[..., c1:c2], z(ML_PAD - ML_IN), wb[..., c2:]], axis=-1)


def kernel(x, c, norm_gain, mod_w, mod_b, ffn_w1, ffn_w3, ffn_w2, w_in, w_out, hg_lb_logits, hg_norm, rw_mu, rw_w0, rw_w2, rw_a0, rw_a2, rw_g2, rw_kk, rw_ka, rw_rk, rw_ln_w, rw_ln_b, ml_conv_w, ml_conv_b, ml_wq, ml_wk, ml_i_b, ml_f_b, ml_norm, ml_skip, lru_conv_w, lru_conv_b, lru_wa, lru_ba, lru_wx, lru_bx, lru_lambda, lru_norm, final_norm):
    bsz = x.shape[0]
    depth = mod_w.shape[0]
    lb_w = jax.nn.softmax(hg_lb_logits.astype(F32), axis=0)
    lower_bounds = jnp.cumsum(lb_w, axis=0) - lb_w[0]
    mod = _mod_call(c, mod_w, mod_b)
    mod5 = mod.reshape(depth, bsz, 3 * N_SUB, 1, D_MODEL)
    w1, w3, w2 = ffn_w1.astype(BF16), ffn_w3.astype(BF16), ffn_w2.astype(BF16)
    w_in_p, w_out_b = _pad_w_in(w_in), w_out.astype(BF16)
    for l in range(depth):
        x = _ffn_call(x, norm_gain[l, 0], mod5, l, 0, 0, w1, w3, w2)
        params = _mixer_params(
            l, lower_bounds, hg_norm, rw_mu, rw_w0, rw_w2, rw_a0, rw_a2, rw_g2, rw_kk, rw_ka, rw_rk,
            rw_ln_w, rw_ln_b, ml_conv_w, ml_conv_b, ml_wq, ml_wk, ml_i_b, ml_f_b, ml_norm, ml_skip,
            lru_conv_w, lru_conv_b, lru_wa, lru_ba, lru_wx, lru_bx, lru_lambda, lru_norm)
        y = _mixer_call(x, norm_gain[l, 1], mod5, l, 1, w_in_p, params)
        x = _outproj_call(x, y, mod5, l, 1, w_out_b)
        last = l == depth - 1
        x = _ffn_call(x, norm_gain[l, 2], mod5, l, 2, 1, w1, w3, w2,
                      final_gain=final_norm if last else None)
    return x
```

```python
import functools

import jax
import jax.numpy as jnp
from jax import lax
from jax.experimental import pallas as pl
from jax.experimental.pallas import tpu as pltpu

F32 = jnp.float32
BF16 = jnp.bfloat16

D_MODEL = 2048
DEPTH = 2
GROUP_W = 512
N_SUB = 3
D_FF = 5632
NORM_EPS = 1e-6
HG_HEADS = 4
RW_HEAD = 64
RW_HEADS = 8
RW_W_LORA = 32
RW_A_LORA = 32
RW_G_LORA = 96
RW_LN_EPS = 64e-5
RW_IN = 3 * GROUP_W + RW_W_LORA + RW_A_LORA + RW_G_LORA
ML_HEADS = 4
ML_DIM = 128
ML_IN = 3 * GROUP_W + 2 * ML_HEADS
NEG_BIG = -1e30
LRU_C = 8.0
HG_IN = 4 * GROUP_W
LRU_IN = 2 * GROUP_W

LANES = 128
MXU_TILE = 256
CHUNK = 64
MIX_ROWS = 2
SUB = 16
RW_PAD = 1792
ML_PAD = 1664
U_HG = 0
U_RW = U_HG + HG_IN
U_ML = U_RW + RW_PAD
U_LRU = U_ML + ML_PAD
U_TOT = 6656
LORA_PAD = RW_PAD - 3 * GROUP_W
VMEM_LIMIT = 56 * 1024 * 1024
MIXER_VMEM_LIMIT = 60 * 1024 * 1024
PROJ_TILE = 512
FFN_VMEM_LIMIT = 62 * 1024 * 1024
FFN_ROW_BLOCK = 128


def _sigmoid(x):
    return 0.5 * jnp.tanh(0.5 * x) + 0.5


def _silu(x):
    return x * _sigmoid(x)


def _softplus(x):
    return jnp.maximum(x, 0.0) + jnp.log(1.0 + jnp.exp(-jnp.abs(x)))


def _dot(a, b):
    return jnp.dot(a.astype(BF16), b.astype(BF16), preferred_element_type=F32)


def _dot_nt(a, b):
    return lax.dot_general(a.astype(BF16), b.astype(BF16), (((1,), (1,)), ((), ())),
                           preferred_element_type=F32)


def _dot_tn(a, b):
    return lax.dot_general(a.astype(BF16), b.astype(BF16), (((0,), (0,)), ((), ())),
                           preferred_element_type=F32)


def _cumsum_rows(x):
    rows = lax.broadcasted_iota(jnp.int32, x.shape, 0)
    d = 1
    while d < CHUNK:
        x = x + jnp.where(rows >= d, pltpu.roll(x, d, 0), 0.0)
        d *= 2
    return x


def _dot_halves(x, w_lo, w_hi):
    d = functools.partial(jnp.dot, preferred_element_type=F32)
    return jnp.concatenate([d(x[:, :MXU_TILE], w_lo), d(x[:, MXU_TILE:], w_hi)], axis=1)


def _seg_dot(x, seg, exact=False):
    hi = x.astype(BF16)
    out = _dot_halves(hi, seg, seg)
    if exact:
        out = out + _dot_halves((x - hi.astype(F32)).astype(BF16), seg, seg)
    return out


def _dot_bd(x, w_ref):
    return _dot_halves(x.astype(BF16), w_ref[0], w_ref[1])


HIST = 8


def _push_history(hist_ref, x):
    hist_ref[0:HIST, :] = hist_ref[CHUNK:CHUNK + HIST, :]
    hist_ref[HIST:HIST + CHUNK, :] = x


def _delayed(hist_ref, d):
    return hist_ref[HIST - d:HIST - d + CHUNK, :]


def _modulated_norm(x, gain, shift, scale):
    y = x * lax.rsqrt(jnp.mean(x * x, axis=-1, keepdims=True) + NORM_EPS) * gain
    return y * (1.0 + scale) + shift


def _mod_kernel(c_ref, w_ref, b_ref, o_ref):
    o_ref[...] = _dot(_silu(c_ref[...]), w_ref[...]) + b_ref[...]


def _mod_call(c, mod_w, mod_b):
    depth, d, n = mod_w.shape
    bsz = c.shape[0]
    tn = 1024
    return pl.pallas_call(
        _mod_kernel,
        grid=(depth, n // tn),
        in_specs=[pl.BlockSpec((bsz, d), lambda l, j: (0, 0)),
                  pl.BlockSpec((None, d, tn), lambda l, j: (l, 0, j)),
                  pl.BlockSpec((None, 1, tn), lambda l, j: (l, 0, j))],
        out_specs=pl.BlockSpec((None, bsz, tn), lambda l, j: (l, 0, j)),
        out_shape=jax.ShapeDtypeStruct((depth, bsz, n), F32),
        compiler_params=pltpu.CompilerParams(
            dimension_semantics=("arbitrary", "arbitrary"), vmem_limit_bytes=VMEM_LIMIT),
        name="adaln_mod",
    )(c, mod_w, mod_b.reshape(depth, 1, n))


def _ffn_kernel(x_ref, gain_ref, shift_ref, scale_ref, gate_ref, w1_ref, w3_ref, w2_ref, *rest,
                n_ff, final, given_h):
    rest = list(rest)
    h_ref = rest.pop(0) if given_h else rest.pop()
    fgain_ref = rest.pop(0) if final else None
    (o_ref,) = rest
    j = pl.program_id(2)

    def row_blocks(fn):
        def body(k, carry):
            fn(pl.ds(pl.multiple_of(k * FFN_ROW_BLOCK, FFN_ROW_BLOCK), FFN_ROW_BLOCK))
            return carry
        lax.fori_loop(0, x_ref.shape[0] // FFN_ROW_BLOCK, body, 0)

    if not given_h:
        @pl.when(j == 0)
        def _():
            def norm(rows):
                h = _modulated_norm(x_ref[rows, :], gain_ref[...], shift_ref[...], scale_ref[...])
                h_ref[rows, :] = h.astype(BF16)
            row_blocks(norm)

    h = h_ref[...]
    a = jnp.dot(h, w1_ref[...], preferred_element_type=F32)
    b = jnp.dot(h, w3_ref[...], preferred_element_type=F32)
    g = (_silu(a) * b).astype(BF16)
    acc = jnp.where(j == 0, 0.0, o_ref[...])
    o_ref[...] = acc + jnp.dot(g, w2_ref[...], preferred_element_type=F32)

    @pl.when(j == n_ff - 1)
    def _():
        def residual(rows):
            xn = x_ref[rows, :] + (0.5 * (1.0 + gate_ref[...])) * o_ref[rows, :]
            if final:
                xn = xn * lax.rsqrt(jnp.mean(xn * xn, axis=-1, keepdims=True) + NORM_EPS) * fgain_ref[...]
            o_ref[rows, :] = xn
        row_blocks(residual)


def _ffn_call(x, gain, mod5, l, sub, s, w1, w3, w2, final_gain=None, h=None, tm=1024, tf=512):
    bsz, seqlen, d = x.shape
    tm = min(tm, seqlen)
    n_ff = w1.shape[-1] // tf
    final = final_gain is not None
    vec = lambda k: pl.BlockSpec((None, None, None, 1, d), lambda b, i, j: (l, b, 3 * sub + k, 0, 0))
    in_specs = [pl.BlockSpec((None, tm, d), lambda b, i, j: (b, i, 0)),
                pl.BlockSpec((1, d), lambda b, i, j: (0, 0)),
                vec(0), vec(1), vec(2),
                pl.BlockSpec((None, None, d, tf), lambda b, i, j: (l, s, 0, j)),
                pl.BlockSpec((None, None, d, tf), lambda b, i, j: (l, s, 0, j)),
                pl.BlockSpec((None, None, tf, d), lambda b, i, j: (l, s, j, 0))]
    args = [x, gain.reshape(1, d), mod5, mod5, mod5, w1, w3, w2]
    if h is not None:
        in_specs.append(pl.BlockSpec((None, tm, d), lambda b, i, j: (b, i, 0)))
        args.append(h)
    if final:
        in_specs.append(pl.BlockSpec((1, d), lambda b, i, j: (0, 0)))
        args.append(final_gain.reshape(1, d))
    return pl.pallas_call(
        functools.partial(_ffn_kernel, n_ff=n_ff, final=final, given_h=h is not None),
        grid=(bsz, seqlen // tm, n_ff),
        in_specs=in_specs,
        out_specs=pl.BlockSpec((None, tm, d), lambda b, i, j: (b, i, 0)),
        out_shape=jax.ShapeDtypeStruct(x.shape, F32),
        scratch_shapes=[] if h is not None else [pltpu.VMEM((tm, d), BF16)],
        compiler_params=pltpu.CompilerParams(
            dimension_semantics=("parallel", "parallel", "arbitrary"), vmem_limit_bytes=FFN_VMEM_LIMIT),
        name="ffn_final" if final else "ffn",
    )(*args)


def _outproj_kernel(x_ref, y_ref, gate_ref, w_ref, gain_ref, shift_ref, scale_ref, o_ref, h_ref):
    xn = x_ref[...] + (1.0 + gate_ref[...]) * jnp.dot(
        y_ref[...], w_ref[...], preferred_element_type=F32)
    o_ref[...] = xn
    h_ref[...] = _modulated_norm(xn, gain_ref[...], shift_ref[...], scale_ref[...]).astype(BF16)


def _outproj_call(x, y, mod5, l, sub, w, next_gain, next_sub, tm=512):
    bsz, seqlen, d = x.shape
    tm = min(tm, seqlen)
    vec = lambda row: pl.BlockSpec((None, None, None, 1, d), lambda b, i: (l, b, row, 0, 0))
    tile = pl.BlockSpec((None, tm, d), lambda b, i: (b, i, 0))
    return pl.pallas_call(
        _outproj_kernel,
        grid=(bsz, seqlen // tm),
        in_specs=[tile,
                  pl.BlockSpec((None, tm, y.shape[-1]), lambda b, i: (b, i, 0)),
                  vec(3 * sub + 2),
                  pl.BlockSpec((None,) + w.shape[1:], lambda b, i: (l, 0, 0)),
                  pl.BlockSpec((1, d), lambda b, i: (0, 0)),
                  vec(3 * next_sub), vec(3 * next_sub + 1)],
        out_specs=[tile, tile],
        out_shape=[jax.ShapeDtypeStruct(x.shape, F32), jax.ShapeDtypeStruct(x.shape, BF16)],
        compiler_params=pltpu.CompilerParams(
            dimension_semantics=("parallel", "parallel"), vmem_limit_bytes=VMEM_LIMIT),
        name="out_proj",
    )(x, y, mod5, w, next_gain.reshape(1, d), mod5, mod5)


def _hgrn2_chunk(u_ref, lb, gnorm, st_ref, y_ref):
    gw = GROUP_W
    q = _silu(u_ref[:, U_HG:U_HG + gw])
    f_raw = u_ref[:, U_HG + gw:U_HG + 2 * gw]
    v = u_ref[:, U_HG + 2 * gw:U_HG + 3 * gw]
    g_raw = u_ref[:, U_HG + 3 * gw:U_HG + 4 * gw]
    e = jnp.exp(-f_raw)
    log_f = jnp.log(1.0 + lb * e) - jnp.log(1.0 + e)
    k = (1.0 - lb) * _sigmoid(-f_raw)
    yield
    bcum = _cumsum_rows(log_f)
    yield
    sub_row = lax.broadcasted_iota(jnp.int32, (SUB, 1), 0)
    for h in range(HG_HEADS):
        sl = slice(h * LANES, (h + 1) * LANES)
        qh, kh, vh, bh = q[:, sl], k[:, sl], v[:, sl], bcum[:, sl]
        st = st_ref[h]
        o_inter = _dot_nt(qh * jnp.exp(bh), st)
        pieces = []
        for i in range(CHUNK // SUB):
            r0 = i * SUB
            qd, kd, vd, bd = qh[r0:r0 + SUB], kh[r0:r0 + SUB], vh[r0:r0 + SUB], bh[r0:r0 + SUB]
            oi = jnp.zeros((SUB, LANES), F32)
            if i > 0:
                bref = bh[r0 - 1:r0]
                qi = qd * jnp.exp(bd - bref)
                kp = kh[0:r0] * jnp.exp(bref - bh[0:r0])
                oi = _dot(_dot_nt(qi, kp), vh[0:r0])
            for j in range(SUB):
                w = jnp.exp(jnp.minimum(bd - bd[j:j + 1], 0.0))
                col = jnp.sum(qd * w * kd[j:j + 1], axis=-1, keepdims=True)
                col = jnp.where(sub_row >= j, col, 0.0)
                oi = oi + col * vd[j:j + 1]
            pieces.append(oi)
            yield
        o = o_inter + jnp.concatenate(pieces, axis=0)
        b_last = bh[CHUNK - 1:CHUNK]
        st_ref[h] = st * jnp.exp(b_last) + _dot_tn(vh, kh * jnp.exp(b_last - bh))
        on = o * lax.rsqrt(jnp.mean(o * o, axis=-1, keepdims=True) + NORM_EPS) * gnorm[:, sl]
        y_ref[:, sl] = (on * _silu(g_raw[:, sl])).astype(y_ref.dtype)
        yield


def _rwkv7_chunk(u_ref, p, prev_ref, ht_ref, o_scr, y_ref):
    gw = GROUP_W
    u_raw = u_ref[:, U_RW:U_RW + RW_PAD]
    _push_history(prev_ref, u_raw)
    u = u_raw + (_delayed(prev_ref, 1) - u_raw) * p["mu"][...]
    yield
    r, k, v, lora = u[:, 0:gw], u[:, gw:2 * gw], u[:, 2 * gw:3 * gw], u[:, 3 * gw:RW_PAD]
    zw = p["w0"][...] + _dot(jnp.tanh(lora), p["w2"][...])
    lw = -jnp.exp(-_softplus(-zw) - 0.5)
    iclr = _sigmoid(p["a0"][...] + _dot(lora, p["a2"][...]))
    gate = _dot(_sigmoid(lora), p["g2"][...])
    yield
    kk = k * p["kk"][...]
    ss = _seg_dot(kk * kk, p["seg1"][...])
    kk = kk / jnp.maximum(jnp.sqrt(ss + 1e-12), 1e-6)
    k = k * (1.0 + (iclr - 1.0) * p["ka"][...])
    a_vec = -kk
    b_vec = kk * iclr
    yield

    lc = _cumsum_rows(lw)
    l_last = lc[CHUNK - 1:CHUNK]
    g_in = jnp.exp(lc)
    g_out = jnp.exp(-lc)
    g_end = jnp.exp(l_last - lc)
    a_s = a_vec * jnp.exp(lc - lw)
    r_s = r * g_in
    b_s = b_vec * g_out
    k_s = k * g_out
    b_e = b_vec * g_end
    k_e = k * g_end
    decay_end = jnp.exp(l_last)
    yield

    lane = lax.broadcasted_iota(jnp.int32, (1, LANES), 1)
    m_a = (lane < RW_HEAD).astype(F32)
    m_b = 1.0 - m_a
    stack = lambda t: jnp.concatenate([t * m_a, t * m_b], axis=0)
    n2 = 2 * CHUNK
    row = lax.broadcasted_iota(jnp.int32, (n2, n2), 0)
    col = lax.broadcasted_iota(jnp.int32, (n2, n2), 1)
    same = (row // CHUNK) == (col // CHUNK)
    strict = same & (row > col)
    incl = same & (row >= col)
    pairs = range(RW_HEADS // 2)
    sls = [slice(pr * LANES, (pr + 1) * LANES) for pr in pairs]
    bf = lambda t: t.astype(BF16)
    cat_k = lambda lhs, rhs: jnp.dot(jnp.concatenate(lhs, axis=1), jnp.concatenate(rhs, axis=0),
                                     preferred_element_type=F32)
    la = [stack(a_s[:, sl]) for sl in sls]
    lr = [stack(r_s[:, sl]) for sl in sls]
    v_st = [bf(stack(v[:, sl])) for sl in sls]
    hs = [ht_ref[pr] for pr in pairs]
    hs_b = [bf(m) for m in hs]
    sc = [_dot_nt(jnp.concatenate([la[pr], lr[pr]], axis=0),
                  jnp.concatenate([stack(b_s[:, sls[pr]]), stack(k_s[:, sls[pr]])], axis=0))
          for pr in pairs]
    yield
    pw = [jnp.where(strict, s[0:n2, 0:n2], 0.0) for s in sc]
    x = [cat_k([bf(jnp.where(strict, sc[pr][0:n2, n2:2 * n2], 0.0)), bf(la[pr])], [v_st[pr], hs_b[pr]])
         for pr in pairs]
    for step in range(6):
        x = [x[pr] + _dot(pw[pr], x[pr]) for pr in pairs]
        if step < 5:
            pw = [_dot(m, m) for m in pw]
        yield
    eye2 = row == col
    for pr in pairs:
        sl = sls[pr]
        a_rb = jnp.where(incl, sc[pr][n2:2 * n2, 0:n2], 0.0)
        a_rk = jnp.where(incl, sc[pr][n2:2 * n2, n2:2 * n2], 0.0)
        xb = bf(x[pr])
        o_st = cat_k([bf(lr[pr]), bf(a_rb), bf(a_rk)], [hs_b[pr], xb, v_st[pr]])
        o_scr[:, sl] = o_st[0:CHUNK] + o_st[CHUNK:n2]
        dec_col = jnp.sum(jnp.where(eye2, decay_end[:, sl], 0.0), axis=1, keepdims=True)
        ht_ref[pr] = hs[pr] * dec_col + lax.dot_general(
            jnp.concatenate([bf(stack(b_e[:, sl])), bf(stack(k_e[:, sl]))], axis=0),
            jnp.concatenate([xb, v_st[pr]], axis=0), (((0,), (0,)), ((), ())),
            preferred_element_type=F32)
        yield

    o = o_scr[...]
    mean = _seg_dot(o, p["segm"][...], exact=True)
    cen = o - mean
    var = _seg_dot(cen * cen, p["segm"][...])
    yield
    o = cen * lax.rsqrt(var + RW_LN_EPS) * p["ln_w"][...] + p["ln_b"][...]
    bonus = _seg_dot(r * k * p["rk"][...], p["seg1"][...]) * v
    y_ref[...] = ((o + bonus) * gate).astype(y_ref.dtype)


def _causal_conv4(x, hist_ref, w_ref, b_ref):
    _push_history(hist_ref, x)
    y = x * w_ref[3:4, :] + b_ref[...]
    for d in (1, 2, 3):
        y = y + _delayed(hist_ref, d) * w_ref[3 - d:4 - d, :]
    return y


def _mlstm_chunk(u_ref, p, prev_ref, m_ref, run_ref, y_ref):
    gw = GROUP_W
    xm = u_ref[:, U_ML:U_ML + gw]
    v = u_ref[:, U_ML + gw:U_ML + 2 * gw]
    o_raw = u_ref[:, U_ML + 2 * gw:U_ML + 3 * gw]
    gates = u_ref[:, U_ML + 3 * gw:U_ML + ML_PAD] + p["gate_b"][...]
    xc = _silu(_causal_conv4(xm, prev_ref, p["conv_w"], p["conv_b"]))
    q = _dot_bd(xc, p["wq"]) * (ML_DIM ** -0.5)
    k = _dot_bd(xc, p["wk"])
    yield
    log_f = jnp.minimum(gates, 0.0) - jnp.log(1.0 + jnp.exp(-jnp.abs(gates)))
    bcum = _cumsum_rows(log_f)
    row = lax.broadcasted_iota(jnp.int32, (CHUNK, CHUNK), 0)
    col = lax.broadcasted_iota(jnp.int32, (CHUNK, CHUNK), 1)
    causal = row >= col
    eye = row == col
    to_row = lambda c: jnp.sum(jnp.where(eye, c, 0.0), axis=0, keepdims=True)
    ones = jnp.ones((CHUNK, LANES), BF16)
    for h in range(ML_HEADS):
        sl = slice(h * LANES, (h + 1) * LANES)
        qh, kh, vh = q[:, sl], k[:, sl], v[:, sl]
        b_col = bcum[:, ML_HEADS + h:ML_HEADS + h + 1]
        i_col = gates[:, h:h + 1]
        m_prev = run_ref[h][0:1, 0:1]
        mem = m_ref[h]
        lw = b_col - to_row(b_col) + to_row(i_col)
        lprev = b_col + m_prev
        m_t = jnp.maximum(lprev, jnp.max(jnp.where(causal, lw, NEG_BIG), axis=-1, keepdims=True))
        wts = jnp.where(causal, jnp.exp(jnp.where(causal, lw - m_t, 0.0)), 0.0)
        s = _dot_nt(qh, kh) * wts
        wp = jnp.exp(lprev - m_t)
        yield
        v1 = jnp.concatenate([vh.astype(BF16), ones], axis=1)
        lhs = jnp.concatenate([(wp * qh).astype(BF16), s.astype(BF16)], axis=1)
        sv = jnp.dot(lhs, jnp.concatenate([mem.astype(BF16), v1], axis=0), preferred_element_type=F32)
        num = sv[:, 0:LANES]
        den = sv[:, LANES:LANES + 1]
        hh = num / jnp.maximum(jnp.abs(den), jnp.exp(-m_t))
        yield
        m_new = m_t[CHUNK - 1:CHUNK]
        b_end = b_col[CHUNK - 1:CHUNK]
        wl = jnp.exp(b_end - b_col + i_col - m_new)
        dec = jnp.exp(b_end + m_prev - m_new)
        kw = kh * wl
        m_ref[h] = dec * mem + lax.dot_general(kw.astype(BF16), v1, (((0,), (0,)), ((), ())),
                                               preferred_element_type=F32)
        run_ref[h] = jnp.broadcast_to(m_new, (8, LANES))
        hg = _sigmoid(o_raw[:, sl]) * hh
        hn = hg * lax.rsqrt(jnp.mean(hg * hg, axis=-1, keepdims=True) + NORM_EPS) * p["norm"][:, sl]
        y_ref[:, sl] = (hn + p["skip"][:, sl] * xc[:, sl]).astype(y_ref.dtype)
        yield


def _rglru_chunk(u_ref, p, prev_ref, h_ref, y_ref):
    gw = GROUP_W
    xb = u_ref[:, U_LRU:U_LRU + gw]
    gb = u_ref[:, U_LRU + gw:U_LRU + 2 * gw]
    xc = _causal_conv4(xb, prev_ref, p["conv_w"], p["conv_b"])
    r = _sigmoid(_dot_bd(xc, p["wa"]) + p["ba"][...])
    i = _sigmoid(_dot_bd(xc, p["wx"]) + p["bx"][...])
    yield
    log_a = -LRU_C * r * _softplus(-p["lam"][...])
    a = jnp.exp(log_a)
    bt = jnp.sqrt(1.0 - jnp.exp(2.0 * log_a)) * (i * xc)
    rows = lax.broadcasted_iota(jnp.int32, a.shape, 0)
    d = 1
    while d < CHUNK:
        keep = rows >= d
        bt = a * jnp.where(keep, pltpu.roll(bt, d, 0), 0.0) + bt
        a = a * jnp.where(keep, pltpu.roll(a, d, 0), 1.0)
        d *= 2
        yield
    hs = bt + a * h_ref[0:1, :]
    h_ref[...] = jnp.broadcast_to(hs[CHUNK - 1:CHUNK], h_ref.shape)
    gelu = 0.5 * gb * (1.0 + jnp.tanh(0.7978845608028654 * (gb + 0.044715 * (gb * gb * gb))))
    yl = hs * gelu
    y_ref[...] = (yl * lax.rsqrt(jnp.mean(yl * yl, axis=-1, keepdims=True) + NORM_EPS)
                  * p["norm"][...]).astype(y_ref.dtype)


def _round_robin(stages):
    live = [[g, 0, n] for g, n in stages]
    while live:
        entry = min(live, key=lambda e: (e[1] + 1) / e[2])
        try:
            next(entry[0])
            entry[1] += 1
        except StopIteration:
            live.remove(entry)


_HG_KEYS = ("lb", "gnorm")
_RW_KEYS = ("mu", "w0", "w2", "a0", "a2", "g2", "kk", "ka", "rk", "ln_w", "ln_b", "seg1", "segm")
_ML_KEYS = ("conv_w", "conv_b", "wq", "wk", "gate_b", "norm", "skip")
_LRU_KEYS = ("conv_w", "conv_b", "wa", "ba", "wx", "bx", "lam", "norm")


def _proj_stages(x_ref, t0, u_dst, gain, shift_ref, scale_ref, w_ref):
    rows = u_dst.shape[0]
    h = jnp.concatenate(
        [_modulated_norm(x_ref[r, t0:t0 + CHUNK, :], gain, shift_ref[r], scale_ref[r]).astype(BF16)
         for r in range(rows)], axis=0)
    yield
    for j in range(U_TOT // PROJ_TILE):
        cols = slice(j * PROJ_TILE, (j + 1) * PROJ_TILE)
        t = jnp.dot(h, w_ref[:, cols], preferred_element_type=F32)
        for r in range(rows):
            u_dst[r, :, cols] = t[r * CHUNK:(r + 1) * CHUNK]
        yield


def _mixer_kernel(*refs):
    it = iter(refs)
    x2_ref, xn_ref, gain_ref, shift_ref, scale_ref, w_ref = (next(it) for _ in range(6))
    hg = {k: next(it) for k in _HG_KEYS}
    rw = {k: next(it) for k in _RW_KEYS}
    ml = {k: next(it) for k in _ML_KEYS}
    lru = {k: next(it) for k in _LRU_KEYS}
    y_ref = next(it)
    (u_a, u_b, hg_st, rw_prev, rw_ht, rw_o, ml_prev, ml_m, ml_run, lru_prev, lru_h) = it
    proj = functools.partial(_proj_stages, gain=gain_ref[...], shift_ref=shift_ref,
                             scale_ref=scale_ref, w_ref=w_ref)

    @pl.when(pl.program_id(1) == 0)
    def _():
        for ref in (hg_st, rw_prev, rw_ht, ml_prev, ml_m, ml_run, lru_prev, lru_h):
            ref[...] = jnp.zeros_like(ref)
        for _ in proj(x2_ref, 0, u_a):
            pass

    gw = GROUP_W

    def mixers(u_buf, t0):
        stages = []
        for r in range(u_buf.shape[0]):
            u_r = u_buf.at[r]
            y_g = lambda g: y_ref.at[r, t0:t0 + CHUNK, g * gw:(g + 1) * gw]
            stages += [
                (_hgrn2_chunk(u_r, hg["lb"][...], hg["gnorm"][...], hg_st.at[r], y_g(0)),
                 2 + HG_HEADS * (CHUNK // SUB + 1)),
                (_rwkv7_chunk(u_r, rw, rw_prev.at[r], rw_ht.at[r], rw_o.at[r], y_g(1)),
                 12 + RW_HEADS // 2),
                (_mlstm_chunk(u_r, ml, ml_prev.at[r], ml_m.at[r], ml_run.at[r], y_g(2)),
                 1 + 3 * ML_HEADS),
                (_rglru_chunk(u_r, lru, lru_prev.at[r], lru_h.at[r], y_g(3)), 7)]
        return stages

    n_proj = 1 + U_TOT // PROJ_TILE
    _round_robin([(proj(x2_ref, CHUNK, u_b), n_proj)] + mixers(u_a, 0))
    _round_robin([(proj(xn_ref, 0, u_a), n_proj)] + mixers(u_b, CHUNK))


def _block_diag(blocks):
    n, a, b = blocks.shape
    tiled = jnp.tile(blocks.reshape(n * a, b), (1, n))
    same = (jnp.arange(n * a)[:, None] // a) == (jnp.arange(n * b)[None, :] // b)
    return jnp.where(same, tiled, jnp.zeros_like(tiled))


def _mixer_params(l, lower_bounds, hg_norm, rw_mu, rw_w0, rw_w2, rw_a0, rw_a2, rw_g2, rw_kk, rw_ka,
                  rw_rk, rw_ln_w, rw_ln_b, ml_conv_w, ml_conv_b, ml_wq, ml_wk, ml_i_b, ml_f_b,
                  ml_norm, ml_skip, lru_conv_w, lru_conv_b, lru_wa, lru_ba, lru_wx, lru_bx,
                  lru_lambda, lru_norm):
    gw = GROUP_W
    row = lambda a: a.reshape(1, -1).astype(F32)

    def lora_rows(w, start):
        z = jnp.zeros((LORA_PAD, gw), F32)
        return lax.dynamic_update_slice(z, w.astype(F32), (start, 0)).astype(BF16)

    head_id = jnp.arange(MXU_TILE) // RW_HEAD
    seg1 = (head_id[:, None] == head_id[None, :]).astype(F32)

    def diag_tiles(blocks):
        half = blocks.shape[0] // 2
        return jnp.stack([_block_diag(blocks[:half]), _block_diag(blocks[half:])]).astype(BF16)
    hg = dict(lb=row(lower_bounds[l]), gnorm=row(hg_norm[l]))
    rw = dict(mu=row(jnp.pad(rw_mu[l], (0, RW_PAD - RW_IN))), w0=row(rw_w0[l]),
              w2=lora_rows(rw_w2[l], 0), a0=row(rw_a0[l]),
              a2=lora_rows(rw_a2[l], RW_W_LORA), g2=lora_rows(rw_g2[l], RW_W_LORA + RW_A_LORA),
              kk=row(rw_kk[l]), ka=row(rw_ka[l]), rk=row(rw_rk[l]), ln_w=row(rw_ln_w[l]),
              ln_b=row(rw_ln_b[l]), seg1=seg1.astype(BF16), segm=(seg1 / RW_HEAD).astype(BF16))
    gate_b = jnp.concatenate([ml_i_b[l], ml_f_b[l], jnp.zeros((LANES - 2 * ML_HEADS,), F32)])
    ml = dict(conv_w=ml_conv_w[l].astype(F32), conv_b=row(ml_conv_b[l]),
              wq=diag_tiles(ml_wq[l]), wk=diag_tiles(ml_wk[l]),
              gate_b=row(gate_b), norm=row(ml_norm[l]), skip=row(ml_skip[l]))
    lru = dict(conv_w=lru_conv_w[l].astype(F32), conv_b=row(lru_conv_b[l]),
               wa=diag_tiles(lru_wa[l]), ba=row(lru_ba[l]),
               wx=diag_tiles(lru_wx[l]), bx=row(lru_bx[l]),
               lam=row(lru_lambda[l]), norm=row(lru_norm[l]))
    return ([hg[k] for k in _HG_KEYS] + [rw[k] for k in _RW_KEYS]
            + [ml[k] for k in _ML_KEYS] + [lru[k] for k in _LRU_KEYS])


def _mixer_call(x, gain, mod5, l, sub, w_in_p, params):
    bsz, seqlen, d = x.shape
    d_mix = 4 * GROUP_W
    rows = MIX_ROWS if bsz % MIX_ROWS == 0 else 1
    last_chunk = seqlen // CHUNK - 1
    once = pl.Buffered(1)
    full = lambda a: pl.BlockSpec(a.shape, lambda b, c: (0,) * a.ndim, pipeline_mode=once)
    vec = lambda k: pl.BlockSpec((None, rows, None, 1, d), lambda b, c: (l, b, 3 * sub + k, 0, 0))
    scratch = [
        pltpu.VMEM((rows, CHUNK, U_TOT), F32),
        pltpu.VMEM((rows, CHUNK, U_TOT), F32),
        pltpu.VMEM((rows, HG_HEADS, LANES, LANES), F32),
        pltpu.VMEM((rows, HIST + CHUNK, RW_PAD), F32),
        pltpu.VMEM((rows, RW_HEADS // 2, LANES, LANES), F32),
        pltpu.VMEM((rows, CHUNK, GROUP_W), F32),
        pltpu.VMEM((rows, HIST + CHUNK, GROUP_W), F32),
        pltpu.VMEM((rows, ML_HEADS, LANES, 2 * LANES), F32),
        pltpu.VMEM((rows, ML_HEADS, 8, LANES), F32),
        pltpu.VMEM((rows, HIST + CHUNK, GROUP_W), F32),
        pltpu.VMEM((rows, 8, GROUP_W), F32),
    ]
    return pl.pallas_call(
        _mixer_kernel,
        grid=(bsz // rows, seqlen // (2 * CHUNK)),
        in_specs=[pl.BlockSpec((rows, 2 * CHUNK, d), lambda b, c: (b, c, 0)),
                  pl.BlockSpec((rows, CHUNK, d), lambda b, c: (b, jnp.minimum(2 * c + 2, last_chunk), 0)),
                  pl.BlockSpec((1, d), lambda b, c: (0, 0)),
                  vec(0), vec(1),
                  pl.BlockSpec((None, d, U_TOT), lambda b, c: (l, 0, 0), pipeline_mode=once)]
        + [full(a) for a in params],
        out_specs=pl.BlockSpec((rows, 2 * CHUNK, d_mix), lambda b, c: (b, c, 0)),
        out_shape=jax.ShapeDtypeStruct((bsz, seqlen, d_mix), BF16),
        scratch_shapes=scratch,
        compiler_params=pltpu.CompilerParams(
            dimension_semantics=("parallel", "arbitrary"), vmem_limit_bytes=MIXER_VMEM_LIMIT),
        name="mixers",
    )(x, x, gain.reshape(1, d), mod5, mod5, w_in_p, *params)


def _pad_w_in(w):
    z = lambda n: jnp.zeros(w.shape[:-1] + (n,), BF16)
    c1, c2 = HG_IN + RW_IN, HG_IN + RW_IN + ML_IN
    wb = w.astype(BF16)
    return jnp.concatenate(
        [wb[..., :c1], z(RW_PAD - RW_IN), wb[..., c1:c2], z(ML_PAD - ML_IN), wb[..., c2:],
         z(U_TOT - U_LRU - LRU_IN)], axis=-1)


def kernel(x, c, norm_gain, mod_w, mod_b, ffn_w1, ffn_w3, ffn_w2, w_in, w_out, hg_lb_logits, hg_norm, rw_mu, rw_w0, rw_w2, rw_a0, rw_a2, rw_g2, rw_kk, rw_ka, rw_rk, rw_ln_w, rw_ln_b, ml_conv_w, ml_conv_b, ml_wq, ml_wk, ml_i_b, ml_f_b, ml_norm, ml_skip, lru_conv_w, lru_conv_b, lru_wa, lru_ba, lru_wx, lru_bx, lru_lambda, lru_norm, final_norm):
    bsz = x.shape[0]
    depth = mod_w.shape[0]
    lb_w = jax.nn.softmax(hg_lb_logits.astype(F32), axis=0)
    lower_bounds = jnp.cumsum(lb_w, axis=0) - lb_w[0]
    mod = _mod_call(c, mod_w, mod_b)
    mod5 = mod.reshape(depth, bsz, 3 * N_SUB, 1, D_MODEL)
    w1, w3, w2 = ffn_w1.astype(BF16), ffn_w3.astype(BF16), ffn_w2.astype(BF16)
    w_in_p, w_out_b = _pad_w_in(w_in), w_out.astype(BF16)
    for l in range(depth):
        x = _ffn_call(x, norm_gain[l, 0], mod5, l, 0, 0, w1, w3, w2)
        params = _mixer_params(
            l, lower_bounds, hg_norm, rw_mu, rw_w0, rw_w2, rw_a0, rw_a2, rw_g2, rw_kk, rw_ka, rw_rk,
            rw_ln_w, rw_ln_b, ml_conv_w, ml_conv_b, ml_wq, ml_wk, ml_i_b, ml_f_b, ml_norm, ml_skip,
            lru_conv_w, lru_conv_b, lru_wa, lru_ba, lru_wx, lru_bx, lru_lambda, lru_norm)
        y = _mixer_call(x, norm_gain[l, 1], mod5, l, 1, w_in_p, params)
        x, h = _outproj_call(x, y, mod5, l, 1, w_out_b, norm_gain[l, 2], 2)
        last = l == depth - 1
        x = _ffn_call(x, norm_gain[l, 2], mod5, l, 2, 1, w1, w3, w2,
                      final_gain=final_norm if last else None, h=h)
    return x
```
